```python
import math
import jax, jax.numpy as jnp
from jax import lax
import numpy as np

D_MODEL = 1024
BATCH = 8
SEQ = 4096
DEPTH = 4

N_META = 16
D_FF = 2816
EPS = 1e-6
Q_BLOCK = 128
A_HEADS = 8
A_HEAD_DIM = 64
A_WIDTH = A_HEADS * A_HEAD_DIM
IDX_HEADS = 8
IDX_DIM = 64
TOPK_MAX = 256
B_HEADS = 4
B_QK_DIM = 64
B_V_DIM = 2 * B_QK_DIM
B_WIDTH = B_HEADS * B_V_DIM
MIX_WIDTH = A_WIDTH + B_WIDTH
ATTN_SPLIT_SIZES = (A_WIDTH, A_WIDTH, A_WIDTH, IDX_HEADS * IDX_DIM, IDX_DIM, IDX_HEADS,
                    B_HEADS * 2 * B_QK_DIM, B_HEADS * 2 * B_QK_DIM, B_WIDTH)
ATTN_IN = 3 * A_WIDTH + IDX_HEADS * IDX_DIM + IDX_DIM + IDX_HEADS + 2 * B_HEADS * 2 * B_QK_DIM + B_WIDTH
LRU_WIDTH = D_MODEL
LRU_BLOCKS = 4
LRU_BLOCK_W = LRU_WIDTH // LRU_BLOCKS
CONV_W = 4
RG_C = 8.0
N_EVEN = (DEPTH + 1) // 2
N_ODD = DEPTH // 2

kernel_name = 'hybrid_dsa_diff_rglru_macaron'


def rms_norm(x, g):
    xf = x.astype(jnp.float32)
    y = xf * lax.rsqrt(jnp.mean(xf * xf, axis=-1, keepdims=True) + EPS)
    return (y * g.astype(jnp.float32)).astype(x.dtype)


def layer_norm(x, g, b):
    xf = x.astype(jnp.float32)
    mu = jnp.mean(xf, axis=-1, keepdims=True)
    xc = xf - mu
    y = xc * lax.rsqrt(jnp.mean(xc * xc, axis=-1, keepdims=True) + EPS)
    return (y * g.astype(jnp.float32) + b.astype(jnp.float32)).astype(x.dtype)


def swiglu(x, w_gu, w_down):
    g, u = jnp.split(x @ w_gu, 2, axis=-1)
    return (jax.nn.silu(g) * u) @ w_down


def to_blocks(a):
    b, s = a.shape[:2]
    a = a.reshape((b, s // Q_BLOCK, Q_BLOCK) + a.shape[2:])
    return jnp.moveaxis(a, 1, 0)


def from_blocks(a):
    a = jnp.moveaxis(a, 0, 1)
    return a.reshape((a.shape[0], a.shape[1] * a.shape[2]) + a.shape[3:])


def dense_causal(q, k, v, mask, scale):
    s = jnp.einsum('bqhd,bkhd->bhqk', q, k).astype(jnp.float32) * scale
    s = jnp.where(mask[None, None], s, -jnp.inf)
    p = jax.nn.softmax(s, axis=-1).astype(v.dtype)
    return jnp.einsum('bhqk,bkhd->bqhd', p, v)


def dsa_attention(q, k, v, q_idx, k_idx, w_idx):
    t_len = q.shape[1]
    s_len = t_len - N_META
    top_k = min(TOPK_MAX, s_len // 4)
    scale = A_HEAD_DIM ** -0.5
    k_meta, v_meta = k[:, :N_META], v[:, :N_META]
    kv_real = jnp.concatenate([k[:, N_META:], v[:, N_META:]], axis=-1)
    k_idx_real = k_idx[:, N_META:]
    meta_mask = jnp.tril(jnp.ones((N_META, N_META), dtype=bool))
    out_meta = dense_causal(q[:, :N_META], k_meta, v_meta, meta_mask, scale)
    key_pos = jnp.arange(s_len)

    def block(args):
        qb, qib, wib, start = args
        logits = jnp.einsum('bqhd,bsd->bqhs', qib, k_idx_real)
        score = jnp.einsum('bqh,bqhs->bqs', wib, jax.nn.relu(logits)).astype(jnp.float32)
        q_pos = start + jnp.arange(Q_BLOCK)
        causal = key_pos[None, :] <= q_pos[:, None]
        score = jnp.where(causal[None], score, -jnp.inf)
        sel_score, sel_idx = lax.top_k(score, top_k)
        valid = jnp.isfinite(sel_score)
        kv_sel = jax.vmap(lambda kv_b, ix_b: kv_b[ix_b])(kv_real, sel_idx)
        k_sel, v_sel = jnp.split(kv_sel, 2, axis=-1)
        s_meta = jnp.einsum('bqhd,bmhd->bhqm', qb, k_meta).astype(jnp.float32)
        s_sel = jnp.einsum('bqhd,bqkhd->bhqk', qb, k_sel).astype(jnp.float32)
        s_sel = jnp.where(valid[:, None], s_sel, -jnp.inf)
        p = jax.nn.softmax(jnp.concatenate([s_meta, s_sel], axis=-1) * scale, axis=-1).astype(v.dtype)
        return (jnp.einsum('bhqm,bmhd->bqhd', p[..., :N_META], v_meta)
                + jnp.einsum('bhqk,bqkhd->bqhd', p[..., N_META:], v_sel))

    starts = jnp.arange(s_len // Q_BLOCK, dtype=jnp.int32) * Q_BLOCK
    out_real = from_blocks(lax.map(block, (to_blocks(q[:, N_META:]), to_blocks(q_idx[:, N_META:]),
                                           to_blocks(w_idx[:, N_META:]), starts)))
    return jnp.concatenate([out_meta, out_real], axis=1)


def diff_attention(q, k, v, lam, lam_init, subln_g):
    t_len = q.shape[1]
    s_len = t_len - N_META
    scale = B_QK_DIM ** -0.5

    def attend(qb, mask):
        n_keys = mask.shape[1]
        s = jnp.einsum('bqhcd,bkhcd->bchqk', qb, k[:, :n_keys]).astype(jnp.float32) * scale
        s = jnp.where(mask, s, -jnp.inf)
        p = jax.nn.softmax(s, axis=-1)
        a = p[:, 0] - lam * p[:, 1]
        return jnp.einsum('bhqk,bkhd->bqhd', a.astype(v.dtype), v[:, :n_keys])

    out_meta = attend(q[:, :N_META], jnp.tril(jnp.ones((N_META, N_META), dtype=bool)))
    key_pos = jnp.arange(t_len)

    def block(args):
        qb, start = args
        q_pos = N_META + start + jnp.arange(Q_BLOCK)
        return attend(qb, key_pos[None, :] <= q_pos[:, None])

    starts = jnp.arange(s_len // Q_BLOCK, dtype=jnp.int32) * Q_BLOCK
    out_real = from_blocks(lax.map(block, (to_blocks(q[:, N_META:]), starts)))
    o = jnp.concatenate([out_meta, out_real], axis=1)
    return rms_norm(o, subln_g) * (1.0 - lam_init)


def attn_mixer(h, w_in, idx_ln_g, idx_ln_b, lam_params, subln_g, w_out, lam_init):
    b, t, _ = h.shape
    split_at = [int(c) for c in np.cumsum(ATTN_SPLIT_SIZES)[:-1]]
    qa, ka, va, qi, ki, wi, qb, kb, vb = jnp.split(h @ w_in, split_at, axis=-1)
    qa = qa.reshape(b, t, A_HEADS, A_HEAD_DIM)
    ka = ka.reshape(b, t, A_HEADS, A_HEAD_DIM)
    va = va.reshape(b, t, A_HEADS, A_HEAD_DIM)
    qi = qi.reshape(b, t, IDX_HEADS, IDX_DIM)
    ki = layer_norm(ki, idx_ln_g, idx_ln_b)
    wi = wi * (IDX_HEADS ** -0.5 * IDX_DIM ** -0.5)
    out_a = dsa_attention(qa, ka, va, qi, ki, wi).reshape(b, t, A_WIDTH)
    lp = lam_params.astype(jnp.float32)
    lam = jnp.exp(jnp.sum(lp[0] * lp[1])) - jnp.exp(jnp.sum(lp[2] * lp[3])) + lam_init
    qb = qb.reshape(b, t, B_HEADS, 2, B_QK_DIM)
    kb = kb.reshape(b, t, B_HEADS, 2, B_QK_DIM)
    vb = vb.reshape(b, t, B_HEADS, B_V_DIM)
    out_b = diff_attention(qb, kb, vb, lam, lam_init, subln_g).reshape(b, t, B_WIDTH)
    return jnp.concatenate([out_a, out_b], axis=-1) @ w_out


def causal_conv(x, w, bias):
    c = x.shape[-1]
    y = lax.conv_general_dilated(x, w[:, None, :], window_strides=(1,), padding=[(CONV_W - 1, 0)],
                                 dimension_numbers=('NWC', 'WIO', 'NWC'), feature_group_count=c)
    return y + bias


def rglru_mixer(h, w_in, conv_w, conv_b, gate_w, gate_b, lru_lambda, w_out):
    b, t, _ = h.shape
    y_br, x_br = jnp.split(h @ w_in, 2, axis=-1)
    y_br = jax.nn.gelu(y_br, approximate=True)
    xc = causal_conv(x_br, conv_w, conv_b)
    xb = xc.reshape(b, t, LRU_BLOCKS, LRU_BLOCK_W)
    gates = jnp.einsum('btni,gnij->gbtnj', xb, gate_w).reshape(2, b, t, LRU_WIDTH)
    gates = (gates + gate_b[:, None, None, :]).astype(jnp.float32)
    gate_x = jax.nn.sigmoid(gates[0])
    gate_a = jax.nn.sigmoid(gates[1])
    log_a = RG_C * gate_a * jax.nn.log_sigmoid(lru_lambda.astype(jnp.float32))
    a = jnp.exp(log_a)
    mult = jnp.sqrt(-jnp.expm1(2.0 * log_a))
    u = mult * (gate_x * xc.astype(jnp.float32))

    def combine(left, right):
        a_l, b_l = left
        a_r, b_r = right
        return a_l * a_r, a_r * b_l + b_r

    _, hs = lax.associative_scan(combine, (a, u), axis=1)
    return (hs.astype(h.dtype) * y_br) @ w_out


def setup_inputs(seed: int = 0) -> dict:
    key = jax.random.key(seed)
    ks = jax.random.split(key, 20)
    f32 = jnp.float32
    nrm = lambda k, shape, s: jax.random.normal(k, shape, f32) * s
    a_c = jax.random.uniform(ks[17], (N_ODD, LRU_WIDTH), f32, 0.9, 0.999)
    sig = a_c ** (1.0 / RG_C)
    return {
        'x': nrm(ks[0], (BATCH, SEQ, D_MODEL), 1.0),
        'meta_tokens': nrm(ks[1], (N_META, D_MODEL), 1.0),
        'norm_g': 1.0 + nrm(ks[2], (DEPTH, 3, D_MODEL), 0.02),
        'ffn_w_gu': nrm(ks[3], (DEPTH, 2, D_MODEL, 2 * D_FF), D_MODEL ** -0.5),
        'ffn_w_down': nrm(ks[4], (DEPTH, 2, D_FF, D_MODEL), D_FF ** -0.5),
        'attn_w_in': nrm(ks[5], (N_EVEN, D_MODEL, ATTN_IN), D_MODEL ** -0.5),
        'idx_k_ln_g': 1.0 + nrm(ks[6], (N_EVEN, IDX_DIM), 0.02),
        'idx_k_ln_b': nrm(ks[7], (N_EVEN, IDX_DIM), 0.02),
        'diff_lambda': nrm(ks[8], (N_EVEN, 4, B_QK_DIM), 0.1),
        'diff_subln_g': 1.0 + nrm(ks[9], (N_EVEN, B_V_DIM), 0.02),
        'attn_w_out': nrm(ks[10], (N_EVEN, MIX_WIDTH, D_MODEL), MIX_WIDTH ** -0.5),
        'rec_w_in': nrm(ks[11], (N_ODD, D_MODEL, 2 * LRU_WIDTH), D_MODEL ** -0.5),
        'rec_conv_w': nrm(ks[12], (N_ODD, CONV_W, LRU_WIDTH), CONV_W ** -0.5),
        'rec_conv_b': nrm(ks[13], (N_ODD, LRU_WIDTH), 0.02),
        'rec_gate_w': nrm(ks[14], (N_ODD, 2, LRU_BLOCKS, LRU_BLOCK_W, LRU_BLOCK_W), LRU_BLOCK_W ** -0.5),
        'rec_gate_b': nrm(ks[15], (N_ODD, 2, LRU_WIDTH), 0.02),
        'rec_lambda': jnp.log(sig) - jnp.log1p(-sig),
        'rec_w_out': nrm(ks[16], (N_ODD, LRU_WIDTH, D_MODEL), LRU_WIDTH ** -0.5),
        'final_norm_g': 1.0 + nrm(ks[18], (D_MODEL,), 0.02),
    }


def reference(x, meta_tokens, norm_g, ffn_w_gu, ffn_w_down, attn_w_in, idx_k_ln_g, idx_k_ln_b,
              diff_lambda, diff_subln_g, attn_w_out, rec_w_in, rec_conv_w, rec_conv_b, rec_gate_w,
              rec_gate_b, rec_lambda, rec_w_out, final_norm_g):
    b = x.shape[0]
    meta = jnp.broadcast_to(meta_tokens[None].astype(x.dtype), (b, N_META, D_MODEL))
    h = jnp.concatenate([meta, x], axis=1)
    for i in range(DEPTH):
        j = i // 2
        h = h + 0.5 * swiglu(rms_norm(h, norm_g[i, 0]), ffn_w_gu[i, 0], ffn_w_down[i, 0])
        hn = rms_norm(h, norm_g[i, 1])
        if i % 2 == 0:
            lam_init = 0.8 - 0.6 * math.exp(-0.3 * i)
            h = h + attn_mixer(hn, attn_w_in[j], idx_k_ln_g[j], idx_k_ln_b[j], diff_lambda[j],
                               diff_subln_g[j], attn_w_out[j], lam_init)
        else:
            h = h + rglru_mixer(hn, rec_w_in[j], rec_conv_w[j], rec_conv_b[j], rec_gate_w[j],
                                rec_gate_b[j], rec_lambda[j], rec_w_out[j])
        h = h + 0.5 * swiglu(rms_norm(h, norm_g[i, 2]), ffn_w_gu[i, 1], ffn_w_down[i, 1])
    return rms_norm(h[:, N_META:], final_norm_g)
```

```python
import functools
import math

import jax
import jax.numpy as jnp
from jax import lax
from jax.experimental import pallas as pl
from jax.experimental.pallas import tpu as pltpu

F32 = jnp.float32
BF16 = jnp.bfloat16
I32 = jnp.int32

D = 1024
N_META = 16
D_FF = 2816
EPS = 1e-6
HEAD_W = 512
IDX_DIM = 64
IDX_HEADS = 8
TOPK_MAX = 256
LRU_BLOCKS = 4
LRU_BLOCK_W = D // LRU_BLOCKS
CONV_W = 4
RG_C = 8.0

TM = 512
TQ = 128
TK = 512
FF_CHUNK = 1408
NEG = -1e30
INT_MIN = -2 ** 31
VMEM_LIMIT = 56 * 1024 * 1024

_NT = (((1,), (1,)), ((), ()))


def _params():
    return pltpu.CompilerParams(dimension_semantics=("arbitrary",), vmem_limit_bytes=VMEM_LIMIT)


def _resident(shape, index_map):
    return pl.BlockSpec(shape, index_map, pipeline_mode=pl.Buffered(1))


def _rms(x, g):
    return x * lax.rsqrt(jnp.mean(x * x, axis=-1, keepdims=True) + EPS) * g


def _sigmoid(x):
    return 1.0 / (1.0 + jnp.exp(-x))


def _ffn_kernel(x_ref, g_ref, wg_ref, wu_ref, wd_ref, *rest, final):
    o_ref = rest[-1]
    x = x_ref[...]
    xn = _rms(x, g_ref[...]).astype(BF16)
    acc = None
    for c in range(D_FF // FF_CHUNK):
        sl = slice(c * FF_CHUNK, (c + 1) * FF_CHUNK)
        g = jnp.dot(xn, wg_ref[:, sl], preferred_element_type=F32)
        u = jnp.dot(xn, wu_ref[:, sl], preferred_element_type=F32)
        a = (g * _sigmoid(g) * u).astype(BF16)
        part = jnp.dot(a, wd_ref[sl, :], preferred_element_type=F32)
        acc = part if acc is None else acc + part
    y = x + 0.5 * acc
    if final:
        y = _rms(y, rest[0][...])
    o_ref[...] = y


def _ffn(h, g, w_gu, w_down, final_g=None, n_rows=None):
    n_rows = h.shape[0] if n_rows is None else n_rows
    final = final_g is not None
    in_specs = [
        pl.BlockSpec((TM, D), lambda i: (i, 0)),
        _resident((1, D), lambda i: (0, 0)),
        _resident((D, D_FF), lambda i: (0, 0)),
        _resident((D, D_FF), lambda i: (0, 1)),
        _resident((D_FF, D), lambda i: (0, 0)),
    ]
    args = [h, g.reshape(1, D), w_gu, w_gu, w_down]
    if final:
        in_specs.append(_resident((1, D), lambda i: (0, 0)))
        args.append(final_g.reshape(1, D))
    return pl.pallas_call(
        functools.partial(_ffn_kernel, final=final),
        grid=(n_rows // TM,),
        in_specs=in_specs,
        out_specs=pl.BlockSpec((TM, D), lambda i: (i, 0)),
        out_shape=jax.ShapeDtypeStruct((n_rows, D), F32),
        compiler_params=_params(),
        name="ffn",
    )(*args)


N_PROJ = 5 * HEAD_W


def _attn_proj_kernel(x_ref, g_ref, w_ref, wki_ref, wwt_ref, lng_ref, lnb_ref,
                      proj_ref, ki_ref, wit_ref, vat_ref, vbt_ref):
    xn = _rms(x_ref[...], g_ref[...]).astype(BF16)
    y = jnp.dot(xn, w_ref[...], preferred_element_type=F32)
    scale = IDX_DIM ** -0.5
    proj_ref[:, 0:HEAD_W] = (y[:, 0:HEAD_W] * scale).astype(BF16)
    proj_ref[:, HEAD_W:3 * HEAD_W] = y[:, HEAD_W:3 * HEAD_W].astype(BF16)
    proj_ref[:, 3 * HEAD_W:4 * HEAD_W] = (y[:, 3 * HEAD_W:4 * HEAD_W] * scale).astype(BF16)
    proj_ref[:, 4 * HEAD_W:5 * HEAD_W] = y[:, 4 * HEAD_W:5 * HEAD_W].astype(BF16)
    vat_ref[0] = y[:, 5 * HEAD_W:6 * HEAD_W].T.astype(BF16)
    vbt_ref[0] = y[:, 6 * HEAD_W:7 * HEAD_W].T.astype(BF16)
    yk = jnp.dot(xn, wki_ref[...], preferred_element_type=F32)
    mu = jnp.mean(yk, axis=-1, keepdims=True)
    yc = yk - mu
    ln = yc * lax.rsqrt(jnp.mean(yc * yc, axis=-1, keepdims=True) + EPS)
    ki_ref[...] = (ln * lng_ref[...] + lnb_ref[...]).astype(BF16)
    wit = lax.dot_general(wwt_ref[...], xn, _NT, preferred_element_type=F32)
    wit_ref[...] = wit * (IDX_HEADS ** -0.5 * IDX_DIM ** -0.5)


def _attn_proj(h, g, w_main, w_ki, w_wt, ln_g, ln_b):
    n = h.shape[0]
    nt = n // TM
    const = lambda i: (0, 0)
    return pl.pallas_call(
        _attn_proj_kernel,
        grid=(nt,),
        in_specs=[
            pl.BlockSpec((TM, D), lambda i: (i, 0)),
            _resident((1, D), const),
            _resident((D, 7 * HEAD_W), const),
            _resident((D, IDX_DIM), const),
            _resident((16, D), const),
            _resident((1, IDX_DIM), const),
            _resident((1, IDX_DIM), const),
        ],
        out_specs=[
            pl.BlockSpec((TM, N_PROJ), lambda i: (i, 0)),
            pl.BlockSpec((TM, IDX_DIM), lambda i: (i, 0)),
            pl.BlockSpec((16, TM), lambda i: (0, i)),
            pl.BlockSpec((1, HEAD_W, TM), lambda i: (i, 0, 0)),
            pl.BlockSpec((1, HEAD_W, TM), lambda i: (i, 0, 0)),
        ],
        out_shape=[
            jax.ShapeDtypeStruct((n, N_PROJ), BF16),
            jax.ShapeDtypeStruct((n, IDX_DIM), BF16),
            jax.ShapeDtypeStruct((16, n), F32),
            jax.ShapeDtypeStruct((nt, HEAD_W, TM), BF16),
            jax.ShapeDtypeStruct((nt, HEAD_W, TM), BF16),
        ],
        compiler_params=_params(),
        name="attn_proj",
    )(h, g.reshape(1, D), w_main, w_ki, w_wt, ln_g.reshape(1, IDX_DIM), ln_b.reshape(1, IDX_DIM))


def _step_geometry(g, seq):
    nq_seq = seq // TQ
    n_real = pl.num_programs(0) - TM // TQ
    is_real = g < n_real
    i = g % nq_seq
    n_ch = jnp.where(is_real, (i * TQ + TQ + TK - 1) // TK, 0)
    return is_real, i, n_ch


def _stack_pair(q_pair):
    lane = lax.broadcasted_iota(I32, (TQ, 128), 1)
    qf = q_pair.astype(F32)
    top = jnp.where(lane < 64, qf, 0.0).astype(BF16)
    bot = jnp.where(lane >= 64, qf, 0.0).astype(BF16)
    return jnp.concatenate([top, bot], axis=0)


def _meta_bias(is_real):
    j = lax.broadcasted_iota(I32, (128, 2 * TQ), 0)
    t = lax.broadcasted_iota(I32, (128, 2 * TQ), 1) % TQ
    real = is_real.astype(I32)
    lim = jnp.minimum(real * N_META + (1 - real) * (t + 1), N_META)
    return jnp.where(j < lim, 0.0, NEG)


def _bcast_halves(row, n_sub):
    return jnp.concatenate([jnp.broadcast_to(row[:, :TQ], (n_sub, TQ)),
                            jnp.broadcast_to(row[:, TQ:], (n_sub, TQ))], axis=0)


def _dsa_kernel(qa_ref, qi_ref, wit_ref, ki_ref, ka_ref, vat_ref, km_ref, vmt_ref, o_ref,
                qall_ref, qstk_ref, key_ref, bias_ref, tri_ref, *, seq, top_k):
    g = pl.program_id(0)
    is_real, i, n_ch = _step_geometry(g, seq)
    q_pos = i * TQ + lax.broadcasted_iota(I32, (TK, TQ), 1)
    k_off = lax.broadcasted_iota(I32, (TK, TQ), 0)

    qi = qi_ref[...]
    for h in range(IDX_HEADS):
        qall_ref[h * TQ:(h + 1) * TQ, :] = qi[:, h * IDX_DIM:(h + 1) * IDX_DIM]
    for p in range(4):
        qstk_ref[p] = _stack_pair(qa_ref[:, p * 128:(p + 1) * 128])
    r = lax.broadcasted_iota(I32, (TK, TK), 0)
    c_ = lax.broadcasted_iota(I32, (TK, TK), 1)
    tri_ref[...] = jnp.where(c_ <= r, 1.0, 0.0).astype(BF16)

    wt = wit_ref[...]

    def score_body(c, carry):
        kc = ki_ref[pl.ds(pl.multiple_of(c * TK, TK), TK), :]
        logits = lax.dot_general(kc, qall_ref[...], _NT, preferred_element_type=F32)
        sc = jnp.zeros((TK, TQ), F32)
        for h in range(IDX_HEADS):
            sc = sc + wt[h:h + 1, :] * jnp.maximum(logits[:, h * TQ:(h + 1) * TQ], 0.0)
        sc = jnp.where(sc == 0.0, 0.0, sc)
        bits = pltpu.bitcast(sc, I32)
        key = bits ^ (lax.shift_right_arithmetic(bits, 31) & 0x7FFFFFFF)
        key_ref[c] = jnp.where(c * TK + k_off <= q_pos, key, INT_MIN)
        return carry

    lax.fori_loop(0, n_ch, score_body, 0)

    def count(pred_fn):
        def body(c, acc):
            ones = jnp.where(pred_fn(key_ref[c]), 1, 0)
            return acc + jnp.sum(ones.reshape(TK // 8, 8, TQ), axis=0)
        acc = lax.fori_loop(0, n_ch, body, jnp.zeros((8, TQ), I32))
        return jnp.sum(acc, axis=0, keepdims=True)

    def bit_body(it, ans):
        try_u = ans | lax.shift_left(jnp.int32(1), 31 - it)
        try_s = try_u ^ INT_MIN
        cnt = count(lambda k: k >= try_s)
        return jnp.where(cnt >= top_k, try_u, ans)

    ans = lax.fori_loop(0, 32, bit_body, jnp.zeros((1, TQ), I32))
    thr = ans ^ INT_MIN
    need = (top_k - count(lambda k: k > thr)).astype(F32)

    def bias_body(c, run):
        k = key_ref[c]
        eq = k == thr
        pre = jnp.dot(tri_ref[...], jnp.where(eq, 1.0, 0.0).astype(BF16), preferred_element_type=F32)
        rank = run + pre
        b = jnp.where(k > thr, 0.0, jnp.where(eq, jnp.where(rank <= need, 0.0, NEG), NEG))
        bias_ref[c] = jnp.where(k == INT_MIN, NEG, b)
        return run + pre[TK - 1:TK, :]

    lax.fori_loop(0, n_ch, bias_body, jnp.zeros((1, TQ), F32))

    mbias = _meta_bias(is_real)
    for p in range(4):
        ps = slice(p * 128, (p + 1) * 128)
        qs = qstk_ref[p]
        st = lax.dot_general(km_ref[:, ps], qs, _NT, preferred_element_type=F32) + mbias
        m = jnp.max(st, axis=0, keepdims=True)
        pm = jnp.exp(st - m)
        l = jnp.sum(pm, axis=0, keepdims=True)
        ot = jnp.dot(vmt_ref[0, ps, 0:128], pm.astype(BF16), preferred_element_type=F32)
        acc = jnp.concatenate([ot[:64, :TQ], ot[64:, TQ:]], axis=0)

        def body(c, carry):
            m, l, acc = carry
            kc = ka_ref[pl.ds(pl.multiple_of(c * TK, TK), TK), ps]
            b = bias_ref[c]
            st = lax.dot_general(kc, qs, _NT, preferred_element_type=F32) + jnp.concatenate([b, b], axis=1)
            m_new = jnp.maximum(m, jnp.max(st, axis=0, keepdims=True))
            alpha = jnp.exp(m - m_new)
            pc = jnp.exp(st - m_new)
            l = alpha * l + jnp.sum(pc, axis=0, keepdims=True)
            ot = jnp.dot(vat_ref[c, ps, :], pc.astype(BF16), preferred_element_type=F32)
            acc = acc * _bcast_halves(alpha, 64) + jnp.concatenate([ot[:64, :TQ], ot[64:, TQ:]], axis=0)
            return m_new, l, acc

        m, l, acc = lax.fori_loop(0, n_ch, body, (m, l, acc))
        o_ref[:, ps] = (acc * _bcast_halves(1.0 / l, 64)).T.astype(BF16)


def _dsa(proj, ki, wit, vat, batch, seq):
    n = proj.shape[0]
    nq_seq = seq // TQ
    n_real_q = batch * nq_seq
    nch = seq // TK
    top_k = min(TOPK_MAX, seq // 4)
    bidx = lambda g: jnp.minimum(g // nq_seq, batch - 1)
    meta_q = n_real_q
    meta_tile = batch * nch
    return pl.pallas_call(
        functools.partial(_dsa_kernel, seq=seq, top_k=top_k),
        grid=(n // TQ,),
        in_specs=[
            pl.BlockSpec((TQ, HEAD_W), lambda g: (g, 0)),
            pl.BlockSpec((TQ, HEAD_W), lambda g: (g, 2)),
            pl.BlockSpec((16, TQ), lambda g: (0, g)),
            pl.BlockSpec((seq, IDX_DIM), lambda g: (bidx(g), 0)),
            pl.BlockSpec((seq, HEAD_W), lambda g: (bidx(g), 1)),
            pl.BlockSpec((nch, HEAD_W, TK), lambda g: (bidx(g), 0, 0)),
            pl.BlockSpec((TQ, HEAD_W), lambda g: (meta_q, 1)),
            pl.BlockSpec((1, HEAD_W, TK), lambda g: (meta_tile, 0, 0)),
        ],
        out_specs=pl.BlockSpec((TQ, HEAD_W), lambda g: (g, 0)),
        out_shape=jax.ShapeDtypeStruct((n, HEAD_W), BF16),
        scratch_shapes=[
            pltpu.VMEM((IDX_HEADS * TQ, IDX_DIM), BF16),
            pltpu.VMEM((4, 2 * TQ, 128), BF16),
            pltpu.VMEM((nch, TK, TQ), I32),
            pltpu.VMEM((nch, TK, TQ), F32),
            pltpu.VMEM((TK, TK), BF16),
        ],
        compiler_params=_params(),
        name="dsa",
    )(proj, proj, wit, ki, proj, vat, proj, vat)


def _diff_kernel(lam_ref, g_ref, qb_ref, kb_ref, vbt_ref, km_ref, vmt_ref, o_ref, *, seq, lam_init):
    g = pl.program_id(0)
    is_real, i, n_ch = _step_geometry(g, seq)
    lp = lam_ref[...]
    lam = (jnp.exp(jnp.sum(lp[0:1] * lp[1:2], axis=1, keepdims=True))
           - jnp.exp(jnp.sum(lp[2:3] * lp[3:4], axis=1, keepdims=True)) + lam_init)
    q_pos = i * TQ + lax.broadcasted_iota(I32, (TK, 2 * TQ), 1) % TQ
    k_off = lax.broadcasted_iota(I32, (TK, 2 * TQ), 0)
    mbias = _meta_bias(is_real)
    for h in range(4):
        hs = slice(h * 128, (h + 1) * 128)
        qs = _stack_pair(qb_ref[:, hs])
        st = lax.dot_general(km_ref[:, hs], qs, _NT, preferred_element_type=F32) + mbias
        m = jnp.max(st, axis=0, keepdims=True)
        pm = jnp.exp(st - m)
        l = jnp.sum(pm, axis=0, keepdims=True)
        acc = jnp.dot(vmt_ref[0, hs, 0:128], pm.astype(BF16), preferred_element_type=F32)

        def body(c, carry):
            m, l, acc = carry
            kc = kb_ref[pl.ds(pl.multiple_of(c * TK, TK), TK), hs]
            st = lax.dot_general(kc, qs, _NT, preferred_element_type=F32)
            st = st + jnp.where(c * TK + k_off <= q_pos, 0.0, NEG)
            m_new = jnp.maximum(m, jnp.max(st, axis=0, keepdims=True))
            alpha = jnp.exp(m - m_new)
            pc = jnp.exp(st - m_new)
            l = alpha * l + jnp.sum(pc, axis=0, keepdims=True)
            acc = acc * alpha + jnp.dot(vbt_ref[c, hs, :], pc.astype(BF16), preferred_element_type=F32)
            return m_new, l, acc

        m, l, acc = lax.fori_loop(0, n_ch, body, (m, l, acc))
        on = acc * (1.0 / l)
        o = on[:, :TQ] - lam * on[:, TQ:]
        y = o * lax.rsqrt(jnp.mean(o * o, axis=0, keepdims=True) + EPS) * g_ref[...] * (1.0 - lam_init)
        o_ref[:, hs] = y.T.astype(BF16)


def _diff(proj, vbt, lam_params, subln_g, batch, seq, lam_init):
    n = proj.shape[0]
    nq_seq = seq // TQ
    nch = seq // TK
    bidx = lambda g: jnp.minimum(g // nq_seq, batch - 1)
    meta_q = batch * nq_seq
    meta_tile = batch * nch
    return pl.pallas_call(
        functools.partial(_diff_kernel, seq=seq, lam_init=lam_init),
        grid=(n // TQ,),
        in_specs=[
            _resident((4, 64), lambda g: (0, 0)),
            _resident((128, 1), lambda g: (0, 0)),
            pl.BlockSpec((TQ, HEAD_W), lambda g: (g, 3)),
            pl.BlockSpec((seq, HEAD_W), lambda g: (bidx(g), 4)),
            pl.BlockSpec((nch, HEAD_W, TK), lambda g: (bidx(g), 0, 0)),
            pl.BlockSpec((TQ, HEAD_W), lambda g: (meta_q, 4)),
            pl.BlockSpec((1, HEAD_W, TK), lambda g: (meta_tile, 0, 0)),
        ],
        out_specs=pl.BlockSpec((TQ, HEAD_W), lambda g: (g, 0)),
        out_shape=jax.ShapeDtypeStruct((n, HEAD_W), BF16),
        compiler_params=_params(),
        name="diff_attn",
    )(lam_params, subln_g.reshape(128, 1), proj, proj, vbt, proj, vbt)


def _out_proj_kernel(h_ref, oa_ref, ob_ref, wa_ref, wb_ref, o_ref):
    o_ref[...] = (h_ref[...]
                  + jnp.dot(oa_ref[...], wa_ref[...], preferred_element_type=F32)
                  + jnp.dot(ob_ref[...], wb_ref[...], preferred_element_type=F32))


def _out_proj(h, oa, ob, w_out):
    n = h.shape[0]
    return pl.pallas_call(
        _out_proj_kernel,
        grid=(n // TM,),
        in_specs=[
            pl.BlockSpec((TM, D), lambda i: (i, 0)),
            pl.BlockSpec((TM, HEAD_W), lambda i: (i, 0)),
            pl.BlockSpec((TM, HEAD_W), lambda i: (i, 0)),
            _resident((HEAD_W, D), lambda i: (0, 0)),
            _resident((HEAD_W, D), lambda i: (1, 0)),
        ],
        out_specs=pl.BlockSpec((TM, D), lambda i: (i, 0)),
        out_shape=jax.ShapeDtypeStruct((n, D), F32),
        compiler_params=_params(),
        name="attn_out_proj",
    )(h, oa, ob, w_out, w_out)


def _rglru_kernel(x_ref, g_ref, win_ref, cw_ref, cb_ref, gw_ref, gb_ref, lam_ref, wout_ref, o_ref,
                  xbuf, a_s, u_s, hstate, mstate, mhist, *, tiles_per_seq):
    g = pl.program_id(0)
    x = x_ref[...]
    xn = _rms(x, g_ref[...]).astype(BF16)
    yx = jnp.dot(xn, win_ref[...], preferred_element_type=F32)
    yb = yx[:, :D]
    y_br = 0.5 * yb * (1.0 + jnp.tanh(math.sqrt(2.0 / math.pi) * (yb + 0.044715 * (yb * yb * yb))))

    @pl.when(g == 0)
    def _():
        xbuf[0:8, :] = jnp.zeros((8, D), F32)
        hstate[...] = jnp.zeros((1, D), F32)

    @pl.when(jnp.logical_and(g >= 1, (g - 1) % tiles_per_seq == 0))
    def _():
        xbuf[0:8, :] = mhist[...]
        hstate[...] = mstate[...]

    xbuf[8:, :] = yx[:, D:]
    xc = cb_ref[...] + cw_ref[0:1, :] * xbuf[5:5 + TM, :]
    for j in range(1, CONV_W):
        xc = xc + cw_ref[j:j + 1, :] * xbuf[5 + j:5 + j + TM, :]

    gx, ga = [], []
    for nb in range(LRU_BLOCKS):
        xb = xc[:, nb * LRU_BLOCK_W:(nb + 1) * LRU_BLOCK_W].astype(BF16)
        gx.append(jnp.dot(xb, gw_ref[0, nb], preferred_element_type=F32))
        ga.append(jnp.dot(xb, gw_ref[1, nb], preferred_element_type=F32))
    gate_x = _sigmoid(jnp.concatenate(gx, axis=1) + gb_ref[0:1, :])
    gate_a = _sigmoid(jnp.concatenate(ga, axis=1) + gb_ref[1:2, :])
    lam = lam_ref[...]
    log_sig = jnp.minimum(lam, 0.0) - jnp.log(1.0 + jnp.exp(-jnp.abs(lam)))
    log_a = RG_C * gate_a * log_sig
    a = jnp.exp(log_a)
    a_s[...] = a
    u_s[...] = jnp.sqrt(-jnp.tanh(log_a) * (a * a + 1.0)) * (gate_x * xc)

    def step(t, hcur):
        hcur = a_s[pl.ds(t, 1), :] * hcur + u_s[pl.ds(t, 1), :]
        u_s[pl.ds(t, 1), :] = hcur
        return hcur

    hstate[...] = lax.fori_loop(0, TM, step, hstate[...], unroll=8)

    @pl.when(g == 0)
    def _():
        mstate[...] = u_s[N_META - 1:N_META, :]
        mhist[...] = xbuf[N_META:N_META + 8, :]

    xbuf[0:8, :] = xbuf[TM:TM + 8, :]
    o_ref[...] = x + jnp.dot((u_s[...] * y_br).astype(BF16), wout_ref[...], preferred_element_type=F32)


def _rglru(h, g, w_in, conv_w, conv_b, gate_w, gate_b, lru_lambda, w_out, tiles_per_seq):
    n = h.shape[0]
    nt = n // TM
    tile = lambda s: (jnp.where(s == 0, nt - 1, s - 1), 0)
    const2 = lambda s: (0, 0)
    return pl.pallas_call(
        functools.partial(_rglru_kernel, tiles_per_seq=tiles_per_seq),
        grid=(nt,),
        in_specs=[
            pl.BlockSpec((TM, D), tile),
            _resident((1, D), const2),
            _resident((D, 2 * D), const2),
            _resident((CONV_W, D), const2),
            _resident((1, D), const2),
            _resident((2, LRU_BLOCKS, LRU_BLOCK_W, LRU_BLOCK_W), lambda s: (0, 0, 0, 0)),
            _resident((2, D), const2),
            _resident((1, D), const2),
            _resident((D, D), const2),
        ],
        out_specs=pl.BlockSpec((TM, D), tile),
        out_shape=jax.ShapeDtypeStruct((n, D), F32),
        scratch_shapes=[
            pltpu.VMEM((TM + 8, D), F32),
            pltpu.VMEM((TM, D), F32),
            pltpu.VMEM((TM, D), F32),
            pltpu.VMEM((1, D), F32),
            pltpu.VMEM((1, D), F32),
            pltpu.VMEM((8, D), F32),
        ],
        compiler_params=_params(),
        name="rglru",
    )(h, g.reshape(1, D), w_in, conv_w, conv_b.reshape(1, D), gate_w, gate_b, lru_lambda.reshape(1, D), w_out)


def _split_attn_w_in(w):
    a = HEAD_W
    o_ki = 4 * a
    o_wi = o_ki + IDX_DIM
    o_qb = o_wi + IDX_HEADS
    qa, ka, va, qi = w[:, 0:a], w[:, a:2 * a], w[:, 2 * a:3 * a], w[:, 3 * a:4 * a]
    ki, wi = w[:, o_ki:o_wi], w[:, o_wi:o_qb]
    qb, kb, vb = w[:, o_qb:o_qb + a], w[:, o_qb + a:o_qb + 2 * a], w[:, o_qb + 2 * a:o_qb + 3 * a]
    w_main = jnp.concatenate([qa, ka, qi, qb, kb, va, vb], axis=1).astype(BF16)
    w_wt = jnp.concatenate([wi.T, jnp.zeros((16 - IDX_HEADS, D), w.dtype)], axis=0).astype(BF16)
    return w_main, ki.astype(BF16), w_wt


def kernel(x, meta_tokens, norm_g, ffn_w_gu, ffn_w_down, attn_w_in, idx_k_ln_g, idx_k_ln_b, diff_lambda, diff_subln_g, attn_w_out, rec_w_in, rec_conv_w, rec_conv_b, rec_gate_w, rec_gate_b, rec_lambda, rec_w_out, final_norm_g):
    batch, seq, _ = x.shape
    depth = norm_g.shape[0]
    n_real = batch * seq
    h = jnp.concatenate([x.reshape(n_real, D), meta_tokens.astype(x.dtype),
                         jnp.zeros((TM - N_META, D), x.dtype)], axis=0)
    w_gu = ffn_w_gu.astype(BF16)
    w_down = ffn_w_down.astype(BF16)
    for i in range(depth):
        j = i // 2
        h = _ffn(h, norm_g[i, 0], w_gu[i, 0], w_down[i, 0])
        if i % 2 == 0:
            lam_init = 0.8 - 0.6 * math.exp(-0.3 * i)
            w_main, w_ki, w_wt = _split_attn_w_in(attn_w_in[j])
            proj, ki, wit, vat, vbt = _attn_proj(h, norm_g[i, 1], w_main, w_ki, w_wt, idx_k_ln_g[j], idx_k_ln_b[j])
            oa = _dsa(proj, ki, wit, vat, batch, seq)
            ob = _diff(proj, vbt, diff_lambda[j], diff_subln_g[j], batch, seq, lam_init)
            h = _out_proj(h, oa, ob, attn_w_out[j].astype(BF16))
        else:
            h = _rglru(h, norm_g[i, 1], rec_w_in[j].astype(BF16), rec_conv_w[j], rec_conv_b[j],
                       rec_gate_w[j].astype(BF16), rec_gate_b[j], rec_lambda[j], rec_w_out[j].astype(BF16),
                       seq // TM)
        if i == depth - 1:
            h = _ffn(h, norm_g[i, 2], w_gu[i, 1], w_down[i, 1], final_g=final_norm_g, n_rows=n_real)
        else:
            h = _ffn(h, norm_g[i, 2], w_gu[i, 1], w_down[i, 1])
    return h.reshape(batch, seq, D)
```

```python
import functools
import math

import jax
import jax.numpy as jnp
from jax import lax
from jax.experimental import pallas as pl
from jax.experimental.pallas import tpu as pltpu

F32 = jnp.float32
BF16 = jnp.bfloat16
I32 = jnp.int32

D = 1024
N_META = 16
D_FF = 2816
EPS = 1e-6
HEAD_W = 512
IDX_DIM = 64
IDX_HEADS = 8
TOPK_MAX = 256
LRU_BLOCKS = 4
LRU_BLOCK_W = D // LRU_BLOCKS
CONV_W = 4
RG_C = 8.0

TM = 512
TQ = 128
TK = 512
FF_CHUNK = 1408
NEG = -1e30
BIG = 1e30
QK_SCALE = IDX_DIM ** -0.5 * math.log2(math.e)
INT_MIN = -2 ** 31
VMEM_LIMIT = 56 * 1024 * 1024

_NT = (((1,), (1,)), ((), ()))


def _params():
    return pltpu.CompilerParams(dimension_semantics=("arbitrary",), vmem_limit_bytes=VMEM_LIMIT)


def _resident(shape, index_map):
    return pl.BlockSpec(shape, index_map, pipeline_mode=pl.Buffered(1))


def _rms(x, g):
    return x * lax.rsqrt(jnp.mean(x * x, axis=-1, keepdims=True) + EPS) * g


def _sigmoid(x):
    return 1.0 / (1.0 + jnp.exp(-x))


def _ffn_kernel(x_ref, g_ref, wg_ref, wu_ref, wd_ref, *rest, final):
    o_ref = rest[-1]
    x = x_ref[...]
    xn = _rms(x, g_ref[...]).astype(BF16)
    acc = None
    for c in range(D_FF // FF_CHUNK):
        sl = slice(c * FF_CHUNK, (c + 1) * FF_CHUNK)
        g = jnp.dot(xn, wg_ref[:, sl], preferred_element_type=F32)
        u = jnp.dot(xn, wu_ref[:, sl], preferred_element_type=F32)
        a = (g * _sigmoid(g) * u).astype(BF16)
        part = jnp.dot(a, wd_ref[sl, :], preferred_element_type=F32)
        acc = part if acc is None else acc + part
    y = x + 0.5 * acc
    if final:
        y = _rms(y, rest[0][...])
    o_ref[...] = y


def _ffn(h, g, w_gu, w_down, final_g=None, n_rows=None):
    n_rows = h.shape[0] if n_rows is None else n_rows
    final = final_g is not None
    in_specs = [
        pl.BlockSpec((TM, D), lambda i: (i, 0)),
        _resident((1, D), lambda i: (0, 0)),
        _resident((D, D_FF), lambda i: (0, 0)),
        _resident((D, D_FF), lambda i: (0, 1)),
        _resident((D_FF, D), lambda i: (0, 0)),
    ]
    args = [h, g.reshape(1, D), w_gu, w_gu, w_down]
    if final:
        in_specs.append(_resident((1, D), lambda i: (0, 0)))
        args.append(final_g.reshape(1, D))
    return pl.pallas_call(
        functools.partial(_ffn_kernel, final=final),
        grid=(n_rows // TM,),
        in_specs=in_specs,
        out_specs=pl.BlockSpec((TM, D), lambda i: (i, 0)),
        out_shape=jax.ShapeDtypeStruct((n_rows, D), F32),
        compiler_params=_params(),
        name="ffn",
    )(*args)


N_PROJ = 5 * HEAD_W


def _attn_proj_kernel(x_ref, g_ref, w_ref, wki_ref, wwt_ref, lng_ref, lnb_ref,
                      proj_ref, ki_ref, wit_ref, vat_ref, vbt_ref):
    xn = _rms(x_ref[...], g_ref[...]).astype(BF16)
    y = jnp.dot(xn, w_ref[...], preferred_element_type=F32)
    scale = QK_SCALE
    proj_ref[:, 0:HEAD_W] = (y[:, 0:HEAD_W] * scale).astype(BF16)
    proj_ref[:, HEAD_W:3 * HEAD_W] = y[:, HEAD_W:3 * HEAD_W].astype(BF16)
    proj_ref[:, 3 * HEAD_W:4 * HEAD_W] = (y[:, 3 * HEAD_W:4 * HEAD_W] * scale).astype(BF16)
    proj_ref[:, 4 * HEAD_W:5 * HEAD_W] = y[:, 4 * HEAD_W:5 * HEAD_W].astype(BF16)
    vat_ref[0] = y[:, 5 * HEAD_W:6 * HEAD_W].T.astype(BF16)
    vbt_ref[0] = y[:, 6 * HEAD_W:7 * HEAD_W].T.astype(BF16)
    yk = jnp.dot(xn, wki_ref[...], preferred_element_type=F32)
    mu = jnp.mean(yk, axis=-1, keepdims=True)
    yc = yk - mu
    ln = yc * lax.rsqrt(jnp.mean(yc * yc, axis=-1, keepdims=True) + EPS)
    ki_ref[...] = (ln * lng_ref[...] + lnb_ref[...]).astype(BF16)
    wit = lax.dot_general(wwt_ref[...], xn, _NT, preferred_element_type=F32)
    wit_ref[...] = wit * (IDX_HEADS ** -0.5 * IDX_DIM ** -0.5)


def _attn_proj(h, g, w_main, w_ki, w_wt, ln_g, ln_b):
    n = h.shape[0]
    nt = n // TM
    const = lambda i: (0, 0)
    return pl.pallas_call(
        _attn_proj_kernel,
        grid=(nt,),
        in_specs=[
            pl.BlockSpec((TM, D), lambda i: (i, 0)),
            _resident((1, D), const),
            _resident((D, 7 * HEAD_W), const),
            _resident((D, IDX_DIM), const),
            _resident((16, D), const),
            _resident((1, IDX_DIM), const),
            _resident((1, IDX_DIM), const),
        ],
        out_specs=[
            pl.BlockSpec((TM, N_PROJ), lambda i: (i, 0)),
            pl.BlockSpec((TM, IDX_DIM), lambda i: (i, 0)),
            pl.BlockSpec((16, TM), lambda i: (0, i)),
            pl.BlockSpec((1, HEAD_W, TM), lambda i: (i, 0, 0)),
            pl.BlockSpec((1, HEAD_W, TM), lambda i: (i, 0, 0)),
        ],
        out_shape=[
            jax.ShapeDtypeStruct((n, N_PROJ), BF16),
            jax.ShapeDtypeStruct((n, IDX_DIM), BF16),
            jax.ShapeDtypeStruct((16, n), F32),
            jax.ShapeDtypeStruct((nt, HEAD_W, TM), BF16),
            jax.ShapeDtypeStruct((nt, HEAD_W, TM), BF16),
        ],
        compiler_params=_params(),
        name="attn_proj",
    )(h, g.reshape(1, D), w_main, w_ki, w_wt, ln_g.reshape(1, IDX_DIM), ln_b.reshape(1, IDX_DIM))


def _step_geometry(g, seq):
    nq_seq = seq // TQ
    n_real = pl.num_programs(0) - TM // TQ
    is_real = g < n_real
    i = g % nq_seq
    n_ch = jnp.where(is_real, (i * TQ + TQ + TK - 1) // TK, 0)
    return is_real, i, n_ch


def _stack_pair(q_pair):
    lane = lax.broadcasted_iota(I32, (TQ, 128), 1)
    qf = q_pair.astype(F32)
    top = jnp.where(lane < 64, qf, 0.0).astype(BF16)
    bot = jnp.where(lane >= 64, qf, 0.0).astype(BF16)
    return jnp.concatenate([top, bot], axis=0)


def _meta_bias(is_real):
    j = lax.broadcasted_iota(I32, (128, 2 * TQ), 0)
    t = lax.broadcasted_iota(I32, (128, 2 * TQ), 1) % TQ
    real = is_real.astype(I32)
    lim = jnp.minimum(real * N_META + (1 - real) * (t + 1), N_META)
    return jnp.where(j < lim, BIG, NEG)


def _bcast_halves(row, n_sub):
    return jnp.concatenate([jnp.broadcast_to(row[:, :TQ], (n_sub, TQ)),
                            jnp.broadcast_to(row[:, TQ:], (n_sub, TQ))], axis=0)


def _dsa_kernel(qa_ref, qi_ref, wit_ref, ki_ref, ka_ref, vat_ref, km_ref, vmt_ref, o_ref,
                qall_ref, qstk_ref, key_ref, bias_ref, tri_ref, m_ref, l_ref, acc_ref, *, seq, top_k):
    g = pl.program_id(0)
    is_real, i, n_ch = _step_geometry(g, seq)
    q_pos = i * TQ + lax.broadcasted_iota(I32, (TK, TQ), 1)
    k_off = lax.broadcasted_iota(I32, (TK, TQ), 0)

    qi = qi_ref[...]
    for h in range(IDX_HEADS):
        qall_ref[h * TQ:(h + 1) * TQ, :] = qi[:, h * IDX_DIM:(h + 1) * IDX_DIM]
    for p in range(4):
        qstk_ref[p] = _stack_pair(qa_ref[:, p * 128:(p + 1) * 128])
    r = lax.broadcasted_iota(I32, (TK, TK), 0)
    c_ = lax.broadcasted_iota(I32, (TK, TK), 1)
    tri_ref[...] = jnp.where(c_ <= r, 1.0, 0.0).astype(BF16)

    wt = wit_ref[...]

    def score_body(c, carry):
        kc = ki_ref[pl.ds(pl.multiple_of(c * TK, TK), TK), :]
        logits = lax.dot_general(kc, qall_ref[...], _NT, preferred_element_type=F32)
        sc = jnp.zeros((TK, TQ), F32)
        for h in range(IDX_HEADS):
            sc = sc + wt[h:h + 1, :] * jnp.maximum(logits[:, h * TQ:(h + 1) * TQ], 0.0)
        sc = jnp.where(sc == 0.0, 0.0, sc)
        bits = pltpu.bitcast(sc, I32)
        key = bits ^ (lax.shift_right_arithmetic(bits, 31) & 0x7FFFFFFF)
        key_ref[c] = jnp.where(c * TK + k_off <= q_pos, key, INT_MIN)
        return carry

    lax.fori_loop(0, n_ch, score_body, 0)

    def count(pred_fn):
        def body(c, acc):
            ones = jnp.where(pred_fn(key_ref[c]), 1, 0)
            return acc + jnp.sum(ones.reshape(TK // 8, 8, TQ), axis=0)
        acc = lax.fori_loop(0, n_ch, body, jnp.zeros((8, TQ), I32))
        return jnp.sum(acc, axis=0, keepdims=True)

    def bit_body(it, ans):
        try_u = ans | lax.shift_left(jnp.int32(1), 31 - it)
        try_s = try_u ^ INT_MIN
        cnt = count(lambda k: k >= try_s)
        return jnp.where(cnt >= top_k, try_u, ans)

    ans = lax.fori_loop(0, 32, bit_body, jnp.zeros((1, TQ), I32))
    thr = ans ^ INT_MIN
    need = (top_k - count(lambda k: k > thr)).astype(F32)

    def bias_body(c, run):
        k = key_ref[c]
        eq = k == thr
        pre = jnp.dot(tri_ref[...], jnp.where(eq, 1.0, 0.0).astype(BF16), preferred_element_type=F32)
        rank = run + pre
        b = jnp.where(k > thr, BIG, jnp.where(eq, jnp.where(rank <= need, BIG, NEG), NEG))
        bias_ref[c] = jnp.where(k == INT_MIN, NEG, b)
        return run + pre[TK - 1:TK, :]

    lax.fori_loop(0, n_ch, bias_body, jnp.zeros((1, TQ), F32))

    def pair_heads(ot):
        return jnp.concatenate([ot[:64, :TQ], ot[64:, TQ:]], axis=0)

    mbias = _meta_bias(is_real)
    for p in range(4):
        ps = slice(p * 128, (p + 1) * 128)
        st = jnp.minimum(lax.dot_general(km_ref[:, ps], qstk_ref[p], _NT, preferred_element_type=F32), mbias)
        m = jnp.max(st, axis=0, keepdims=True)
        pm = jnp.exp2(st - m)
        m_ref[p] = m
        l_ref[p] = jnp.sum(pm, axis=0, keepdims=True)
        acc_ref[p] = pair_heads(jnp.dot(vmt_ref[0, ps, 0:128], pm.astype(BF16), preferred_element_type=F32))

    def chunk_body(c, carry):
        b = bias_ref[c]
        b2 = jnp.concatenate([b, b], axis=1)
        row = pl.ds(pl.multiple_of(c * TK, TK), TK)
        sts = [lax.dot_general(ka_ref[row, p * 128:(p + 1) * 128], qstk_ref[p], _NT, preferred_element_type=F32)
               for p in range(4)]
        for p in range(4):
            ps = slice(p * 128, (p + 1) * 128)
            st = jnp.minimum(sts[p], b2)
            m_old = m_ref[p]
            m_new = jnp.maximum(m_old, jnp.max(st, axis=0, keepdims=True))
            alpha = jnp.exp2(m_old - m_new)
            pc = jnp.exp2(st - m_new)
            m_ref[p] = m_new
            l_ref[p] = alpha * l_ref[p] + jnp.sum(pc, axis=0, keepdims=True)
            ot = jnp.dot(vat_ref[c, ps, :], pc.astype(BF16), preferred_element_type=F32)
            acc_ref[p] = acc_ref[p] * _bcast_halves(alpha, 64) + pair_heads(ot)
        return carry

    lax.fori_loop(0, n_ch, chunk_body, 0)
    for p in range(4):
        ps = slice(p * 128, (p + 1) * 128)
        o_ref[:, ps] = (acc_ref[p] * _bcast_halves(1.0 / l_ref[p], 64)).T.astype(BF16)


def _dsa(proj, ki, wit, vat, batch, seq):
    n = proj.shape[0]
    nq_seq = seq // TQ
    n_real_q = batch * nq_seq
    nch = seq // TK
    top_k = min(TOPK_MAX, seq // 4)
    bidx = lambda g: jnp.minimum(g // nq_seq, batch - 1)
    meta_q = n_real_q
    meta_tile = batch * nch
    return pl.pallas_call(
        functools.partial(_dsa_kernel, seq=seq, top_k=top_k),
        grid=(n // TQ,),
        in_specs=[
            pl.BlockSpec((TQ, HEAD_W), lambda g: (g, 0)),
            pl.BlockSpec((TQ, HEAD_W), lambda g: (g, 2)),
            pl.BlockSpec((16, TQ), lambda g: (0, g)),
            pl.BlockSpec((seq, IDX_DIM), lambda g: (bidx(g), 0)),
            pl.BlockSpec((seq, HEAD_W), lambda g: (bidx(g), 1)),
            pl.BlockSpec((nch, HEAD_W, TK), lambda g: (bidx(g), 0, 0)),
            pl.BlockSpec((TQ, HEAD_W), lambda g: (meta_q, 1)),
            pl.BlockSpec((1, HEAD_W, TK), lambda g: (meta_tile, 0, 0)),
        ],
        out_specs=pl.BlockSpec((TQ, HEAD_W), lambda g: (g, 0)),
        out_shape=jax.ShapeDtypeStruct((n, HEAD_W), BF16),
        scratch_shapes=[
            pltpu.VMEM((IDX_HEADS * TQ, IDX_DIM), BF16),
            pltpu.VMEM((4, 2 * TQ, 128), BF16),
            pltpu.VMEM((nch, TK, TQ), I32),
            pltpu.VMEM((nch, TK, TQ), F32),
            pltpu.VMEM((TK, TK), BF16),
            pltpu.VMEM((4, 1, 2 * TQ), F32),
            pltpu.VMEM((4, 1, 2 * TQ), F32),
            pltpu.VMEM((4, 128, TQ), F32),
        ],
        compiler_params=_params(),
        name="dsa",
    )(proj, proj, wit, ki, proj, vat, proj, vat)


def _diff_kernel(lam_ref, g_ref, qb_ref, kb_ref, vbt_ref, km_ref, vmt_ref, o_ref,
                 qstk_ref, m_ref, l_ref, acc_ref, *, seq, lam_init):
    g = pl.program_id(0)
    is_real, i, n_ch = _step_geometry(g, seq)
    lp = lam_ref[...]
    lam = (jnp.exp(jnp.sum(lp[0:1] * lp[1:2], axis=1, keepdims=True))
           - jnp.exp(jnp.sum(lp[2:3] * lp[3:4], axis=1, keepdims=True)) + lam_init)
    mbias = _meta_bias(is_real)
    for h in range(4):
        hs = slice(h * 128, (h + 1) * 128)
        qstk_ref[h] = _stack_pair(qb_ref[:, hs])
        st = jnp.minimum(lax.dot_general(km_ref[:, hs], qstk_ref[h], _NT, preferred_element_type=F32), mbias)
        m = jnp.max(st, axis=0, keepdims=True)
        pm = jnp.exp2(st - m)
        m_ref[h] = m
        l_ref[h] = jnp.sum(pm, axis=0, keepdims=True)
        acc_ref[h] = jnp.dot(vmt_ref[0, hs, 0:128], pm.astype(BF16), preferred_element_type=F32)

    def chunk(c, cap):
        row = pl.ds(pl.multiple_of(c * TK, TK), TK)
        sts = [lax.dot_general(kb_ref[row, h * 128:(h + 1) * 128], qstk_ref[h], _NT, preferred_element_type=F32)
               for h in range(4)]
        for h in range(4):
            hs = slice(h * 128, (h + 1) * 128)
            st = sts[h]
            if cap is not None:
                st = jnp.minimum(st, cap)
            m_old = m_ref[h]
            m_new = jnp.maximum(m_old, jnp.max(st, axis=0, keepdims=True))
            alpha = jnp.exp2(m_old - m_new)
            pc = jnp.exp2(st - m_new)
            m_ref[h] = m_new
            l_ref[h] = alpha * l_ref[h] + jnp.sum(pc, axis=0, keepdims=True)
            acc_ref[h] = acc_ref[h] * alpha + jnp.dot(vbt_ref[c, hs, :], pc.astype(BF16),
                                                      preferred_element_type=F32)

    n_full = n_ch - is_real.astype(I32)

    def full_body(c, carry):
        chunk(c, None)
        return carry

    lax.fori_loop(0, n_full, full_body, 0)
    q_pos = i * TQ + lax.broadcasted_iota(I32, (TK, 2 * TQ), 1) % TQ
    k_pos = n_full * TK + lax.broadcasted_iota(I32, (TK, 2 * TQ), 0)
    visible = jnp.where(k_pos <= q_pos, is_real.astype(I32), 0)
    chunk(n_full, jnp.where(visible > 0, BIG, NEG))

    for h in range(4):
        hs = slice(h * 128, (h + 1) * 128)
        on = acc_ref[h] * (1.0 / l_ref[h])
        o = on[:, :TQ] - lam * on[:, TQ:]
        y = o * lax.rsqrt(jnp.mean(o * o, axis=0, keepdims=True) + EPS) * g_ref[...] * (1.0 - lam_init)
        o_ref[:, hs] = y.T.astype(BF16)


def _diff(proj, vbt, lam_params, subln_g, batch, seq, lam_init):
    n = proj.shape[0]
    nq_seq = seq // TQ
    nch = seq // TK
    bidx = lambda g: jnp.minimum(g // nq_seq, batch - 1)
    meta_q = batch * nq_seq
    meta_tile = batch * nch
    return pl.pallas_call(
        functools.partial(_diff_kernel, seq=seq, lam_init=lam_init),
        grid=(n // TQ,),
        in_specs=[
            _resident((4, 64), lambda g: (0, 0)),
            _resident((128, 1), lambda g: (0, 0)),
            pl.BlockSpec((TQ, HEAD_W), lambda g: (g, 3)),
            pl.BlockSpec((seq, HEAD_W), lambda g: (bidx(g), 4)),
            pl.BlockSpec((nch, HEAD_W, TK), lambda g: (bidx(g), 0, 0)),
            pl.BlockSpec((TQ, HEAD_W), lambda g: (meta_q, 4)),
            pl.BlockSpec((1, HEAD_W, TK), lambda g: (meta_tile, 0, 0)),
        ],
        out_specs=pl.BlockSpec((TQ, HEAD_W), lambda g: (g, 0)),
        out_shape=jax.ShapeDtypeStruct((n, HEAD_W), BF16),
        scratch_shapes=[
            pltpu.VMEM((4, 2 * TQ, 128), BF16),
            pltpu.VMEM((4, 1, 2 * TQ), F32),
            pltpu.VMEM((4, 1, 2 * TQ), F32),
            pltpu.VMEM((4, 128, 2 * TQ), F32),
        ],
        compiler_params=_params(),
        name="diff_attn",
    )(lam_params, subln_g.reshape(128, 1), proj, proj, vbt, proj, vbt)


def _out_proj_kernel(h_ref, oa_ref, ob_ref, wa_ref, wb_ref, o_ref):
    o_ref[...] = (h_ref[...]
                  + jnp.dot(oa_ref[...], wa_ref[...], preferred_element_type=F32)
                  + jnp.dot(ob_ref[...], wb_ref[...], preferred_element_type=F32))


def _out_proj(h, oa, ob, w_out):
    n = h.shape[0]
    return pl.pallas_call(
        _out_proj_kernel,
        grid=(n // TM,),
        in_specs=[
            pl.BlockSpec((TM, D), lambda i: (i, 0)),
            pl.BlockSpec((TM, HEAD_W), lambda i: (i, 0)),
            pl.BlockSpec((TM, HEAD_W), lambda i: (i, 0)),
            _resident((HEAD_W, D), lambda i: (0, 0)),
            _resident((HEAD_W, D), lambda i: (1, 0)),
        ],
        out_specs=pl.BlockSpec((TM, D), lambda i: (i, 0)),
        out_shape=jax.ShapeDtypeStruct((n, D), F32),
        compiler_params=_params(),
        name="attn_out_proj",
    )(h, oa, ob, w_out, w_out)


def _rglru_kernel(x_ref, g_ref, win_ref, cw_ref, cb_ref, gw_ref, gb_ref, lam_ref, wout_ref, o_ref,
                  xbuf, a_s, u_s, hstate, mstate, mhist, *, tiles_per_seq):
    g = pl.program_id(0)
    x = x_ref[...]
    xn = _rms(x, g_ref[...]).astype(BF16)
    yx = jnp.dot(xn, win_ref[...], preferred_element_type=F32)
    yb = yx[:, :D]
    y_br = 0.5 * yb * (1.0 + jnp.tanh(math.sqrt(2.0 / math.pi) * (yb + 0.044715 * (yb * yb * yb))))

    @pl.when(g == 0)
    def _():
        xbuf[0:8, :] = jnp.zeros((8, D), F32)
        hstate[...] = jnp.zeros((1, D), F32)

    @pl.when(jnp.logical_and(g >= 1, (g - 1) % tiles_per_seq == 0))
    def _():
        xbuf[0:8, :] = mhist[...]
        hstate[...] = mstate[...]

    xbuf[8:, :] = yx[:, D:]
    xc = cb_ref[...] + cw_ref[0:1, :] * xbuf[5:5 + TM, :]
    for j in range(1, CONV_W):
        xc = xc + cw_ref[j:j + 1, :] * xbuf[5 + j:5 + j + TM, :]

    gx, ga = [], []
    for nb in range(LRU_BLOCKS):
        xb = xc[:, nb * LRU_BLOCK_W:(nb + 1) * LRU_BLOCK_W].astype(BF16)
        gx.append(jnp.dot(xb, gw_ref[0, nb], preferred_element_type=F32))
        ga.append(jnp.dot(xb, gw_ref[1, nb], preferred_element_type=F32))
    gate_x = _sigmoid(jnp.concatenate(gx, axis=1) + gb_ref[0:1, :])
    gate_a = _sigmoid(jnp.concatenate(ga, axis=1) + gb_ref[1:2, :])
    lam = lam_ref[...]
    log_sig = jnp.minimum(lam, 0.0) - jnp.log(1.0 + jnp.exp(-jnp.abs(lam)))
    log_a = RG_C * gate_a * log_sig
    a = jnp.exp(log_a)
    a_s[...] = a
    u_s[...] = jnp.sqrt(-jnp.tanh(log_a) * (a * a + 1.0)) * (gate_x * xc)

    def step(t, hcur):
        hcur = a_s[pl.ds(t, 1), :] * hcur + u_s[pl.ds(t, 1), :]
        u_s[pl.ds(t, 1), :] = hcur
        return hcur

    hstate[...] = lax.fori_loop(0, TM, step, hstate[...], unroll=8)

    @pl.when(g == 0)
    def _():
        mstate[...] = u_s[N_META - 1:N_META, :]
        mhist[...] = xbuf[N_META:N_META + 8, :]

    xbuf[0:8, :] = xbuf[TM:TM + 8, :]
    o_ref[...] = x + jnp.dot((u_s[...] * y_br).astype(BF16), wout_ref[...], preferred_element_type=F32)


def _rglru(h, g, w_in, conv_w, conv_b, gate_w, gate_b, lru_lambda, w_out, tiles_per_seq):
    n = h.shape[0]
    nt = n // TM
    tile = lambda s: (jnp.where(s == 0, nt - 1, s - 1), 0)
    const2 = lambda s: (0, 0)
    return pl.pallas_call(
        functools.partial(_rglru_kernel, tiles_per_seq=tiles_per_seq),
        grid=(nt,),
        in_specs=[
            pl.BlockSpec((TM, D), tile),
            _resident((1, D), const2),
            _resident((D, 2 * D), const2),
            _resident((CONV_W, D), const2),
            _resident((1, D), const2),
            _resident((2, LRU_BLOCKS, LRU_BLOCK_W, LRU_BLOCK_W), lambda s: (0, 0, 0, 0)),
            _resident((2, D), const2),
            _resident((1, D), const2),
            _resident((D, D), const2),
        ],
        out_specs=pl.BlockSpec((TM, D), tile),
        out_shape=jax.ShapeDtypeStruct((n, D), F32),
        scratch_shapes=[
            pltpu.VMEM((TM + 8, D), F32),
            pltpu.VMEM((TM, D), F32),
            pltpu.VMEM((TM, D), F32),
            pltpu.VMEM((1, D), F32),
            pltpu.VMEM((1, D), F32),
            pltpu.VMEM((8, D), F32),
        ],
        compiler_params=_params(),
        name="rglru",
    )(h, g.reshape(1, D), w_in, conv_w, conv_b.reshape(1, D), gate_w, gate_b, lru_lambda.reshape(1, D), w_out)


def _split_attn_w_in(w):
    a = HEAD_W
    o_ki = 4 * a
    o_wi = o_ki + IDX_DIM
    o_qb = o_wi + IDX_HEADS
    qa, ka, va, qi = w[:, 0:a], w[:, a:2 * a], w[:, 2 * a:3 * a], w[:, 3 * a:4 * a]
    ki, wi = w[:, o_ki:o_wi], w[:, o_wi:o_qb]
    qb, kb, vb = w[:, o_qb:o_qb + a], w[:, o_qb + a:o_qb + 2 * a], w[:, o_qb + 2 * a:o_qb + 3 * a]
    w_main = jnp.concatenate([qa, ka, qi, qb, kb, va, vb], axis=1).astype(BF16)
    w_wt = jnp.concatenate([wi.T, jnp.zeros((16 - IDX_HEADS, D), w.dtype)], axis=0).astype(BF16)
    return w_main, ki.astype(BF16), w_wt


def kernel(x, meta_tokens, norm_g, ffn_w_gu, ffn_w_down, attn_w_in, idx_k_ln_g, idx_k_ln_b, diff_lambda, diff_subln_g, attn_w_out, rec_w_in, rec_conv_w, rec_conv_b, rec_gate_w, rec_gate_b, rec_lambda, rec_w_out, final_norm_g):
    batch, seq, _ = x.shape
    depth = norm_g.shape[0]
    n_real = batch * seq
    h = jnp.concatenate([x.reshape(n_real, D), meta_tokens.astype(x.dtype),
                         jnp.zeros((TM - N_META, D), x.dtype)], axis=0)
    w_gu = ffn_w_gu.astype(BF16)
    w_down = ffn_w_down.astype(BF16)
    for i in range(depth):
        j = i // 2
        h = _ffn(h, norm_g[i, 0], w_gu[i, 0], w_down[i, 0])
        if i % 2 == 0:
            lam_init = 0.8 - 0.6 * math.exp(-0.3 * i)
            w_main, w_ki, w_wt = _split_attn_w_in(attn_w_in[j])
            proj, ki, wit, vat, vbt = _attn_proj(h, norm_g[i, 1], w_main, w_ki, w_wt, idx_k_ln_g[j], idx_k_ln_b[j])
            oa = _dsa(proj, ki, wit, vat, batch, seq)
            ob = _diff(proj, vbt, diff_lambda[j], diff_subln_g[j], batch, seq, lam_init)
            h = _out_proj(h, oa, ob, attn_w_out[j].astype(BF16))
        else:
            h = _rglru(h, norm_g[i, 1], rec_w_in[j].astype(BF16), rec_conv_w[j], rec_conv_b[j],
                       rec_gate_w[j].astype(BF16), rec_gate_b[j], rec_lambda[j], rec_w_out[j].astype(BF16),
                       seq // TM)
        if i == depth - 1:
            h = _ffn(h, norm_g[i, 2], w_gu[i, 1], w_down[i, 1], final_g=final_norm_g, n_rows=n_real)
        else:
            h = _ffn(h, norm_g[i, 2], w_gu[i, 1], w_down[i, 1])
    return h.reshape(batch, seq, D)
```

```python
import functools
import math

import jax
import jax.numpy as jnp
from jax import lax
from jax.experimental import pallas as pl
from jax.experimental.pallas import tpu as pltpu

F32 = jnp.float32
BF16 = jnp.bfloat16
I32 = jnp.int32

D = 1024
N_META = 16
D_FF = 2816
EPS = 1e-6
HEAD_W = 512
IDX_DIM = 64
IDX_HEADS = 8
TOPK_MAX = 256
LRU_BLOCKS = 4
LRU_BLOCK_W = D // LRU_BLOCKS
CONV_W = 4
RG_C = 8.0

TM = 512
TQ = 128
TK = 512
FF_CHUNK = 1408
NEG = -1e30
BIG = 1e30
QK_SCALE = IDX_DIM ** -0.5 * math.log2(math.e)
INT_MIN = -2 ** 31
VMEM_LIMIT = 56 * 1024 * 1024

_NT = (((1,), (1,)), ((), ()))


def _params():
    return pltpu.CompilerParams(dimension_semantics=("arbitrary",), vmem_limit_bytes=VMEM_LIMIT)


def _resident(shape, index_map):
    return pl.BlockSpec(shape, index_map, pipeline_mode=pl.Buffered(1))


def _rms(x, g):
    return x * lax.rsqrt(jnp.mean(x * x, axis=-1, keepdims=True) + EPS) * g


def _sigmoid(x):
    return 1.0 / (1.0 + jnp.exp(-x))


def _ffn_kernel(x_ref, g_ref, wg_ref, wu_ref, wd_ref, *rest, final):
    o_ref = rest[-1]
    x = x_ref[...]
    xn = _rms(x, g_ref[...]).astype(BF16)
    acc = None
    for c in range(D_FF // FF_CHUNK):
        sl = slice(c * FF_CHUNK, (c + 1) * FF_CHUNK)
        g = jnp.dot(xn, wg_ref[:, sl], preferred_element_type=F32)
        u = jnp.dot(xn, wu_ref[:, sl], preferred_element_type=F32)
        a = (g * _sigmoid(g) * u).astype(BF16)
        part = jnp.dot(a, wd_ref[sl, :], preferred_element_type=F32)
        acc = part if acc is None else acc + part
    y = x + 0.5 * acc
    if final:
        y = _rms(y, rest[0][...])
    o_ref[...] = y


def _ffn(h, g, w_gu, w_down, final_g=None, n_rows=None):
    n_rows = h.shape[0] if n_rows is None else n_rows
    final = final_g is not None
    in_specs = [
        pl.BlockSpec((TM, D), lambda i: (i, 0)),
        _resident((1, D), lambda i: (0, 0)),
        _resident((D, D_FF), lambda i: (0, 0)),
        _resident((D, D_FF), lambda i: (0, 1)),
        _resident((D_FF, D), lambda i: (0, 0)),
    ]
    args = [h, g.reshape(1, D), w_gu, w_gu, w_down]
    if final:
        in_specs.append(_resident((1, D), lambda i: (0, 0)))
        args.append(final_g.reshape(1, D))
    return pl.pallas_call(
        functools.partial(_ffn_kernel, final=final),
        grid=(n_rows // TM,),
        in_specs=in_specs,
        out_specs=pl.BlockSpec((TM, D), lambda i: (i, 0)),
        out_shape=jax.ShapeDtypeStruct((n_rows, D), F32),
        compiler_params=_params(),
        name="ffn",
    )(*args)


N_PROJ = 5 * HEAD_W


def _attn_proj_kernel(x_ref, g_ref, w_ref, wki_ref, wwt_ref, lng_ref, lnb_ref,
                      proj_ref, ki_ref, wit_ref, vat_ref, vbt_ref):
    xn = _rms(x_ref[...], g_ref[...]).astype(BF16)
    y = jnp.dot(xn, w_ref[...], preferred_element_type=F32)
    scale = QK_SCALE
    proj_ref[:, 0:HEAD_W] = (y[:, 0:HEAD_W] * scale).astype(BF16)
    proj_ref[:, HEAD_W:3 * HEAD_W] = y[:, HEAD_W:3 * HEAD_W].astype(BF16)
    proj_ref[:, 3 * HEAD_W:4 * HEAD_W] = (y[:, 3 * HEAD_W:4 * HEAD_W] * scale).astype(BF16)
    proj_ref[:, 4 * HEAD_W:5 * HEAD_W] = y[:, 4 * HEAD_W:5 * HEAD_W].astype(BF16)
    vat_ref[0] = y[:, 5 * HEAD_W:6 * HEAD_W].T.astype(BF16)
    vbt_ref[0] = y[:, 6 * HEAD_W:7 * HEAD_W].T.astype(BF16)
    yk = jnp.dot(xn, wki_ref[...], preferred_element_type=F32)
    mu = jnp.mean(yk, axis=-1, keepdims=True)
    yc = yk - mu
    ln = yc * lax.rsqrt(jnp.mean(yc * yc, axis=-1, keepdims=True) + EPS)
    ki_ref[...] = (ln * lng_ref[...] + lnb_ref[...]).astype(BF16)
    wit = lax.dot_general(wwt_ref[...], xn, _NT, preferred_element_type=F32)
    wit_ref[...] = wit * (IDX_HEADS ** -0.5 * IDX_DIM ** -0.5)


def _attn_proj(h, g, w_main, w_ki, w_wt, ln_g, ln_b):
    n = h.shape[0]
    nt = n // TM
    const = lambda i: (0, 0)
    return pl.pallas_call(
        _attn_proj_kernel,
        grid=(nt,),
        in_specs=[
            pl.BlockSpec((TM, D), lambda i: (i, 0)),
            _resident((1, D), const),
            _resident((D, 7 * HEAD_W), const),
            _resident((D, IDX_DIM), const),
            _resident((16, D), const),
            _resident((1, IDX_DIM), const),
            _resident((1, IDX_DIM), const),
        ],
        out_specs=[
            pl.BlockSpec((TM, N_PROJ), lambda i: (i, 0)),
            pl.BlockSpec((TM, IDX_DIM), lambda i: (i, 0)),
            pl.BlockSpec((16, TM), lambda i: (0, i)),
            pl.BlockSpec((1, HEAD_W, TM), lambda i: (i, 0, 0)),
            pl.BlockSpec((1, HEAD_W, TM), lambda i: (i, 0, 0)),
        ],
        out_shape=[
            jax.ShapeDtypeStruct((n, N_PROJ), BF16),
            jax.ShapeDtypeStruct((n, IDX_DIM), BF16),
            jax.ShapeDtypeStruct((16, n), F32),
            jax.ShapeDtypeStruct((nt, HEAD_W, TM), BF16),
            jax.ShapeDtypeStruct((nt, HEAD_W, TM), BF16),
        ],
        compiler_params=_params(),
        name="attn_proj",
    )(h, g.reshape(1, D), w_main, w_ki, w_wt, ln_g.reshape(1, IDX_DIM), ln_b.reshape(1, IDX_DIM))


def _step_geometry(g, seq):
    nq_seq = seq // TQ
    n_real = pl.num_programs(0) - TM // TQ
    is_real = g < n_real
    i = g % nq_seq
    n_ch = jnp.where(is_real, (i * TQ + TQ + TK - 1) // TK, 0)
    return is_real, i, n_ch


def _stack_pair(q_pair):
    lane = lax.broadcasted_iota(I32, (TQ, 128), 1)
    qf = q_pair.astype(F32)
    top = jnp.where(lane < 64, qf, 0.0).astype(BF16)
    bot = jnp.where(lane >= 64, qf, 0.0).astype(BF16)
    return jnp.concatenate([top, bot], axis=0)


def _meta_bias(is_real):
    j = lax.broadcasted_iota(I32, (128, 2 * TQ), 0)
    t = lax.broadcasted_iota(I32, (128, 2 * TQ), 1) % TQ
    real = is_real.astype(I32)
    lim = jnp.minimum(real * N_META + (1 - real) * (t + 1), N_META)
    return jnp.where(j < lim, BIG, NEG)


def _bcast_halves(row, n_sub):
    return jnp.concatenate([jnp.broadcast_to(row[:, :TQ], (n_sub, TQ)),
                            jnp.broadcast_to(row[:, TQ:], (n_sub, TQ))], axis=0)


def _attend(segments, v_tiles, m_ref, l_ref, acc_ref, fold, first):
    fold_pv, fold_row = fold
    for h in range(4):
        sts = segments[h]
        m_loc = jnp.max(sts[0], axis=0, keepdims=True)
        for st in sts[1:]:
            m_loc = jnp.maximum(m_loc, jnp.max(st, axis=0, keepdims=True))
        m_new = m_loc if first else jnp.maximum(m_ref[h], m_loc)
        l_new, pv = None, None
        for st, vt in zip(sts, v_tiles[h]):
            pc = jnp.exp2(st - m_new)
            s = jnp.sum(pc, axis=0, keepdims=True)
            d = jnp.dot(vt, pc.astype(BF16), preferred_element_type=F32)
            l_new = s if l_new is None else l_new + s
            pv = d if pv is None else pv + d
        if first:
            l_ref[h] = l_new
            acc_ref[h] = fold_pv(pv)
        else:
            alpha = jnp.exp2(m_ref[h] - m_new)
            l_ref[h] = alpha * l_ref[h] + l_new
            acc_ref[h] = acc_ref[h] * fold_row(alpha) + fold_pv(pv)
        m_ref[h] = m_new


GROUP_KEYS = 256
GROUPS_PER_CHUNK = TK // GROUP_KEYS


def _bit_transpose32(words):
    a = list(words)
    j, mask = 16, 0x0000FFFF
    while j:
        k = 0
        while k < 32:
            t = (a[k] ^ lax.shift_right_logical(a[k + j], jnp.int32(j))) & mask
            a[k] = a[k] ^ t
            a[k + j] = a[k + j] ^ lax.shift_left(t, jnp.int32(j))
            k = (k + j + 1) & ~j
        j >>= 1
        mask ^= mask << j
    return a


def _dsa_kernel(qa_ref, qi_ref, wit_ref, ki_ref, ka_ref, vat_ref, km_ref, vmt_ref, o_ref,
                qall_ref, qstk_ref, key_ref, plane_ref, bias_ref, tri_ref, m_ref, l_ref, acc_ref, *, seq, top_k):
    g = pl.program_id(0)

    @pl.when(g == 0)
    def _():
        plane_ref[...] = jnp.zeros(plane_ref.shape, I32)
        r = lax.broadcasted_iota(I32, (TK, TK), 0)
        c_ = lax.broadcasted_iota(I32, (TK, TK), 1)
        tri_ref[...] = jnp.where(c_ <= r, 1.0, 0.0).astype(BF16)

    is_real, i, n_ch = _step_geometry(g, seq)
    q_pos = i * TQ + lax.broadcasted_iota(I32, (TK, TQ), 1)
    k_off = lax.broadcasted_iota(I32, (TK, TQ), 0)

    qi = qi_ref[...]
    for h in range(IDX_HEADS):
        qall_ref[h * TQ:(h + 1) * TQ, :] = qi[:, h * IDX_DIM:(h + 1) * IDX_DIM]
    for p in range(4):
        qstk_ref[p] = _stack_pair(qa_ref[:, p * 128:(p + 1) * 128])

    wt = wit_ref[...]

    def score_body(c, carry):
        kc = ki_ref[pl.ds(pl.multiple_of(c * TK, TK), TK), :]
        logits = lax.dot_general(kc, qall_ref[...], _NT, preferred_element_type=F32)
        sc = jnp.zeros((TK, TQ), F32)
        for h in range(IDX_HEADS):
            sc = sc + wt[h:h + 1, :] * jnp.maximum(logits[:, h * TQ:(h + 1) * TQ], 0.0)
        sc = jnp.where(sc == 0.0, 0.0, sc)
        bits = pltpu.bitcast(sc, I32)
        key = bits ^ (lax.shift_right_arithmetic(bits, 31) & 0x7FFFFFFF)
        key = jnp.where(c * TK + k_off <= q_pos, key, INT_MIN)
        key_ref[c] = key
        u = key ^ INT_MIN
        for gi in range(GROUPS_PER_CHUNK):
            base = gi * GROUP_KEYS
            planes = _bit_transpose32([u[base + w * 8:base + (w + 1) * 8, :] for w in range(32)])
            for b in range(32):
                plane_ref[c * GROUPS_PER_CHUNK + gi, b] = planes[b]
        return carry

    lax.fori_loop(0, n_ch, score_body, 0)

    n_grp = n_ch * GROUPS_PER_CHUNK
    n_groups = plane_ref.shape[0]
    alive0 = tuple(jnp.full((8, TQ), -1, I32) * (gq < n_grp).astype(I32) for gq in range(n_groups))

    def bit_body(b, carry):
        alive, k_rem, thr_u = carry
        ones = [a & plane_ref[gq, b] for gq, a in enumerate(alive)]
        cnt8 = lax.population_count(ones[0])
        for o in ones[1:]:
            cnt8 = cnt8 + lax.population_count(o)
        cnt = jnp.sum(cnt8, axis=0, keepdims=True)
        take = cnt >= k_rem
        alive = tuple(jnp.where(take, o, a ^ o) for a, o in zip(alive, ones))
        k_rem = jnp.where(take, k_rem, k_rem - cnt)
        thr_u = thr_u | jnp.where(take, lax.shift_left(jnp.int32(1), 31 - b), 0)
        return alive, k_rem, thr_u

    alive, k_rem, thr_u = lax.fori_loop(
        0, 32, bit_body, (alive0, jnp.full((1, TQ), top_k, I32), jnp.zeros((1, TQ), I32)))
    thr = thr_u ^ INT_MIN
    eq8 = lax.population_count(alive[0])
    for a in alive[1:]:
        eq8 = eq8 + lax.population_count(a)
    cnt_eq = jnp.sum(eq8, axis=0, keepdims=True)
    excess = jnp.where(thr_u != 0, cnt_eq - k_rem, 0)
    has_excess = jnp.max(excess) > 0

    @pl.when(jnp.logical_not(has_excess))
    def _():
        thr_eff = jnp.maximum(thr, INT_MIN + 1)

        def cap_body(c, carry):
            bias_ref[c] = jnp.where(key_ref[c] >= thr_eff, BIG, NEG)
            return carry

        lax.fori_loop(0, n_ch, cap_body, 0)

    @pl.when(has_excess)
    def _():
        need = k_rem.astype(F32)

        def cap_body(c, run):
            k = key_ref[c]
            eq = k == thr
            pre = jnp.dot(tri_ref[...], jnp.where(eq, 1.0, 0.0).astype(BF16), preferred_element_type=F32)
            rank = run + pre
            b = jnp.where(k > thr, BIG, jnp.where(eq, jnp.where(rank <= need, BIG, NEG), NEG))
            bias_ref[c] = jnp.where(k == INT_MIN, NEG, b)
            return run + pre[TK - 1:TK, :]

        lax.fori_loop(0, n_ch, cap_body, jnp.zeros((1, TQ), F32))

    fold = (lambda ot: jnp.concatenate([ot[:64, :TQ], ot[64:, TQ:]], axis=0), lambda row: _bcast_halves(row, 64))
    pairs = [slice(p * 128, (p + 1) * 128) for p in range(4)]

    def scores(k_tile):
        return [lax.dot_general(k_tile(ps), qstk_ref[p], _NT, preferred_element_type=F32)
                for p, ps in enumerate(pairs)]

    c_last = jnp.maximum(n_ch - 1, 0)
    row_last = pl.ds(pl.multiple_of(c_last * TK, TK), TK)
    st_meta = scores(lambda ps: km_ref[:, ps])
    st_last = scores(lambda ps: ka_ref[row_last, ps])
    mcap = _meta_bias(is_real)
    cap = jnp.where(is_real, bias_ref[c_last], NEG)
    cap2 = jnp.concatenate([cap, cap], axis=1)
    _attend([[jnp.minimum(st_meta[p], mcap), jnp.minimum(st_last[p], cap2)] for p in range(4)],
            [[vmt_ref[0, ps, 0:128], vat_ref[c_last, ps, :]] for ps in pairs],
            m_ref, l_ref, acc_ref, fold, first=True)

    def chunk_body(c, carry):
        cap = bias_ref[c]
        cap2 = jnp.concatenate([cap, cap], axis=1)
        row = pl.ds(pl.multiple_of(c * TK, TK), TK)
        sts = scores(lambda ps: ka_ref[row, ps])
        _attend([[jnp.minimum(sts[p], cap2)] for p in range(4)], [[vat_ref[c, ps, :]] for ps in pairs],
                m_ref, l_ref, acc_ref, fold, first=False)
        return carry

    lax.fori_loop(0, c_last, chunk_body, 0)
    for p, ps in enumerate(pairs):
        o_ref[:, ps] = (acc_ref[p] * _bcast_halves(1.0 / l_ref[p], 64)).T.astype(BF16)


def _dsa(proj, ki, wit, vat, batch, seq):
    n = proj.shape[0]
    nq_seq = seq // TQ
    n_real_q = batch * nq_seq
    nch = seq // TK
    top_k = min(TOPK_MAX, seq // 4)
    bidx = lambda g: jnp.minimum(g // nq_seq, batch - 1)
    meta_q = n_real_q
    meta_tile = batch * nch
    return pl.pallas_call(
        functools.partial(_dsa_kernel, seq=seq, top_k=top_k),
        grid=(n // TQ,),
        in_specs=[
            pl.BlockSpec((TQ, HEAD_W), lambda g: (g, 0)),
            pl.BlockSpec((TQ, HEAD_W), lambda g: (g, 2)),
            pl.BlockSpec((16, TQ), lambda g: (0, g)),
            pl.BlockSpec((seq, IDX_DIM), lambda g: (bidx(g), 0)),
            pl.BlockSpec((seq, HEAD_W), lambda g: (bidx(g), 1)),
            pl.BlockSpec((nch, HEAD_W, TK), lambda g: (bidx(g), 0, 0)),
            pl.BlockSpec((TQ, HEAD_W), lambda g: (meta_q, 1)),
            pl.BlockSpec((1, HEAD_W, TK), lambda g: (meta_tile, 0, 0)),
        ],
        out_specs=pl.BlockSpec((TQ, HEAD_W), lambda g: (g, 0)),
        out_shape=jax.ShapeDtypeStruct((n, HEAD_W), BF16),
        scratch_shapes=[
            pltpu.VMEM((IDX_HEADS * TQ, IDX_DIM), BF16),
            pltpu.VMEM((4, 2 * TQ, 128), BF16),
            pltpu.VMEM((nch, TK, TQ), I32),
            pltpu.VMEM((nch * GROUPS_PER_CHUNK, 32, 8, TQ), I32),
            pltpu.VMEM((nch, TK, TQ), F32),
            pltpu.VMEM((TK, TK), BF16),
            pltpu.VMEM((4, 1, 2 * TQ), F32),
            pltpu.VMEM((4, 1, 2 * TQ), F32),
            pltpu.VMEM((4, 128, TQ), F32),
        ],
        compiler_params=_params(),
        name="dsa",
    )(proj, proj, wit, ki, proj, vat, proj, vat)


def _diff_kernel(lam_ref, g_ref, qb_ref, kb_ref, vbt_ref, km_ref, vmt_ref, o_ref,
                 qstk_ref, m_ref, l_ref, acc_ref, *, seq, lam_init):
    g = pl.program_id(0)
    is_real, i, n_ch = _step_geometry(g, seq)
    lp = lam_ref[...]
    lam = (jnp.exp(jnp.sum(lp[0:1] * lp[1:2], axis=1, keepdims=True))
           - jnp.exp(jnp.sum(lp[2:3] * lp[3:4], axis=1, keepdims=True)) + lam_init)
    heads = [slice(h * 128, (h + 1) * 128) for h in range(4)]
    for h, hs in enumerate(heads):
        qstk_ref[h] = _stack_pair(qb_ref[:, hs])

    fold = (lambda ot: ot, lambda row: row)

    def scores(k_tile):
        return [lax.dot_general(k_tile(hs), qstk_ref[h], _NT, preferred_element_type=F32)
                for h, hs in enumerate(heads)]

    n_full = n_ch - is_real.astype(I32)
    row_diag = pl.ds(pl.multiple_of(n_full * TK, TK), TK)
    st_meta = scores(lambda hs: km_ref[:, hs])
    st_diag = scores(lambda hs: kb_ref[row_diag, hs])
    mcap = _meta_bias(is_real)
    q_pos = i * TQ + lax.broadcasted_iota(I32, (TK, 2 * TQ), 1) % TQ
    k_pos = n_full * TK + lax.broadcasted_iota(I32, (TK, 2 * TQ), 0)
    cap = jnp.where(jnp.where(k_pos <= q_pos, is_real.astype(I32), 0) > 0, BIG, NEG)
    _attend([[jnp.minimum(st_meta[h], mcap), jnp.minimum(st_diag[h], cap)] for h in range(4)],
            [[vmt_ref[0, hs, 0:128], vbt_ref[n_full, hs, :]] for hs in heads],
            m_ref, l_ref, acc_ref, fold, first=True)

    def chunk_body(c, carry):
        row = pl.ds(pl.multiple_of(c * TK, TK), TK)
        sts = scores(lambda hs: kb_ref[row, hs])
        _attend([[sts[h]] for h in range(4)], [[vbt_ref[c, hs, :]] for hs in heads],
                m_ref, l_ref, acc_ref, fold, first=False)
        return carry

    lax.fori_loop(0, n_full, chunk_body, 0)

    for h, hs in enumerate(heads):
        on = acc_ref[h] * (1.0 / l_ref[h])
        o = on[:, :TQ] - lam * on[:, TQ:]
        y = o * lax.rsqrt(jnp.mean(o * o, axis=0, keepdims=True) + EPS) * g_ref[...] * (1.0 - lam_init)
        o_ref[:, hs] = y.T.astype(BF16)


def _diff(proj, vbt, lam_params, subln_g, batch, seq, lam_init):
    n = proj.shape[0]
    nq_seq = seq // TQ
    nch = seq // TK
    bidx = lambda g: jnp.minimum(g // nq_seq, batch - 1)
    meta_q = batch * nq_seq
    meta_tile = batch * nch
    return pl.pallas_call(
        functools.partial(_diff_kernel, seq=seq, lam_init=lam_init),
        grid=(n // TQ,),
        in_specs=[
            _resident((4, 64), lambda g: (0, 0)),
            _resident((128, 1), lambda g: (0, 0)),
            pl.BlockSpec((TQ, HEAD_W), lambda g: (g, 3)),
            pl.BlockSpec((seq, HEAD_W), lambda g: (bidx(g), 4)),
            pl.BlockSpec((nch, HEAD_W, TK), lambda g: (bidx(g), 0, 0)),
            pl.BlockSpec((TQ, HEAD_W), lambda g: (meta_q, 4)),
            pl.BlockSpec((1, HEAD_W, TK), lambda g: (meta_tile, 0, 0)),
        ],
        out_specs=pl.BlockSpec((TQ, HEAD_W), lambda g: (g, 0)),
        out_shape=jax.ShapeDtypeStruct((n, HEAD_W), BF16),
        scratch_shapes=[
            pltpu.VMEM((4, 2 * TQ, 128), BF16),
            pltpu.VMEM((4, 1, 2 * TQ), F32),
            pltpu.VMEM((4, 1, 2 * TQ), F32),
            pltpu.VMEM((4, 128, 2 * TQ), F32),
        ],
        compiler_params=_params(),
        name="diff_attn",
    )(lam_params, subln_g.reshape(128, 1), proj, proj, vbt, proj, vbt)


def _out_proj_kernel(h_ref, oa_ref, ob_ref, wa_ref, wb_ref, o_ref):
    o_ref[...] = (h_ref[...]
                  + jnp.dot(oa_ref[...], wa_ref[...], preferred_element_type=F32)
                  + jnp.dot(ob_ref[...], wb_ref[...], preferred_element_type=F32))


def _out_proj(h, oa, ob, w_out):
    n = h.shape[0]
    return pl.pallas_call(
        _out_proj_kernel,
        grid=(n // TM,),
        in_specs=[
            pl.BlockSpec((TM, D), lambda i: (i, 0)),
            pl.BlockSpec((TM, HEAD_W), lambda i: (i, 0)),
            pl.BlockSpec((TM, HEAD_W), lambda i: (i, 0)),
            _resident((HEAD_W, D), lambda i: (0, 0)),
            _resident((HEAD_W, D), lambda i: (1, 0)),
        ],
        out_specs=pl.BlockSpec((TM, D), lambda i: (i, 0)),
        out_shape=jax.ShapeDtypeStruct((n, D), F32),
        compiler_params=_params(),
        name="attn_out_proj",
    )(h, oa, ob, w_out, w_out)


def _rglru_kernel(x_ref, g_ref, win_ref, cw_ref, cb_ref, gw_ref, gb_ref, lam_ref, wout_ref, o_ref,
                  xbuf, a_s, u_s, hstate, mstate, mhist, *, tiles_per_seq):
    g = pl.program_id(0)
    x = x_ref[...]
    xn = _rms(x, g_ref[...]).astype(BF16)
    yx = jnp.dot(xn, win_ref[...], preferred_element_type=F32)
    yb = yx[:, :D]
    y_br = 0.5 * yb * (1.0 + jnp.tanh(math.sqrt(2.0 / math.pi) * (yb + 0.044715 * (yb * yb * yb))))

    @pl.when(g == 0)
    def _():
        xbuf[0:8, :] = jnp.zeros((8, D), F32)
        hstate[...] = jnp.zeros((1, D), F32)

    @pl.when(jnp.logical_and(g >= 1, (g - 1) % tiles_per_seq == 0))
    def _():
        xbuf[0:8, :] = mhist[...]
        hstate[...] = mstate[...]

    xbuf[8:, :] = yx[:, D:]
    xc = cb_ref[...] + cw_ref[0:1, :] * xbuf[5:5 + TM, :]
    for j in range(1, CONV_W):
        xc = xc + cw_ref[j:j + 1, :] * xbuf[5 + j:5 + j + TM, :]

    gx, ga = [], []
    for nb in range(LRU_BLOCKS):
        xb = xc[:, nb * LRU_BLOCK_W:(nb + 1) * LRU_BLOCK_W].astype(BF16)
        gx.append(jnp.dot(xb, gw_ref[0, nb], preferred_element_type=F32))
        ga.append(jnp.dot(xb, gw_ref[1, nb], preferred_element_type=F32))
    gate_x = _sigmoid(jnp.concatenate(gx, axis=1) + gb_ref[0:1, :])
    gate_a = _sigmoid(jnp.concatenate(ga, axis=1) + gb_ref[1:2, :])
    lam = lam_ref[...]
    log_sig = jnp.minimum(lam, 0.0) - jnp.log(1.0 + jnp.exp(-jnp.abs(lam)))
    log_a = RG_C * gate_a * log_sig
    a = jnp.exp(log_a)
    a_s[...] = a
    u_s[...] = jnp.sqrt(-jnp.tanh(log_a) * (a * a + 1.0)) * (gate_x * xc)

    def step(t, hcur):
        hcur = a_s[pl.ds(t, 1), :] * hcur + u_s[pl.ds(t, 1), :]
        u_s[pl.ds(t, 1), :] = hcur
        return hcur

    hstate[...] = lax.fori_loop(0, TM, step, hstate[...], unroll=8)

    @pl.when(g == 0)
    def _():
        mstate[...] = u_s[N_META - 1:N_META, :]
        mhist[...] = xbuf[N_META:N_META + 8, :]

    xbuf[0:8, :] = xbuf[TM:TM + 8, :]
    o_ref[...] = x + jnp.dot((u_s[...] * y_br).astype(BF16), wout_ref[...], preferred_element_type=F32)


def _rglru(h, g, w_in, conv_w, conv_b, gate_w, gate_b, lru_lambda, w_out, tiles_per_seq):
    n = h.shape[0]
    nt = n // TM
    tile = lambda s: (jnp.where(s == 0, nt - 1, s - 1), 0)
    const2 = lambda s: (0, 0)
    return pl.pallas_call(
        functools.partial(_rglru_kernel, tiles_per_seq=tiles_per_seq),
        grid=(nt,),
        in_specs=[
            pl.BlockSpec((TM, D), tile),
            _resident((1, D), const2),
            _resident((D, 2 * D), const2),
            _resident((CONV_W, D), const2),
            _resident((1, D), const2),
            _resident((2, LRU_BLOCKS, LRU_BLOCK_W, LRU_BLOCK_W), lambda s: (0, 0, 0, 0)),
            _resident((2, D), const2),
            _resident((1, D), const2),
            _resident((D, D), const2),
        ],
        out_specs=pl.BlockSpec((TM, D), tile),
        out_shape=jax.ShapeDtypeStruct((n, D), F32),
        scratch_shapes=[
            pltpu.VMEM((TM + 8, D), F32),
            pltpu.VMEM((TM, D), F32),
            pltpu.VMEM((TM, D), F32),
            pltpu.VMEM((1, D), F32),
            pltpu.VMEM((1, D), F32),
            pltpu.VMEM((8, D), F32),
        ],
        compiler_params=_params(),
        name="rglru",
    )(h, g.reshape(1, D), w_in, conv_w, conv_b.reshape(1, D), gate_w, gate_b, lru_lambda.reshape(1, D), w_out)


def _split_attn_w_in(w):
    a = HEAD_W
    o_ki = 4 * a
    o_wi = o_ki + IDX_DIM
    o_qb = o_wi + IDX_HEADS
    qa, ka, va, qi = w[:, 0:a], w[:, a:2 * a], w[:, 2 * a:3 * a], w[:, 3 * a:4 * a]
    ki, wi = w[:, o_ki:o_wi], w[:, o_wi:o_qb]
    qb, kb, vb = w[:, o_qb:o_qb + a], w[:, o_qb + a:o_qb + 2 * a], w[:, o_qb + 2 * a:o_qb + 3 * a]
    w_main = jnp.concatenate([qa, ka, qi, qb, kb, va, vb], axis=1).astype(BF16)
    w_wt = jnp.concatenate([wi.T, jnp.zeros((16 - IDX_HEADS, D), w.dtype)], axis=0).astype(BF16)
    return w_main, ki.astype(BF16), w_wt


def kernel(x, meta_tokens, norm_g, ffn_w_gu, ffn_w_down, attn_w_in, idx_k_ln_g, idx_k_ln_b, diff_lambda, diff_subln_g, attn_w_out, rec_w_in, rec_conv_w, rec_conv_b, rec_gate_w, rec_gate_b, rec_lambda, rec_w_out, final_norm_g):
    batch, seq, _ = x.shape
    depth = norm_g.shape[0]
    n_real = batch * seq
    h = jnp.concatenate([x.reshape(n_real, D), meta_tokens.astype(x.dtype),
                         jnp.zeros((TM - N_META, D), x.dtype)], axis=0)
    w_gu = ffn_w_gu.astype(BF16)
    w_down = ffn_w_down.astype(BF16)
    for i in range(depth):
        j = i // 2
        h = _ffn(h, norm_g[i, 0], w_gu[i, 0], w_down[i, 0])
        if i % 2 == 0:
            lam_init = 0.8 - 0.6 * math.exp(-0.3 * i)
            w_main, w_ki, w_wt = _split_attn_w_in(attn_w_in[j])
            proj, ki, wit, vat, vbt = _attn_proj(h, norm_g[i, 1], w_main, w_ki, w_wt, idx_k_ln_g[j], idx_k_ln_b[j])
            oa = _dsa(proj, ki, wit, vat, batch, seq)
            ob = _diff(proj, vbt, diff_lambda[j], diff_subln_g[j], batch, seq, lam_init)
            h = _out_proj(h, oa, ob, attn_w_out[j].astype(BF16))
        else:
            h = _rglru(h, norm_g[i, 1], rec_w_in[j].astype(BF16), rec_conv_w[j], rec_conv_b[j],
                       rec_gate_w[j].astype(BF16), rec_gate_b[j], rec_lambda[j], rec_w_out[j].astype(BF16),
                       seq // TM)
        if i == depth - 1:
            h = _ffn(h, norm_g[i, 2], w_gu[i, 1], w_down[i, 1], final_g=final_norm_g, n_rows=n_real)
        else:
            h = _ffn(h, norm_g[i, 2], w_gu[i, 1], w_down[i, 1])
    return h.reshape(batch, seq, D)
```

```python
import functools
import math

import jax
import jax.numpy as jnp
from jax import lax
from jax.experimental import pallas as pl
from jax.experimental.pallas import tpu as pltpu

F32 = jnp.float32
BF16 = jnp.bfloat16
I32 = jnp.int32

D = 1024
N_META = 16
D_FF = 2816
EPS = 1e-6
HEAD_W = 512
IDX_DIM = 64
IDX_HEADS = 8
TOPK_MAX = 256
LRU_BLOCKS = 4
LRU_BLOCK_W = D // LRU_BLOCKS
CONV_W = 4
RG_C = 8.0

TM = 512
TQ = 128
TK = 512
FF_CHUNK = 1408
NEG = -1e30
BIG = 1e30
QK_SCALE = IDX_DIM ** -0.5 * math.log2(math.e)
INT_MIN = -2 ** 31
VMEM_LIMIT = 56 * 1024 * 1024

_NT = (((1,), (1,)), ((), ()))


def _params():
    return pltpu.CompilerParams(dimension_semantics=("arbitrary",), vmem_limit_bytes=VMEM_LIMIT)


def _resident(shape, index_map):
    return pl.BlockSpec(shape, index_map, pipeline_mode=pl.Buffered(1))


def _rms(x, g):
    return x * lax.rsqrt(jnp.mean(x * x, axis=-1, keepdims=True) + EPS) * g


def _sigmoid(x):
    return 0.5 * jnp.tanh(0.5 * x) + 0.5


def _ffn_kernel(x_ref, *refs, final, n_real_tiles):
    refs = list(refs)
    o_ref = refs.pop()
    tail_ref = refs.pop(0) if n_real_tiles is not None else None
    g_ref, wg_ref, wu_ref, wd_ref = refs[:4]
    x = x_ref[...]
    if tail_ref is not None:
        x = jnp.where(pl.program_id(0) < n_real_tiles, x, tail_ref[...])
    xn = _rms(x, g_ref[...]).astype(BF16)
    acc = None
    for c in range(D_FF // FF_CHUNK):
        sl = slice(c * FF_CHUNK, (c + 1) * FF_CHUNK)
        g = jnp.dot(xn, wg_ref[:, sl], preferred_element_type=F32)
        u = jnp.dot(xn, wu_ref[:, sl], preferred_element_type=F32)
        a = (g * _sigmoid(g) * u).astype(BF16)
        part = jnp.dot(a, wd_ref[sl, :], preferred_element_type=F32)
        acc = part if acc is None else acc + part
    y = x + 0.5 * acc
    if final:
        y = _rms(y, refs[4][...])
    o_ref[...] = y


def _ffn(h, g, w_gu, w_down, layer, half, tail=None, final_g=None, n_rows=None):
    n_real_tiles = None if tail is None else h.shape[0] // TM
    if n_rows is None:
        n_rows = h.shape[0] + (0 if tail is None else TM)
    final = final_g is not None
    if tail is None:
        in_specs, args = [pl.BlockSpec((TM, D), lambda i: (i, 0))], [h]
    else:
        in_specs = [pl.BlockSpec((TM, D), lambda i: (jnp.minimum(i, n_real_tiles - 1), 0)),
                    _resident((TM, D), lambda i: (0, 0))]
        args = [h, tail]
    in_specs += [
        _resident((1, D), lambda i: (0, 0)),
        _resident((None, None, D, D_FF), lambda i: (layer, half, 0, 0)),
        _resident((None, None, D, D_FF), lambda i: (layer, half, 0, 1)),
        _resident((None, None, D_FF, D), lambda i: (layer, half, 0, 0)),
    ]
    args += [g.reshape(1, D), w_gu, w_gu, w_down]
    if final:
        in_specs.append(_resident((1, D), lambda i: (0, 0)))
        args.append(final_g.reshape(1, D))
    return pl.pallas_call(
        functools.partial(_ffn_kernel, final=final, n_real_tiles=n_real_tiles),
        grid=(n_rows // TM,),
        in_specs=in_specs,
        out_specs=pl.BlockSpec((TM, D), lambda i: (i, 0)),
        out_shape=jax.ShapeDtypeStruct((n_rows, D), F32),
        compiler_params=_params(),
        name="ffn",
    )(*args)


N_PROJ = 5 * HEAD_W


def _attn_proj_kernel(x_ref, g_ref, w_ref, wki_ref, wwt_ref, lng_ref, lnb_ref,
                      proj_ref, ki_ref, wit_ref, vat_ref, vbt_ref):
    xn = _rms(x_ref[...], g_ref[...]).astype(BF16)
    y = jnp.dot(xn, w_ref[...], preferred_element_type=F32)
    scale = QK_SCALE
    proj_ref[:, 0:HEAD_W] = (y[:, 0:HEAD_W] * scale).astype(BF16)
    proj_ref[:, HEAD_W:3 * HEAD_W] = y[:, HEAD_W:3 * HEAD_W].astype(BF16)
    proj_ref[:, 3 * HEAD_W:4 * HEAD_W] = (y[:, 3 * HEAD_W:4 * HEAD_W] * scale).astype(BF16)
    proj_ref[:, 4 * HEAD_W:5 * HEAD_W] = y[:, 4 * HEAD_W:5 * HEAD_W].astype(BF16)
    vat_ref[0] = y[:, 5 * HEAD_W:6 * HEAD_W].T.astype(BF16)
    vbt_ref[0] = y[:, 6 * HEAD_W:7 * HEAD_W].T.astype(BF16)
    yk = jnp.dot(xn, wki_ref[...], preferred_element_type=F32)
    mu = jnp.mean(yk, axis=-1, keepdims=True)
    yc = yk - mu
    ln = yc * lax.rsqrt(jnp.mean(yc * yc, axis=-1, keepdims=True) + EPS)
    ki_ref[...] = (ln * lng_ref[...] + lnb_ref[...]).astype(BF16)
    wit = lax.dot_general(wwt_ref[...], xn, _NT, preferred_element_type=F32)
    wit_ref[...] = wit * (IDX_HEADS ** -0.5 * IDX_DIM ** -0.5)


def _attn_proj(h, g, w_main, w_ki, w_wt, ln_g, ln_b):
    n = h.shape[0]
    nt = n // TM
    const = lambda i: (0, 0)
    return pl.pallas_call(
        _attn_proj_kernel,
        grid=(nt,),
        in_specs=[
            pl.BlockSpec((TM, D), lambda i: (i, 0)),
            _resident((1, D), const),
            _resident((D, 7 * HEAD_W), const),
            _resident((D, IDX_DIM), const),
            _resident((16, D), const),
            _resident((1, IDX_DIM), const),
            _resident((1, IDX_DIM), const),
        ],
        out_specs=[
            pl.BlockSpec((TM, N_PROJ), lambda i: (i, 0)),
            pl.BlockSpec((TM, IDX_DIM), lambda i: (i, 0)),
            pl.BlockSpec((16, TM), lambda i: (0, i)),
            pl.BlockSpec((1, HEAD_W, TM), lambda i: (i, 0, 0)),
            pl.BlockSpec((1, HEAD_W, TM), lambda i: (i, 0, 0)),
        ],
        out_shape=[
            jax.ShapeDtypeStruct((n, N_PROJ), BF16),
            jax.ShapeDtypeStruct((n, IDX_DIM), BF16),
            jax.ShapeDtypeStruct((16, n), F32),
            jax.ShapeDtypeStruct((nt, HEAD_W, TM), BF16),
            jax.ShapeDtypeStruct((nt, HEAD_W, TM), BF16),
        ],
        compiler_params=_params(),
        name="attn_proj",
    )(h, g.reshape(1, D), w_main, w_ki, w_wt, ln_g.reshape(1, IDX_DIM), ln_b.reshape(1, IDX_DIM))


def _step_geometry(g, seq):
    nq_seq = seq // TQ
    n_real = pl.num_programs(0) - TM // TQ
    is_real = g < n_real
    i = g % nq_seq
    n_ch = jnp.where(is_real, (i * TQ + TQ + TK - 1) // TK, 0)
    return is_real, i, n_ch


def _stack_pair(q_pair):
    lane = lax.broadcasted_iota(I32, (TQ, 128), 1)
    qf = q_pair.astype(F32)
    top = jnp.where(lane < 64, qf, 0.0).astype(BF16)
    bot = jnp.where(lane >= 64, qf, 0.0).astype(BF16)
    return jnp.concatenate([top, bot], axis=0)


def _meta_bias(is_real):
    j = lax.broadcasted_iota(I32, (128, 2 * TQ), 0)
    t = lax.broadcasted_iota(I32, (128, 2 * TQ), 1) % TQ
    real = is_real.astype(I32)
    lim = jnp.minimum(real * N_META + (1 - real) * (t + 1), N_META)
    return jnp.where(j < lim, BIG, NEG)


def _bcast_halves(row, n_sub):
    return jnp.concatenate([jnp.broadcast_to(row[:, :TQ], (n_sub, TQ)),
                            jnp.broadcast_to(row[:, TQ:], (n_sub, TQ))], axis=0)


SCORE_LOOKAHEAD = 2
MAX_LAG = 64.0


def _softmax_pv(segments, v_tiles, base):
    issued, out = {}, []
    for h in range(SCORE_LOOKAHEAD):
        issued[h] = [fn() for fn in segments[h]]
    for h in range(4):
        if h + SCORE_LOOKAHEAD < 4:
            issued[h + SCORE_LOOKAHEAD] = [fn() for fn in segments[h + SCORE_LOOKAHEAD]]
        sts = issued.pop(h)
        m_loc = jnp.max(sts[0], axis=0, keepdims=True)
        for st in sts[1:]:
            m_loc = jnp.maximum(m_loc, jnp.max(st, axis=0, keepdims=True))
        m_use = base(h, m_loc)
        l_new, pv = None, None
        for st, vt in zip(sts, v_tiles[h]):
            pc = jnp.exp2(st - m_use)
            s = jnp.sum(pc, axis=0, keepdims=True)
            d = jnp.dot(vt(), pc.astype(BF16), preferred_element_type=F32)
            l_new = s if l_new is None else l_new + s
            pv = d if pv is None else pv + d
        out.append((m_loc, l_new, pv))
    return out


def _attend(segments, v_tiles, m_ref, l_ref, acc_ref, fold, first):
    fold_pv, fold_row = fold
    if first:
        for h, (m_loc, l_new, pv) in enumerate(_softmax_pv(segments, v_tiles, lambda h, m_loc: m_loc)):
            m_ref[h] = m_loc
            l_ref[h] = l_new
            acc_ref[h] = fold_pv(pv)
        return

    def commit(parts, rebased):
        for h, (m_loc, l_new, pv) in enumerate(parts):
            m_new = jnp.maximum(m_ref[h], m_loc)
            alpha = jnp.exp2(m_ref[h] - m_new)
            if rebased:
                l_ref[h] = alpha * l_ref[h] + l_new
                acc_ref[h] = acc_ref[h] * fold_row(alpha) + fold_pv(pv)
            else:
                l_ref[h] = alpha * (l_ref[h] + l_new)
                acc_ref[h] = (acc_ref[h] + fold_pv(pv)) * fold_row(alpha)
            m_ref[h] = m_new

    parts = _softmax_pv(segments, v_tiles, lambda h, m_loc: m_ref[h])
    rise = parts[0][0] - m_ref[0]
    for h in range(1, 4):
        rise = jnp.maximum(rise, parts[h][0] - m_ref[h])
    safe = jnp.max(rise) <= MAX_LAG

    @pl.when(safe)
    def _():
        commit(parts, rebased=False)

    @pl.when(jnp.logical_not(safe))
    def _():
        commit(_softmax_pv(segments, v_tiles, lambda h, m_loc: jnp.maximum(m_ref[h], m_loc)), rebased=True)


GROUP_KEYS = 256
GROUPS_PER_CHUNK = TK // GROUP_KEYS


def _bit_transpose32(words):
    a = list(words)
    j, mask = 16, 0x0000FFFF
    while j:
        k = 0
        while k < 32:
            t = (a[k] ^ lax.shift_right_logical(a[k + j], jnp.int32(j))) & mask
            a[k] = a[k] ^ t
            a[k + j] = a[k + j] ^ lax.shift_left(t, jnp.int32(j))
            k = (k + j + 1) & ~j
        j >>= 1
        mask ^= mask << j
    return a


def _dsa_kernel(qa_ref, qi_ref, wit_ref, ki_ref, ka_ref, vat_ref, km_ref, vmt_ref, o_ref,
                qall_ref, qstk_ref, key_ref, plane_ref, bias_ref, tri_ref, m_ref, l_ref, acc_ref, *, seq, top_k):
    g = pl.program_id(0)

    @pl.when(g == 0)
    def _():
        plane_ref[...] = jnp.zeros(plane_ref.shape, I32)
        r = lax.broadcasted_iota(I32, (TK, TK), 0)
        c_ = lax.broadcasted_iota(I32, (TK, TK), 1)
        tri_ref[...] = jnp.where(c_ <= r, 1.0, 0.0).astype(BF16)

    is_real, i, n_ch = _step_geometry(g, seq)
    q_pos = i * TQ + lax.broadcasted_iota(I32, (TK, TQ), 1)
    k_off = lax.broadcasted_iota(I32, (TK, TQ), 0)

    qi = qi_ref[...]
    for h in range(IDX_HEADS):
        qall_ref[h * TQ:(h + 1) * TQ, :] = qi[:, h * IDX_DIM:(h + 1) * IDX_DIM]
    for p in range(4):
        qstk_ref[p] = _stack_pair(qa_ref[:, p * 128:(p + 1) * 128])

    wt = wit_ref[...]

    def score_body(c, carry):
        kc = ki_ref[pl.ds(pl.multiple_of(c * TK, TK), TK), :]
        logits = lax.dot_general(kc, qall_ref[...], _NT, preferred_element_type=F32)
        sc = jnp.zeros((TK, TQ), F32)
        for h in range(IDX_HEADS):
            sc = sc + wt[h:h + 1, :] * jnp.maximum(logits[:, h * TQ:(h + 1) * TQ], 0.0)
        sc = jnp.where(sc == 0.0, 0.0, sc)
        bits = pltpu.bitcast(sc, I32)
        key = bits ^ (lax.shift_right_arithmetic(bits, 31) & 0x7FFFFFFF)
        key = jnp.where(c * TK + k_off <= q_pos, key, INT_MIN)
        key_ref[c] = key
        u = key ^ INT_MIN
        for gi in range(GROUPS_PER_CHUNK):
            base = gi * GROUP_KEYS
            planes = _bit_transpose32([u[base + w * 8:base + (w + 1) * 8, :] for w in range(32)])
            for b in range(32):
                plane_ref[c * GROUPS_PER_CHUNK + gi, b] = planes[b]
        return carry

    lax.fori_loop(0, n_ch, score_body, 0)

    n_grp = n_ch * GROUPS_PER_CHUNK
    n_groups = plane_ref.shape[0]
    alive0 = tuple(jnp.full((8, TQ), -1, I32) * (gq < n_grp).astype(I32) for gq in range(n_groups))

    def bit_body(b, carry):
        alive, k_rem, thr_u = carry
        ones = [a & plane_ref[gq, b] for gq, a in enumerate(alive)]
        cnt8 = lax.population_count(ones[0])
        for o in ones[1:]:
            cnt8 = cnt8 + lax.population_count(o)
        cnt = jnp.sum(cnt8, axis=0, keepdims=True)
        take = cnt >= k_rem
        alive = tuple(jnp.where(take, o, a ^ o) for a, o in zip(alive, ones))
        k_rem = jnp.where(take, k_rem, k_rem - cnt)
        thr_u = thr_u | jnp.where(take, lax.shift_left(jnp.int32(1), 31 - b), 0)
        return alive, k_rem, thr_u

    alive, k_rem, thr_u = lax.fori_loop(
        0, 32, bit_body, (alive0, jnp.full((1, TQ), top_k, I32), jnp.zeros((1, TQ), I32)))
    thr = thr_u ^ INT_MIN
    eq8 = lax.population_count(alive[0])
    for a in alive[1:]:
        eq8 = eq8 + lax.population_count(a)
    cnt_eq = jnp.sum(eq8, axis=0, keepdims=True)
    excess = jnp.where(thr_u != 0, cnt_eq - k_rem, 0)
    has_excess = jnp.max(excess) > 0

    @pl.when(jnp.logical_not(has_excess))
    def _():
        thr_eff = jnp.maximum(thr, INT_MIN + 1)

        def cap_body(c, carry):
            bias_ref[c] = jnp.where(key_ref[c] >= thr_eff, BIG, NEG)
            return carry

        lax.fori_loop(0, n_ch, cap_body, 0)

    @pl.when(has_excess)
    def _():
        need = k_rem.astype(F32)

        def cap_body(c, run):
            k = key_ref[c]
            eq = k == thr
            pre = jnp.dot(tri_ref[...], jnp.where(eq, 1.0, 0.0).astype(BF16), preferred_element_type=F32)
            rank = run + pre
            b = jnp.where(k > thr, BIG, jnp.where(eq, jnp.where(rank <= need, BIG, NEG), NEG))
            bias_ref[c] = jnp.where(k == INT_MIN, NEG, b)
            return run + pre[TK - 1:TK, :]

        lax.fori_loop(0, n_ch, cap_body, jnp.zeros((1, TQ), F32))

    fold = (lambda ot: jnp.concatenate([ot[:64, :TQ], ot[64:, TQ:]], axis=0), lambda row: _bcast_halves(row, 64))
    pairs = [slice(p * 128, (p + 1) * 128) for p in range(4)]

    def score(p, k_ref, idx, cap):
        return lambda: jnp.minimum(lax.dot_general(k_ref[idx], qstk_ref[p], _NT, preferred_element_type=F32), cap)

    def tile(ref, idx):
        return lambda: ref[idx]

    c_last = jnp.maximum(n_ch - 1, 0)
    row_last = pl.ds(pl.multiple_of(c_last * TK, TK), TK)
    mcap = _meta_bias(is_real)
    cap = jnp.where(is_real, bias_ref[c_last], NEG)
    cap2 = jnp.concatenate([cap, cap], axis=1)
    _attend([[score(p, km_ref, (slice(None), ps), mcap), score(p, ka_ref, (row_last, ps), cap2)]
             for p, ps in enumerate(pairs)],
            [[tile(vmt_ref, (0, ps, slice(0, 128))), tile(vat_ref, (c_last, ps, slice(None)))] for ps in pairs],
            m_ref, l_ref, acc_ref, fold, first=True)

    def update(chunks):
        rows = [pl.ds(pl.multiple_of(c * TK, TK), TK) for c in chunks]
        caps = [jnp.concatenate([bias_ref[c]] * 2, axis=1) for c in chunks]
        _attend([[score(p, ka_ref, (row, ps), cap2) for row, cap2 in zip(rows, caps)] for p, ps in enumerate(pairs)],
                [[tile(vat_ref, (c, ps, slice(None))) for c in chunks] for ps in pairs],
                m_ref, l_ref, acc_ref, fold, first=False)

    def pair_body(j, carry):
        update([2 * j, 2 * j + 1])
        return carry

    lax.fori_loop(0, c_last // 2, pair_body, 0)

    @pl.when(c_last % 2 == 1)
    def _():
        update([c_last - 1])

    for p, ps in enumerate(pairs):
        o_ref[:, ps] = (acc_ref[p] * _bcast_halves(1.0 / l_ref[p], 64)).T.astype(BF16)


def _dsa(proj, ki, wit, vat, batch, seq):
    n = proj.shape[0]
    nq_seq = seq // TQ
    n_real_q = batch * nq_seq
    nch = seq // TK
    top_k = min(TOPK_MAX, seq // 4)
    bidx = lambda g: jnp.minimum(g // nq_seq, batch - 1)
    meta_q = n_real_q
    meta_tile = batch * nch
    return pl.pallas_call(
        functools.partial(_dsa_kernel, seq=seq, top_k=top_k),
        grid=(n // TQ,),
        in_specs=[
            pl.BlockSpec((TQ, HEAD_W), lambda g: (g, 0)),
            pl.BlockSpec((TQ, HEAD_W), lambda g: (g, 2)),
            pl.BlockSpec((16, TQ), lambda g: (0, g)),
            pl.BlockSpec((seq, IDX_DIM), lambda g: (bidx(g), 0)),
            pl.BlockSpec((seq, HEAD_W), lambda g: (bidx(g), 1)),
            pl.BlockSpec((nch, HEAD_W, TK), lambda g: (bidx(g), 0, 0)),
            pl.BlockSpec((TQ, HEAD_W), lambda g: (meta_q, 1)),
            pl.BlockSpec((1, HEAD_W, TK), lambda g: (meta_tile, 0, 0)),
        ],
        out_specs=pl.BlockSpec((TQ, HEAD_W), lambda g: (g, 0)),
        out_shape=jax.ShapeDtypeStruct((n, HEAD_W), BF16),
        scratch_shapes=[
            pltpu.VMEM((IDX_HEADS * TQ, IDX_DIM), BF16),
            pltpu.VMEM((4, 2 * TQ, 128), BF16),
            pltpu.VMEM((nch, TK, TQ), I32),
            pltpu.VMEM((nch * GROUPS_PER_CHUNK, 32, 8, TQ), I32),
            pltpu.VMEM((nch, TK, TQ), F32),
            pltpu.VMEM((TK, TK), BF16),
            pltpu.VMEM((4, 1, 2 * TQ), F32),
            pltpu.VMEM((4, 1, 2 * TQ), F32),
            pltpu.VMEM((4, 128, TQ), F32),
        ],
        compiler_params=_params(),
        name="dsa",
    )(proj, proj, wit, ki, proj, vat, proj, vat)


def _diff_kernel(lam_ref, g_ref, qb_ref, kb_ref, vbt_ref, km_ref, vmt_ref, o_ref,
                 qstk_ref, m_ref, l_ref, acc_ref, *, seq, lam_init):
    g = pl.program_id(0)
    is_real, i, n_ch = _step_geometry(g, seq)
    lp = lam_ref[...]
    lam = (jnp.exp(jnp.sum(lp[0:1] * lp[1:2], axis=1, keepdims=True))
           - jnp.exp(jnp.sum(lp[2:3] * lp[3:4], axis=1, keepdims=True)) + lam_init)
    heads = [slice(h * 128, (h + 1) * 128) for h in range(4)]
    for h, hs in enumerate(heads):
        qstk_ref[h] = _stack_pair(qb_ref[:, hs])

    fold = (lambda ot: ot, lambda row: row)

    def score(h, k_ref, idx, cap=None):
        def fn():
            st = lax.dot_general(k_ref[idx], qstk_ref[h], _NT, preferred_element_type=F32)
            return st if cap is None else jnp.minimum(st, cap)
        return fn

    def tile(ref, idx):
        return lambda: ref[idx]

    n_full = n_ch - is_real.astype(I32)
    row_diag = pl.ds(pl.multiple_of(n_full * TK, TK), TK)
    mcap = _meta_bias(is_real)
    q_pos = i * TQ + lax.broadcasted_iota(I32, (TK, 2 * TQ), 1) % TQ
    k_pos = n_full * TK + lax.broadcasted_iota(I32, (TK, 2 * TQ), 0)
    cap = jnp.where(jnp.where(k_pos <= q_pos, is_real.astype(I32), 0) > 0, BIG, NEG)
    _attend([[score(h, km_ref, (slice(None), hs), mcap), score(h, kb_ref, (row_diag, hs), cap)]
             for h, hs in enumerate(heads)],
            [[tile(vmt_ref, (0, hs, slice(0, 128))), tile(vbt_ref, (n_full, hs, slice(None)))] for hs in heads],
            m_ref, l_ref, acc_ref, fold, first=True)

    def update(chunks):
        rows = [pl.ds(pl.multiple_of(c * TK, TK), TK) for c in chunks]
        _attend([[score(h, kb_ref, (row, hs)) for row in rows] for h, hs in enumerate(heads)],
                [[tile(vbt_ref, (c, hs, slice(None))) for c in chunks] for hs in heads],
                m_ref, l_ref, acc_ref, fold, first=False)

    def pair_body(j, carry):
        update([2 * j, 2 * j + 1])
        return carry

    lax.fori_loop(0, n_full // 2, pair_body, 0)

    @pl.when(n_full % 2 == 1)
    def _():
        update([n_full - 1])

    for h, hs in enumerate(heads):
        on = acc_ref[h] * (1.0 / l_ref[h])
        o = on[:, :TQ] - lam * on[:, TQ:]
        y = o * lax.rsqrt(jnp.mean(o * o, axis=0, keepdims=True) + EPS) * g_ref[...] * (1.0 - lam_init)
        o_ref[:, hs] = y.T.astype(BF16)


def _diff(proj, vbt, lam_params, subln_g, batch, seq, lam_init):
    n = proj.shape[0]
    nq_seq = seq // TQ
    nch = seq // TK
    bidx = lambda g: jnp.minimum(g // nq_seq, batch - 1)
    meta_q = batch * nq_seq
    meta_tile = batch * nch
    return pl.pallas_call(
        functools.partial(_diff_kernel, seq=seq, lam_init=lam_init),
        grid=(n // TQ,),
        in_specs=[
            _resident((4, 64), lambda g: (0, 0)),
            _resident((128, 1), lambda g: (0, 0)),
            pl.BlockSpec((TQ, HEAD_W), lambda g: (g, 3)),
            pl.BlockSpec((seq, HEAD_W), lambda g: (bidx(g), 4)),
            pl.BlockSpec((nch, HEAD_W, TK), lambda g: (bidx(g), 0, 0)),
            pl.BlockSpec((TQ, HEAD_W), lambda g: (meta_q, 4)),
            pl.BlockSpec((1, HEAD_W, TK), lambda g: (meta_tile, 0, 0)),
        ],
        out_specs=pl.BlockSpec((TQ, HEAD_W), lambda g: (g, 0)),
        out_shape=jax.ShapeDtypeStruct((n, HEAD_W), BF16),
        scratch_shapes=[
            pltpu.VMEM((4, 2 * TQ, 128), BF16),
            pltpu.VMEM((4, 1, 2 * TQ), F32),
            pltpu.VMEM((4, 1, 2 * TQ), F32),
            pltpu.VMEM((4, 128, 2 * TQ), F32),
        ],
        compiler_params=_params(),
        name="diff_attn",
    )(lam_params, subln_g.reshape(128, 1), proj, proj, vbt, proj, vbt)


def _out_proj_kernel(h_ref, oa_ref, ob_ref, wa_ref, wb_ref, o_ref):
    o_ref[...] = (h_ref[...]
                  + jnp.dot(oa_ref[...], wa_ref[...], preferred_element_type=F32)
                  + jnp.dot(ob_ref[...], wb_ref[...], preferred_element_type=F32))


def _out_proj(h, oa, ob, w_out):
    n = h.shape[0]
    return pl.pallas_call(
        _out_proj_kernel,
        grid=(n // TM,),
        in_specs=[
            pl.BlockSpec((TM, D), lambda i: (i, 0)),
            pl.BlockSpec((TM, HEAD_W), lambda i: (i, 0)),
            pl.BlockSpec((TM, HEAD_W), lambda i: (i, 0)),
            _resident((HEAD_W, D), lambda i: (0, 0)),
            _resident((HEAD_W, D), lambda i: (1, 0)),
        ],
        out_specs=pl.BlockSpec((TM, D), lambda i: (i, 0)),
        out_shape=jax.ShapeDtypeStruct((n, D), F32),
        compiler_params=_params(),
        name="attn_out_proj",
    )(h, oa, ob, w_out, w_out)


def _rglru_kernel(x_ref, g_ref, win_ref, cw_ref, cb_ref, gw_ref, gb_ref, lam_ref, wout_ref, o_ref,
                  xbuf, a_s, u_s, hstate, mstate, mhist, *, tiles_per_seq):
    g = pl.program_id(0)
    x = x_ref[...]
    xn = _rms(x, g_ref[...]).astype(BF16)
    yx = jnp.dot(xn, win_ref[...], preferred_element_type=F32)
    yb = yx[:, :D]
    y_br = 0.5 * yb * (1.0 + jnp.tanh(math.sqrt(2.0 / math.pi) * (yb + 0.044715 * (yb * yb * yb))))

    @pl.when(g == 0)
    def _():
        xbuf[0:8, :] = jnp.zeros((8, D), F32)
        hstate[...] = jnp.zeros(hstate.shape, F32)

    @pl.when(jnp.logical_and(g >= 1, (g - 1) % tiles_per_seq == 0))
    def _():
        xbuf[0:8, :] = mhist[...]
        hstate[...] = mstate[...]

    xbuf[8:, :] = yx[:, D:]
    xc = cb_ref[...] + cw_ref[0:1, :] * xbuf[5:5 + TM, :]
    for j in range(1, CONV_W):
        xc = xc + cw_ref[j:j + 1, :] * xbuf[5 + j:5 + j + TM, :]

    gx, ga = [], []
    for nb in range(LRU_BLOCKS):
        xb = xc[:, nb * LRU_BLOCK_W:(nb + 1) * LRU_BLOCK_W].astype(BF16)
        gx.append(jnp.dot(xb, gw_ref[0, nb], preferred_element_type=F32))
        ga.append(jnp.dot(xb, gw_ref[1, nb], preferred_element_type=F32))
    gate_x = _sigmoid(jnp.concatenate(gx, axis=1) + gb_ref[0:1, :])
    gate_a = _sigmoid(jnp.concatenate(ga, axis=1) + gb_ref[1:2, :])
    lam = lam_ref[...]
    log_sig = jnp.minimum(lam, 0.0) - jnp.log(1.0 + jnp.exp(-jnp.abs(lam)))
    log_a = RG_C * gate_a * log_sig
    a = jnp.exp(log_a)
    a_s[...] = a
    u_s[...] = jnp.sqrt(-jnp.tanh(log_a) * (a * a + 1.0)) * (gate_x * xc)

    def step(t, hcur):
        hcur = a_s[pl.ds(t, 1), :] * hcur + u_s[pl.ds(t, 1), :]
        u_s[pl.ds(t, 1), :] = hcur
        return hcur

    hstate[...] = lax.fori_loop(0, TM, step, hstate[...], unroll=8)

    @pl.when(g == 0)
    def _():
        mstate[...] = u_s[N_META - 1:N_META, :]
        mhist[...] = xbuf[N_META:N_META + 8, :]

    xbuf[0:8, :] = xbuf[TM:TM + 8, :]
    o_ref[...] = x + jnp.dot((u_s[...] * y_br).astype(BF16), wout_ref[...], preferred_element_type=F32)


def _rglru(h, g, w_in, conv_w, conv_b, gate_w, gate_b, lru_lambda, w_out, tiles_per_seq):
    n = h.shape[0]
    nt = n // TM
    tile = lambda s: (jnp.where(s == 0, nt - 1, s - 1), 0)
    const2 = lambda s: (0, 0)
    return pl.pallas_call(
        functools.partial(_rglru_kernel, tiles_per_seq=tiles_per_seq),
        grid=(nt,),
        in_specs=[
            pl.BlockSpec((TM, D), tile),
            _resident((1, D), const2),
            _resident((D, 2 * D), const2),
            _resident((CONV_W, D), const2),
            _resident((1, D), const2),
            _resident((2, LRU_BLOCKS, LRU_BLOCK_W, LRU_BLOCK_W), lambda s: (0, 0, 0, 0)),
            _resident((2, D), const2),
            _resident((1, D), const2),
            _resident((D, D), const2),
        ],
        out_specs=pl.BlockSpec((TM, D), tile),
        out_shape=jax.ShapeDtypeStruct((n, D), F32),
        scratch_shapes=[
            pltpu.VMEM((TM + 8, D), F32),
            pltpu.VMEM((TM, D), F32),
            pltpu.VMEM((TM, D), F32),
            pltpu.VMEM((1, D), F32),
            pltpu.VMEM((1, D), F32),
            pltpu.VMEM((8, D), F32),
        ],
        compiler_params=_params(),
        name="rglru",
    )(h, g.reshape(1, D), w_in, conv_w, conv_b.reshape(1, D), gate_w, gate_b, lru_lambda.reshape(1, D), w_out)


def _split_attn_w_in(w):
    a = HEAD_W
    o_ki = 4 * a
    o_wi = o_ki + IDX_DIM
    o_qb = o_wi + IDX_HEADS
    qa, ka, va, qi = w[:, 0:a], w[:, a:2 * a], w[:, 2 * a:3 * a], w[:, 3 * a:4 * a]
    ki, wi = w[:, o_ki:o_wi], w[:, o_wi:o_qb]
    qb, kb, vb = w[:, o_qb:o_qb + a], w[:, o_qb + a:o_qb + 2 * a], w[:, o_qb + 2 * a:o_qb + 3 * a]
    w_main = jnp.concatenate([qa, ka, qi, qb, kb, va, vb], axis=1).astype(BF16)
    w_wt = jnp.concatenate([wi.T, jnp.zeros((16 - IDX_HEADS, D), w.dtype)], axis=0).astype(BF16)
    return w_main, ki.astype(BF16), w_wt


def kernel(x, meta_tokens, norm_g, ffn_w_gu, ffn_w_down, attn_w_in, idx_k_ln_g, idx_k_ln_b, diff_lambda, diff_subln_g, attn_w_out, rec_w_in, rec_conv_w, rec_conv_b, rec_gate_w, rec_gate_b, rec_lambda, rec_w_out, final_norm_g):
    batch, seq, _ = x.shape
    depth = norm_g.shape[0]
    n_real = batch * seq
    tail = jnp.concatenate([meta_tokens.astype(x.dtype), jnp.zeros((TM - N_META, D), x.dtype)], axis=0)
    w_gu = ffn_w_gu.astype(BF16)
    w_down = ffn_w_down.astype(BF16)
    h = x.reshape(n_real, D)
    for i in range(depth):
        j = i // 2
        h = _ffn(h, norm_g[i, 0], w_gu, w_down, i, 0, tail=tail if i == 0 else None)
        if i % 2 == 0:
            lam_init = 0.8 - 0.6 * math.exp(-0.3 * i)
            w_main, w_ki, w_wt = _split_attn_w_in(attn_w_in[j])
            proj, ki, wit, vat, vbt = _attn_proj(h, norm_g[i, 1], w_main, w_ki, w_wt, idx_k_ln_g[j], idx_k_ln_b[j])
            oa = _dsa(proj, ki, wit, vat, batch, seq)
            ob = _diff(proj, vbt, diff_lambda[j], diff_subln_g[j], batch, seq, lam_init)
            h = _out_proj(h, oa, ob, attn_w_out[j].astype(BF16))
        else:
            h = _rglru(h, norm_g[i, 1], rec_w_in[j].astype(BF16), rec_conv_w[j], rec_conv_b[j],
                       rec_gate_w[j].astype(BF16), rec_gate_b[j], rec_lambda[j], rec_w_out[j].astype(BF16),
                       seq // TM)
        if i == depth - 1:
            h = _ffn(h, norm_g[i, 2], w_gu, w_down, i, 1, final_g=final_norm_g, n_rows=n_real)
        else:
            h = _ffn(h, norm_g[i, 2], w_gu, w_down, i, 1)
    return h.reshape(batch, seq, D)
```

```python
import functools
import math

import jax
import jax.numpy as jnp
from jax import lax
from jax.experimental import pallas as pl
from jax.experimental.pallas import tpu as pltpu

F32 = jnp.float32
BF16 = jnp.bfloat16
I32 = jnp.int32

D = 1024
N_META = 16
D_FF = 2816
EPS = 1e-6
HEAD_W = 512
IDX_DIM = 64
IDX_HEADS = 8
TOPK_MAX = 256
LRU_BLOCKS = 4
LRU_BLOCK_W = D // LRU_BLOCKS
CONV_W = 4
RG_C = 8.0

TM = 512
TQ = 128
TK = 512
MXU_TILE = 256
FF_CHUNKS = ((0, 6 * MXU_TILE), (6 * MXU_TILE, D_FF))
NEG = -1e30
BIG = 1e30
QK_SCALE = IDX_DIM ** -0.5 * math.log2(math.e)
INT_MIN = -2 ** 31
VMEM_LIMIT = 56 * 1024 * 1024

_NT = (((1,), (1,)), ((), ()))


def _params():
    return pltpu.CompilerParams(dimension_semantics=("arbitrary",), vmem_limit_bytes=VMEM_LIMIT)


def _resident(shape, index_map):
    return pl.BlockSpec(shape, index_map, pipeline_mode=pl.Buffered(1))


def _rms(x, g):
    return x * lax.rsqrt(jnp.mean(x * x, axis=-1, keepdims=True) + EPS) * g


def _sigmoid(x):
    return 0.5 * jnp.tanh(0.5 * x) + 0.5


def _ffn_kernel(x_ref, *refs, final, n_real_tiles):
    refs = list(refs)
    o_ref = refs.pop()
    tail_ref = refs.pop(0) if n_real_tiles is not None else None
    g_ref, wg_ref, wu_ref, wd_ref = refs[:4]
    x = x_ref[...]
    if tail_ref is not None:
        x = jnp.where(pl.program_id(0) < n_real_tiles, x, tail_ref[...])
    xn = _rms(x, g_ref[...]).astype(BF16)
    acc = None
    for lo, hi in FF_CHUNKS:
        sl = slice(lo, hi)
        g = jnp.dot(xn, wg_ref[:, sl], preferred_element_type=F32)
        u = jnp.dot(xn, wu_ref[:, sl], preferred_element_type=F32)
        a = (g * _sigmoid(g) * u).astype(BF16)
        part = jnp.dot(a, wd_ref[sl, :], preferred_element_type=F32)
        acc = part if acc is None else acc + part
    y = x + 0.5 * acc
    if final:
        y = _rms(y, refs[4][...])
    o_ref[...] = y


def _ffn(h, g, w_gu, w_down, layer, half, tail=None, final_g=None, n_rows=None):
    n_real_tiles = None if tail is None else h.shape[0] // TM
    if n_rows is None:
        n_rows = h.shape[0] + (0 if tail is None else TM)
    final = final_g is not None
    if tail is None:
        in_specs, args = [pl.BlockSpec((TM, D), lambda i: (i, 0))], [h]
    else:
        in_specs = [pl.BlockSpec((TM, D), lambda i: (jnp.minimum(i, n_real_tiles - 1), 0)),
                    _resident((TM, D), lambda i: (0, 0))]
        args = [h, tail]
    in_specs += [
        _resident((1, D), lambda i: (0, 0)),
        _resident((None, None, D, D_FF), lambda i: (layer, half, 0, 0)),
        _resident((None, None, D, D_FF), lambda i: (layer, half, 0, 1)),
        _resident((None, None, D_FF, D), lambda i: (layer, half, 0, 0)),
    ]
    args += [g.reshape(1, D), w_gu, w_gu, w_down]
    if final:
        in_specs.append(_resident((1, D), lambda i: (0, 0)))
        args.append(final_g.reshape(1, D))
    return pl.pallas_call(
        functools.partial(_ffn_kernel, final=final, n_real_tiles=n_real_tiles),
        grid=(n_rows // TM,),
        in_specs=in_specs,
        out_specs=pl.BlockSpec((TM, D), lambda i: (i, 0)),
        out_shape=jax.ShapeDtypeStruct((n_rows, D), F32),
        compiler_params=_params(),
        name="ffn",
    )(*args)


N_PROJ = 5 * HEAD_W


def _attn_proj_kernel(x_ref, g_ref, w_ref, wki_ref, wwt_ref, lng_ref, lnb_ref,
                      proj_ref, ki_ref, wit_ref, vat_ref, vbt_ref):
    xn = _rms(x_ref[...], g_ref[...]).astype(BF16)
    y = jnp.dot(xn, w_ref[...], preferred_element_type=F32)
    scale = QK_SCALE
    proj_ref[:, 0:HEAD_W] = (y[:, 0:HEAD_W] * scale).astype(BF16)
    proj_ref[:, HEAD_W:3 * HEAD_W] = y[:, HEAD_W:3 * HEAD_W].astype(BF16)
    proj_ref[:, 3 * HEAD_W:4 * HEAD_W] = (y[:, 3 * HEAD_W:4 * HEAD_W] * scale).astype(BF16)
    proj_ref[:, 4 * HEAD_W:5 * HEAD_W] = y[:, 4 * HEAD_W:5 * HEAD_W].astype(BF16)
    vat_ref[0] = y[:, 5 * HEAD_W:6 * HEAD_W].T.astype(BF16)
    vbt_ref[0] = y[:, 6 * HEAD_W:7 * HEAD_W].T.astype(BF16)
    yk = jnp.dot(xn, wki_ref[...], preferred_element_type=F32)
    mu = jnp.mean(yk, axis=-1, keepdims=True)
    yc = yk - mu
    ln = yc * lax.rsqrt(jnp.mean(yc * yc, axis=-1, keepdims=True) + EPS)
    ki_ref[...] = (ln * lng_ref[...] + lnb_ref[...]).astype(BF16)
    wit = lax.dot_general(wwt_ref[...], xn, _NT, preferred_element_type=F32)
    wit_ref[...] = wit * (IDX_HEADS ** -0.5 * IDX_DIM ** -0.5)


def _attn_proj(h, g, w_main, w_ki, w_wt, ln_g, ln_b):
    n = h.shape[0]
    nt = n // TM
    const = lambda i: (0, 0)
    return pl.pallas_call(
        _attn_proj_kernel,
        grid=(nt,),
        in_specs=[
            pl.BlockSpec((TM, D), lambda i: (i, 0)),
            _resident((1, D), const),
            _resident((D, 7 * HEAD_W), const),
            _resident((D, IDX_DIM), const),
            _resident((16, D), const),
            _resident((1, IDX_DIM), const),
            _resident((1, IDX_DIM), const),
        ],
        out_specs=[
            pl.BlockSpec((TM, N_PROJ), lambda i: (i, 0)),
            pl.BlockSpec((TM, IDX_DIM), lambda i: (i, 0)),
            pl.BlockSpec((16, TM), lambda i: (0, i)),
            pl.BlockSpec((1, HEAD_W, TM), lambda i: (i, 0, 0)),
            pl.BlockSpec((1, HEAD_W, TM), lambda i: (i, 0, 0)),
        ],
        out_shape=[
            jax.ShapeDtypeStruct((n, N_PROJ), BF16),
            jax.ShapeDtypeStruct((n, IDX_DIM), BF16),
            jax.ShapeDtypeStruct((16, n), F32),
            jax.ShapeDtypeStruct((nt, HEAD_W, TM), BF16),
            jax.ShapeDtypeStruct((nt, HEAD_W, TM), BF16),
        ],
        compiler_params=_params(),
        name="attn_proj",
    )(h, g.reshape(1, D), w_main, w_ki, w_wt, ln_g.reshape(1, IDX_DIM), ln_b.reshape(1, IDX_DIM))


def _step_geometry(g, seq):
    nq_seq = seq // TQ
    n_real = pl.num_programs(0) - TM // TQ
    is_real = g < n_real
    i = g % nq_seq
    n_ch = jnp.where(is_real, (i * TQ + TQ + TK - 1) // TK, 0)
    return is_real, i, n_ch


def _stack_pair(q_pair):
    lane = lax.broadcasted_iota(I32, (TQ, 128), 1)
    qf = q_pair.astype(F32)
    top = jnp.where(lane < 64, qf, 0.0).astype(BF16)
    bot = jnp.where(lane >= 64, qf, 0.0).astype(BF16)
    return jnp.concatenate([top, bot], axis=0)


def _meta_bias(is_real):
    j = lax.broadcasted_iota(I32, (128, 2 * TQ), 0)
    t = lax.broadcasted_iota(I32, (128, 2 * TQ), 1) % TQ
    real = is_real.astype(I32)
    lim = jnp.minimum(real * N_META + (1 - real) * (t + 1), N_META)
    return jnp.where(j < lim, BIG, NEG)


def _bcast_halves(row, n_sub):
    return jnp.concatenate([jnp.broadcast_to(row[:, :TQ], (n_sub, TQ)),
                            jnp.broadcast_to(row[:, TQ:], (n_sub, TQ))], axis=0)


SCORE_LOOKAHEAD = 2
MAX_LAG = 64.0


def _softmax_pv(segments, v_tiles, base):
    issued, out = {}, []
    for h in range(SCORE_LOOKAHEAD):
        issued[h] = [fn() for fn in segments[h]]
    for h in range(4):
        if h + SCORE_LOOKAHEAD < 4:
            issued[h + SCORE_LOOKAHEAD] = [fn() for fn in segments[h + SCORE_LOOKAHEAD]]
        sts = issued.pop(h)
        m_loc = jnp.max(sts[0], axis=0, keepdims=True)
        for st in sts[1:]:
            m_loc = jnp.maximum(m_loc, jnp.max(st, axis=0, keepdims=True))
        m_use = base(h, m_loc)
        l_new, pv = None, None
        for st, vt in zip(sts, v_tiles[h]):
            pc = jnp.exp2(st - m_use)
            s = jnp.sum(pc, axis=0, keepdims=True)
            d = jnp.dot(vt(), pc.astype(BF16), preferred_element_type=F32)
            l_new = s if l_new is None else l_new + s
            pv = d if pv is None else pv + d
        out.append((m_loc, l_new, pv))
    return out


def _attend(segments, v_tiles, m_ref, l_ref, acc_ref, fold, first):
    fold_pv, fold_row = fold
    if first:
        for h, (m_loc, l_new, pv) in enumerate(_softmax_pv(segments, v_tiles, lambda h, m_loc: m_loc)):
            m_ref[h] = m_loc
            l_ref[h] = l_new
            acc_ref[h] = fold_pv(pv)
        return

    def commit(parts, rebased):
        for h, (m_loc, l_new, pv) in enumerate(parts):
            m_new = jnp.maximum(m_ref[h], m_loc)
            alpha = jnp.exp2(m_ref[h] - m_new)
            if rebased:
                l_ref[h] = alpha * l_ref[h] + l_new
                acc_ref[h] = acc_ref[h] * fold_row(alpha) + fold_pv(pv)
            else:
                l_ref[h] = alpha * (l_ref[h] + l_new)
                acc_ref[h] = (acc_ref[h] + fold_pv(pv)) * fold_row(alpha)
            m_ref[h] = m_new

    parts = _softmax_pv(segments, v_tiles, lambda h, m_loc: m_ref[h])
    rise = parts[0][0] - m_ref[0]
    for h in range(1, 4):
        rise = jnp.maximum(rise, parts[h][0] - m_ref[h])
    safe = jnp.max(rise) <= MAX_LAG

    @pl.when(safe)
    def _():
        commit(parts, rebased=False)

    @pl.when(jnp.logical_not(safe))
    def _():
        commit(_softmax_pv(segments, v_tiles, lambda h, m_loc: jnp.maximum(m_ref[h], m_loc)), rebased=True)


GROUP_KEYS = 256
GROUPS_PER_CHUNK = TK // GROUP_KEYS


def _bit_transpose32(words):
    a = list(words)
    j, mask = 16, 0x0000FFFF
    while j:
        k = 0
        while k < 32:
            t = (a[k] ^ lax.shift_right_logical(a[k + j], jnp.int32(j))) & mask
            a[k] = a[k] ^ t
            a[k + j] = a[k + j] ^ lax.shift_left(t, jnp.int32(j))
            k = (k + j + 1) & ~j
        j >>= 1
        mask ^= mask << j
    return a


def _dsa_kernel(qa_ref, qi_ref, wit_ref, ki_ref, ka_ref, vat_ref, km_ref, vmt_ref, o_ref,
                qall_ref, qstk_ref, key_ref, plane_ref, bias_ref, tri_ref, m_ref, l_ref, acc_ref, *, seq, top_k):
    g = pl.program_id(0)

    @pl.when(g == 0)
    def _():
        plane_ref[...] = jnp.zeros(plane_ref.shape, I32)
        r = lax.broadcasted_iota(I32, (TK, TK), 0)
        c_ = lax.broadcasted_iota(I32, (TK, TK), 1)
        tri_ref[...] = jnp.where(c_ <= r, 1.0, 0.0).astype(BF16)

    is_real, i, n_ch = _step_geometry(g, seq)
    q_pos = i * TQ + lax.broadcasted_iota(I32, (TK, TQ), 1)
    k_off = lax.broadcasted_iota(I32, (TK, TQ), 0)

    qi = qi_ref[...]
    for h in range(IDX_HEADS):
        qall_ref[h * TQ:(h + 1) * TQ, :] = qi[:, h * IDX_DIM:(h + 1) * IDX_DIM]
    for p in range(4):
        qstk_ref[p] = _stack_pair(qa_ref[:, p * 128:(p + 1) * 128])

    wt = wit_ref[...]

    def score_body(c, carry):
        kc = ki_ref[pl.ds(pl.multiple_of(c * TK, TK), TK), :]
        logits = lax.dot_general(kc, qall_ref[...], _NT, preferred_element_type=F32)
        sc = jnp.zeros((TK, TQ), F32)
        for h in range(IDX_HEADS):
            sc = sc + wt[h:h + 1, :] * jnp.maximum(logits[:, h * TQ:(h + 1) * TQ], 0.0)
        sc = jnp.where(sc == 0.0, 0.0, sc)
        bits = pltpu.bitcast(sc, I32)
        key = bits ^ (lax.shift_right_arithmetic(bits, 31) & 0x7FFFFFFF)
        key = jnp.where(c * TK + k_off <= q_pos, key, INT_MIN)
        key_ref[c] = key
        u = key ^ INT_MIN
        for gi in range(GROUPS_PER_CHUNK):
            base = gi * GROUP_KEYS
            planes = _bit_transpose32([u[base + w * 8:base + (w + 1) * 8, :] for w in range(32)])
            for b in range(32):
                plane_ref[c * GROUPS_PER_CHUNK + gi, b] = planes[b]
        return carry

    lax.fori_loop(0, n_ch, score_body, 0)

    n_grp = n_ch * GROUPS_PER_CHUNK
    n_groups = plane_ref.shape[0]
    alive0 = tuple(jnp.full((8, TQ), -1, I32) * (gq < n_grp).astype(I32) for gq in range(n_groups))

    def bit_body(b, carry):
        alive, k_rem, thr_u = carry
        ones = [a & plane_ref[gq, b] for gq, a in enumerate(alive)]
        cnt8 = lax.population_count(ones[0])
        for o in ones[1:]:
            cnt8 = cnt8 + lax.population_count(o)
        cnt = jnp.sum(cnt8, axis=0, keepdims=True)
        take = cnt >= k_rem
        alive = tuple(jnp.where(take, o, a ^ o) for a, o in zip(alive, ones))
        k_rem = jnp.where(take, k_rem, k_rem - cnt)
        thr_u = thr_u | jnp.where(take, lax.shift_left(jnp.int32(1), 31 - b), 0)
        return alive, k_rem, thr_u

    alive, k_rem, thr_u = lax.fori_loop(
        0, 32, bit_body, (alive0, jnp.full((1, TQ), top_k, I32), jnp.zeros((1, TQ), I32)))
    thr = thr_u ^ INT_MIN
    eq8 = lax.population_count(alive[0])
    for a in alive[1:]:
        eq8 = eq8 + lax.population_count(a)
    cnt_eq = jnp.sum(eq8, axis=0, keepdims=True)
    excess = jnp.where(thr_u != 0, cnt_eq - k_rem, 0)
    has_excess = jnp.max(excess) > 0

    @pl.when(jnp.logical_not(has_excess))
    def _():
        thr_eff = jnp.maximum(thr, INT_MIN + 1)

        def cap_body(c, carry):
            bias_ref[c] = jnp.where(key_ref[c] >= thr_eff, BIG, NEG)
            return carry

        lax.fori_loop(0, n_ch, cap_body, 0)

    @pl.when(has_excess)
    def _():
        need = k_rem.astype(F32)

        def cap_body(c, run):
            k = key_ref[c]
            eq = k == thr
            pre = jnp.dot(tri_ref[...], jnp.where(eq, 1.0, 0.0).astype(BF16), preferred_element_type=F32)
            rank = run + pre
            b = jnp.where(k > thr, BIG, jnp.where(eq, jnp.where(rank <= need, BIG, NEG), NEG))
            bias_ref[c] = jnp.where(k == INT_MIN, NEG, b)
            return run + pre[TK - 1:TK, :]

        lax.fori_loop(0, n_ch, cap_body, jnp.zeros((1, TQ), F32))

    fold = (lambda ot: jnp.concatenate([ot[:64, :TQ], ot[64:, TQ:]], axis=0), lambda row: _bcast_halves(row, 64))
    pairs = [slice(p * 128, (p + 1) * 128) for p in range(4)]

    def score(p, k_ref, idx, cap):
        return lambda: jnp.minimum(lax.dot_general(k_ref[idx], qstk_ref[p], _NT, preferred_element_type=F32), cap)

    def tile(ref, idx):
        return lambda: ref[idx]

    c_last = jnp.maximum(n_ch - 1, 0)
    row_last = pl.ds(pl.multiple_of(c_last * TK, TK), TK)
    mcap = _meta_bias(is_real)
    cap = jnp.where(is_real, bias_ref[c_last], NEG)
    cap2 = jnp.concatenate([cap, cap], axis=1)
    _attend([[score(p, km_ref, (slice(None), ps), mcap), score(p, ka_ref, (row_last, ps), cap2)]
             for p, ps in enumerate(pairs)],
            [[tile(vmt_ref, (0, ps, slice(0, 128))), tile(vat_ref, (c_last, ps, slice(None)))] for ps in pairs],
            m_ref, l_ref, acc_ref, fold, first=True)

    def update(chunks):
        rows = [pl.ds(pl.multiple_of(c * TK, TK), TK) for c in chunks]
        caps = [jnp.concatenate([bias_ref[c]] * 2, axis=1) for c in chunks]
        _attend([[score(p, ka_ref, (row, ps), cap2) for row, cap2 in zip(rows, caps)] for p, ps in enumerate(pairs)],
                [[tile(vat_ref, (c, ps, slice(None))) for c in chunks] for ps in pairs],
                m_ref, l_ref, acc_ref, fold, first=False)

    def pair_body(j, carry):
        update([2 * j, 2 * j + 1])
        return carry

    lax.fori_loop(0, c_last // 2, pair_body, 0)

    @pl.when(c_last % 2 == 1)
    def _():
        update([c_last - 1])

    for p, ps in enumerate(pairs):
        o_ref[:, ps] = (acc_ref[p] * _bcast_halves(1.0 / l_ref[p], 64)).T.astype(BF16)


def _dsa(proj, ki, wit, vat, batch, seq):
    n = proj.shape[0]
    nq_seq = seq // TQ
    n_real_q = batch * nq_seq
    nch = seq // TK
    top_k = min(TOPK_MAX, seq // 4)
    bidx = lambda g: jnp.minimum(g // nq_seq, batch - 1)
    meta_q = n_real_q
    meta_tile = batch * nch
    return pl.pallas_call(
        functools.partial(_dsa_kernel, seq=seq, top_k=top_k),
        grid=(n // TQ,),
        in_specs=[
            pl.BlockSpec((TQ, HEAD_W), lambda g: (g, 0)),
            pl.BlockSpec((TQ, HEAD_W), lambda g: (g, 2)),
            pl.BlockSpec((16, TQ), lambda g: (0, g)),
            pl.BlockSpec((seq, IDX_DIM), lambda g: (bidx(g), 0)),
            pl.BlockSpec((seq, HEAD_W), lambda g: (bidx(g), 1)),
            pl.BlockSpec((nch, HEAD_W, TK), lambda g: (bidx(g), 0, 0)),
            pl.BlockSpec((TQ, HEAD_W), lambda g: (meta_q, 1)),
            pl.BlockSpec((1, HEAD_W, TK), lambda g: (meta_tile, 0, 0)),
        ],
        out_specs=pl.BlockSpec((TQ, HEAD_W), lambda g: (g, 0)),
        out_shape=jax.ShapeDtypeStruct((n, HEAD_W), BF16),
        scratch_shapes=[
            pltpu.VMEM((IDX_HEADS * TQ, IDX_DIM), BF16),
            pltpu.VMEM((4, 2 * TQ, 128), BF16),
            pltpu.VMEM((nch, TK, TQ), I32),
            pltpu.VMEM((nch * GROUPS_PER_CHUNK, 32, 8, TQ), I32),
            pltpu.VMEM((nch, TK, TQ), F32),
            pltpu.VMEM((TK, TK), BF16),
            pltpu.VMEM((4, 1, 2 * TQ), F32),
            pltpu.VMEM((4, 1, 2 * TQ), F32),
            pltpu.VMEM((4, 128, TQ), F32),
        ],
        compiler_params=_params(),
        name="dsa",
    )(proj, proj, wit, ki, proj, vat, proj, vat)


def _diff_kernel(lam_ref, g_ref, qb_ref, kb_ref, vbt_ref, km_ref, vmt_ref, o_ref,
                 qstk_ref, m_ref, l_ref, acc_ref, *, seq, lam_init):
    g = pl.program_id(0)
    is_real, i, n_ch = _step_geometry(g, seq)
    lp = lam_ref[...]
    lam = (jnp.exp(jnp.sum(lp[0:1] * lp[1:2], axis=1, keepdims=True))
           - jnp.exp(jnp.sum(lp[2:3] * lp[3:4], axis=1, keepdims=True)) + lam_init)
    heads = [slice(h * 128, (h + 1) * 128) for h in range(4)]
    for h, hs in enumerate(heads):
        qstk_ref[h] = _stack_pair(qb_ref[:, hs])

    fold = (lambda ot: ot, lambda row: row)

    def score(h, k_ref, idx, cap=None):
        def fn():
            st = lax.dot_general(k_ref[idx], qstk_ref[h], _NT, preferred_element_type=F32)
            return st if cap is None else jnp.minimum(st, cap)
        return fn

    def tile(ref, idx):
        return lambda: ref[idx]

    n_full = n_ch - is_real.astype(I32)
    row_diag = pl.ds(pl.multiple_of(n_full * TK, TK), TK)
    mcap = _meta_bias(is_real)
    q_pos = i * TQ + lax.broadcasted_iota(I32, (TK, 2 * TQ), 1) % TQ
    k_pos = n_full * TK + lax.broadcasted_iota(I32, (TK, 2 * TQ), 0)
    cap = jnp.where(jnp.where(k_pos <= q_pos, is_real.astype(I32), 0) > 0, BIG, NEG)
    _attend([[score(h, km_ref, (slice(None), hs), mcap), score(h, kb_ref, (row_diag, hs), cap)]
             for h, hs in enumerate(heads)],
            [[tile(vmt_ref, (0, hs, slice(0, 128))), tile(vbt_ref, (n_full, hs, slice(None)))] for hs in heads],
            m_ref, l_ref, acc_ref, fold, first=True)

    def update(chunks):
        rows = [pl.ds(pl.multiple_of(c * TK, TK), TK) for c in chunks]
        _attend([[score(h, kb_ref, (row, hs)) for row in rows] for h, hs in enumerate(heads)],
                [[tile(vbt_ref, (c, hs, slice(None))) for c in chunks] for hs in heads],
                m_ref, l_ref, acc_ref, fold, first=False)

    def pair_body(j, carry):
        update([2 * j, 2 * j + 1])
        return carry

    lax.fori_loop(0, n_full // 2, pair_body, 0)

    @pl.when(n_full % 2 == 1)
    def _():
        update([n_full - 1])

    for h, hs in enumerate(heads):
        on = acc_ref[h] * (1.0 / l_ref[h])
        o = on[:, :TQ] - lam * on[:, TQ:]
        y = o * lax.rsqrt(jnp.mean(o * o, axis=0, keepdims=True) + EPS) * g_ref[...] * (1.0 - lam_init)
        o_ref[:, hs] = y.T.astype(BF16)


def _diff(proj, vbt, lam_params, subln_g, batch, seq, lam_init):
    n = proj.shape[0]
    nq_seq = seq // TQ
    nch = seq // TK
    bidx = lambda g: jnp.minimum(g // nq_seq, batch - 1)
    meta_q = batch * nq_seq
    meta_tile = batch * nch
    return pl.pallas_call(
        functools.partial(_diff_kernel, seq=seq, lam_init=lam_init),
        grid=(n // TQ,),
        in_specs=[
            _resident((4, 64), lambda g: (0, 0)),
            _resident((128, 1), lambda g: (0, 0)),
            pl.BlockSpec((TQ, HEAD_W), lambda g: (g, 3)),
            pl.BlockSpec((seq, HEAD_W), lambda g: (bidx(g), 4)),
            pl.BlockSpec((nch, HEAD_W, TK), lambda g: (bidx(g), 0, 0)),
            pl.BlockSpec((TQ, HEAD_W), lambda g: (meta_q, 4)),
            pl.BlockSpec((1, HEAD_W, TK), lambda g: (meta_tile, 0, 0)),
        ],
        out_specs=pl.BlockSpec((TQ, HEAD_W), lambda g: (g, 0)),
        out_shape=jax.ShapeDtypeStruct((n, HEAD_W), BF16),
        scratch_shapes=[
            pltpu.VMEM((4, 2 * TQ, 128), BF16),
            pltpu.VMEM((4, 1, 2 * TQ), F32),
            pltpu.VMEM((4, 1, 2 * TQ), F32),
            pltpu.VMEM((4, 128, 2 * TQ), F32),
        ],
        compiler_params=_params(),
        name="diff_attn",
    )(lam_params, subln_g.reshape(128, 1), proj, proj, vbt, proj, vbt)


def _out_proj_kernel(h_ref, oa_ref, ob_ref, wa_ref, wb_ref, o_ref):
    o_ref[...] = (h_ref[...]
                  + jnp.dot(oa_ref[...], wa_ref[...], preferred_element_type=F32)
                  + jnp.dot(ob_ref[...], wb_ref[...], preferred_element_type=F32))


def _out_proj(h, oa, ob, w_out):
    n = h.shape[0]
    return pl.pallas_call(
        _out_proj_kernel,
        grid=(n // TM,),
        in_specs=[
            pl.BlockSpec((TM, D), lambda i: (i, 0)),
            pl.BlockSpec((TM, HEAD_W), lambda i: (i, 0)),
            pl.BlockSpec((TM, HEAD_W), lambda i: (i, 0)),
            _resident((HEAD_W, D), lambda i: (0, 0)),
            _resident((HEAD_W, D), lambda i: (1, 0)),
        ],
        out_specs=pl.BlockSpec((TM, D), lambda i: (i, 0)),
        out_shape=jax.ShapeDtypeStruct((n, D), F32),
        compiler_params=_params(),
        name="attn_out_proj",
    )(h, oa, ob, w_out, w_out)


def _rglru_kernel(x_ref, g_ref, win_ref, cw_ref, cb_ref, gw_ref, gb_ref, lam_ref, wout_ref, o_ref,
                  xbuf, a_s, u_s, hstate, mstate, mhist, *, tiles_per_seq):
    g = pl.program_id(0)
    x = x_ref[...]
    xn = _rms(x, g_ref[...]).astype(BF16)
    yx = jnp.dot(xn, win_ref[...], preferred_element_type=F32)
    yb = yx[:, :D]
    y_br = 0.5 * yb * (1.0 + jnp.tanh(math.sqrt(2.0 / math.pi) * (yb + 0.044715 * (yb * yb * yb))))

    @pl.when(g == 0)
    def _():
        xbuf[0:8, :] = jnp.zeros((8, D), F32)
        hstate[...] = jnp.zeros(hstate.shape, F32)

    @pl.when(jnp.logical_and(g >= 1, (g - 1) % tiles_per_seq == 0))
    def _():
        xbuf[0:8, :] = mhist[...]
        hstate[...] = mstate[...]

    xbuf[8:, :] = yx[:, D:]
    xc = cb_ref[...] + cw_ref[0:1, :] * xbuf[5:5 + TM, :]
    for j in range(1, CONV_W):
        xc = xc + cw_ref[j:j + 1, :] * xbuf[5 + j:5 + j + TM, :]

    gx, ga = [], []
    for nb in range(LRU_BLOCKS):
        xb = xc[:, nb * LRU_BLOCK_W:(nb + 1) * LRU_BLOCK_W].astype(BF16)
        gx.append(jnp.dot(xb, gw_ref[0, nb], preferred_element_type=F32))
        ga.append(jnp.dot(xb, gw_ref[1, nb], preferred_element_type=F32))
    gate_x = _sigmoid(jnp.concatenate(gx, axis=1) + gb_ref[0:1, :])
    gate_a = _sigmoid(jnp.concatenate(ga, axis=1) + gb_ref[1:2, :])
    lam = lam_ref[...]
    log_sig = jnp.minimum(lam, 0.0) - jnp.log(1.0 + jnp.exp(-jnp.abs(lam)))
    log_a = RG_C * gate_a * log_sig
    a = jnp.exp(log_a)
    a_s[...] = a
    u_s[...] = jnp.sqrt(-jnp.tanh(log_a) * (a * a + 1.0)) * (gate_x * xc)

    def step(t, hcur):
        hcur = a_s[pl.ds(t, 1), :] * hcur + u_s[pl.ds(t, 1), :]
        u_s[pl.ds(t, 1), :] = hcur
        return hcur

    hstate[...] = lax.fori_loop(0, TM, step, hstate[...], unroll=8)

    @pl.when(g == 0)
    def _():
        mstate[...] = u_s[N_META - 1:N_META, :]
        mhist[...] = xbuf[N_META:N_META + 8, :]

    xbuf[0:8, :] = xbuf[TM:TM + 8, :]
    o_ref[...] = x + jnp.dot((u_s[...] * y_br).astype(BF16), wout_ref[...], preferred_element_type=F32)


def _rglru(h, g, w_in, conv_w, conv_b, gate_w, gate_b, lru_lambda, w_out, tiles_per_seq):
    n = h.shape[0]
    nt = n // TM
    tile = lambda s: (jnp.where(s == 0, nt - 1, s - 1), 0)
    const2 = lambda s: (0, 0)
    return pl.pallas_call(
        functools.partial(_rglru_kernel, tiles_per_seq=tiles_per_seq),
        grid=(nt,),
        in_specs=[
            pl.BlockSpec((TM, D), tile),
            _resident((1, D), const2),
            _resident((D, 2 * D), const2),
            _resident((CONV_W, D), const2),
            _resident((1, D), const2),
            _resident((2, LRU_BLOCKS, LRU_BLOCK_W, LRU_BLOCK_W), lambda s: (0, 0, 0, 0)),
            _resident((2, D), const2),
            _resident((1, D), const2),
            _resident((D, D), const2),
        ],
        out_specs=pl.BlockSpec((TM, D), tile),
        out_shape=jax.ShapeDtypeStruct((n, D), F32),
        scratch_shapes=[
            pltpu.VMEM((TM + 8, D), F32),
            pltpu.VMEM((TM, D), F32),
            pltpu.VMEM((TM, D), F32),
            pltpu.VMEM((1, D), F32),
            pltpu.VMEM((1, D), F32),
            pltpu.VMEM((8, D), F32),
        ],
        compiler_params=_params(),
        name="rglru",
    )(h, g.reshape(1, D), w_in, conv_w, conv_b.reshape(1, D), gate_w, gate_b, lru_lambda.reshape(1, D), w_out)


def _split_attn_w_in(w):
    a = HEAD_W
    o_ki = 4 * a
    o_wi = o_ki + IDX_DIM
    o_qb = o_wi + IDX_HEADS
    qa, ka, va, qi = w[:, 0:a], w[:, a:2 * a], w[:, 2 * a:3 * a], w[:, 3 * a:4 * a]
    ki, wi = w[:, o_ki:o_wi], w[:, o_wi:o_qb]
    qb, kb, vb = w[:, o_qb:o_qb + a], w[:, o_qb + a:o_qb + 2 * a], w[:, o_qb + 2 * a:o_qb + 3 * a]
    w_main = jnp.concatenate([qa, ka, qi, qb, kb, va, vb], axis=1).astype(BF16)
    w_wt = jnp.concatenate([wi.T, jnp.zeros((16 - IDX_HEADS, D), w.dtype)], axis=0).astype(BF16)
    return w_main, ki.astype(BF16), w_wt


def kernel(x, meta_tokens, norm_g, ffn_w_gu, ffn_w_down, attn_w_in, idx_k_ln_g, idx_k_ln_b, diff_lambda, diff_subln_g, attn_w_out, rec_w_in, rec_conv_w, rec_conv_b, rec_gate_w, rec_gate_b, rec_lambda, rec_w_out, final_norm_g):
    batch, seq, _ = x.shape
    depth = norm_g.shape[0]
    n_real = batch * seq
    tail = jnp.concatenate([meta_tokens.astype(x.dtype), jnp.zeros((TM - N_META, D), x.dtype)], axis=0)
    w_gu = ffn_w_gu.astype(BF16)
    w_down = ffn_w_down.astype(BF16)
    h = x.reshape(n_real, D)
    for i in range(depth):
        j = i // 2
        h = _ffn(h, norm_g[i, 0], w_gu, w_down, i, 0, tail=tail if i == 0 else None)
        if i % 2 == 0:
            lam_init = 0.8 - 0.6 * math.exp(-0.3 * i)
            w_main, w_ki, w_wt = _split_attn_w_in(attn_w_in[j])
            proj, ki, wit, vat, vbt = _attn_proj(h, norm_g[i, 1], w_main, w_ki, w_wt, idx_k_ln_g[j], idx_k_ln_b[j])
            oa = _dsa(proj, ki, wit, vat, batch, seq)
            ob = _diff(proj, vbt, diff_lambda[j], diff_subln_g[j], batch, seq, lam_init)
            h = _out_proj(h, oa, ob, attn_w_out[j].astype(BF16))
        else:
            h = _rglru(h, norm_g[i, 1], rec_w_in[j].astype(BF16), rec_conv_w[j], rec_conv_b[j],
                       rec_gate_w[j].astype(BF16), rec_gate_b[j], rec_lambda[j], rec_w_out[j].astype(BF16),
                       seq // TM)
        if i == depth - 1:
            h = _ffn(h, norm_g[i, 2], w_gu, w_down, i, 1, final_g=final_norm_g, n_rows=n_real)
        else:
            h = _ffn(h, norm_g[i, 2], w_gu, w_down, i, 1)
    return h.reshape(batch, seq, D)
```

```python
import functools
import math

import jax
import jax.numpy as jnp
from jax import lax
from jax.experimental import pallas as pl
from jax.experimental.pallas import tpu as pltpu

F32 = jnp.float32
BF16 = jnp.bfloat16
I32 = jnp.int32

D = 1024
N_META = 16
D_FF = 2816
EPS = 1e-6
HEAD_W = 512
IDX_DIM = 64
IDX_HEADS = 8
TOPK_MAX = 256
LRU_BLOCKS = 4
LRU_BLOCK_W = D // LRU_BLOCKS
CONV_W = 4
RG_C = 8.0

TM = 512
TQ = 256
META_KEYS = 128
TK = 512
MXU_TILE = 256
FF_CHUNKS = ((0, 6 * MXU_TILE), (6 * MXU_TILE, D_FF))
NEG = -1e30
BIG = 1e30
QK_SCALE = IDX_DIM ** -0.5 * math.log2(math.e)
INT_MIN = -2 ** 31
VMEM_LIMIT = 56 * 1024 * 1024

_NT = (((1,), (1,)), ((), ()))


def _params():
    return pltpu.CompilerParams(dimension_semantics=("arbitrary",), vmem_limit_bytes=VMEM_LIMIT)


def _resident(shape, index_map):
    return pl.BlockSpec(shape, index_map, pipeline_mode=pl.Buffered(1))


def _rms(x, g):
    return x * lax.rsqrt(jnp.mean(x * x, axis=-1, keepdims=True) + EPS) * g


def _sigmoid(x):
    return 0.5 * jnp.tanh(0.5 * x) + 0.5


def _ffn_kernel(x_ref, *refs, final, n_real_tiles):
    refs = list(refs)
    o_ref = refs.pop()
    tail_ref = refs.pop(0) if n_real_tiles is not None else None
    g_ref, wg_ref, wu_ref, wd_ref = refs[:4]
    x = x_ref[...]
    if tail_ref is not None:
        x = jnp.where(pl.program_id(0) < n_real_tiles, x, tail_ref[...])
    xn = _rms(x, g_ref[...]).astype(BF16)
    acc = None
    for lo, hi in FF_CHUNKS:
        sl = slice(lo, hi)
        g = jnp.dot(xn, wg_ref[:, sl], preferred_element_type=F32)
        u = jnp.dot(xn, wu_ref[:, sl], preferred_element_type=F32)
        a = (g * _sigmoid(g) * u).astype(BF16)
        part = jnp.dot(a, wd_ref[sl, :], preferred_element_type=F32)
        acc = part if acc is None else acc + part
    y = x + 0.5 * acc
    if final:
        y = _rms(y, refs[4][...])
    o_ref[...] = y


def _ffn(h, g, w_gu, w_down, layer, half, tail=None, final_g=None, n_rows=None):
    n_real_tiles = None if tail is None else h.shape[0] // TM
    if n_rows is None:
        n_rows = h.shape[0] + (0 if tail is None else TM)
    final = final_g is not None
    if tail is None:
        in_specs, args = [pl.BlockSpec((TM, D), lambda i: (i, 0))], [h]
    else:
        in_specs = [pl.BlockSpec((TM, D), lambda i: (jnp.minimum(i, n_real_tiles - 1), 0)),
                    _resident((TM, D), lambda i: (0, 0))]
        args = [h, tail]
    in_specs += [
        _resident((1, D), lambda i: (0, 0)),
        _resident((None, None, D, D_FF), lambda i: (layer, half, 0, 0)),
        _resident((None, None, D, D_FF), lambda i: (layer, half, 0, 1)),
        _resident((None, None, D_FF, D), lambda i: (layer, half, 0, 0)),
    ]
    args += [g.reshape(1, D), w_gu, w_gu, w_down]
    if final:
        in_specs.append(_resident((1, D), lambda i: (0, 0)))
        args.append(final_g.reshape(1, D))
    return pl.pallas_call(
        functools.partial(_ffn_kernel, final=final, n_real_tiles=n_real_tiles),
        grid=(n_rows // TM,),
        in_specs=in_specs,
        out_specs=pl.BlockSpec((TM, D), lambda i: (i, 0)),
        out_shape=jax.ShapeDtypeStruct((n_rows, D), F32),
        compiler_params=_params(),
        name="ffn",
    )(*args)


N_PROJ = 5 * HEAD_W


def _attn_proj_kernel(x_ref, g_ref, w_ref, wki_ref, wwt_ref, lng_ref, lnb_ref,
                      proj_ref, ki_ref, wit_ref, vat_ref, vbt_ref):
    xn = _rms(x_ref[...], g_ref[...]).astype(BF16)
    y = jnp.dot(xn, w_ref[...], preferred_element_type=F32)
    scale = QK_SCALE
    proj_ref[:, 0:HEAD_W] = (y[:, 0:HEAD_W] * scale).astype(BF16)
    proj_ref[:, HEAD_W:3 * HEAD_W] = y[:, HEAD_W:3 * HEAD_W].astype(BF16)
    proj_ref[:, 3 * HEAD_W:4 * HEAD_W] = (y[:, 3 * HEAD_W:4 * HEAD_W] * scale).astype(BF16)
    proj_ref[:, 4 * HEAD_W:5 * HEAD_W] = y[:, 4 * HEAD_W:5 * HEAD_W].astype(BF16)
    vat_ref[0] = y[:, 5 * HEAD_W:6 * HEAD_W].T.astype(BF16)
    vbt_ref[0] = y[:, 6 * HEAD_W:7 * HEAD_W].T.astype(BF16)
    yk = jnp.dot(xn, wki_ref[...], preferred_element_type=F32)
    mu = jnp.mean(yk, axis=-1, keepdims=True)
    yc = yk - mu
    ln = yc * lax.rsqrt(jnp.mean(yc * yc, axis=-1, keepdims=True) + EPS)
    ki_ref[...] = (ln * lng_ref[...] + lnb_ref[...]).astype(BF16)
    wit = lax.dot_general(wwt_ref[...], xn, _NT, preferred_element_type=F32)
    wit_ref[...] = wit * (IDX_HEADS ** -0.5 * IDX_DIM ** -0.5)


def _attn_proj(h, g, w_main, w_ki, w_wt, ln_g, ln_b):
    n = h.shape[0]
    nt = n // TM
    const = lambda i: (0, 0)
    return pl.pallas_call(
        _attn_proj_kernel,
        grid=(nt,),
        in_specs=[
            pl.BlockSpec((TM, D), lambda i: (i, 0)),
            _resident((1, D), const),
            _resident((D, 7 * HEAD_W), const),
            _resident((D, IDX_DIM), const),
            _resident((16, D), const),
            _resident((1, IDX_DIM), const),
            _resident((1, IDX_DIM), const),
        ],
        out_specs=[
            pl.BlockSpec((TM, N_PROJ), lambda i: (i, 0)),
            pl.BlockSpec((TM, IDX_DIM), lambda i: (i, 0)),
            pl.BlockSpec((16, TM), lambda i: (0, i)),
            pl.BlockSpec((1, HEAD_W, TM), lambda i: (i, 0, 0)),
            pl.BlockSpec((1, HEAD_W, TM), lambda i: (i, 0, 0)),
        ],
        out_shape=[
            jax.ShapeDtypeStruct((n, N_PROJ), BF16),
            jax.ShapeDtypeStruct((n, IDX_DIM), BF16),
            jax.ShapeDtypeStruct((16, n), F32),
            jax.ShapeDtypeStruct((nt, HEAD_W, TM), BF16),
            jax.ShapeDtypeStruct((nt, HEAD_W, TM), BF16),
        ],
        compiler_params=_params(),
        name="attn_proj",
    )(h, g.reshape(1, D), w_main, w_ki, w_wt, ln_g.reshape(1, IDX_DIM), ln_b.reshape(1, IDX_DIM))


def _step_geometry(g, seq):
    nq_seq = seq // TQ
    n_real = pl.num_programs(0) - TM // TQ
    is_real = g < n_real
    i = g % nq_seq
    n_ch = jnp.where(is_real, (i * TQ + TQ + TK - 1) // TK, 0)
    return is_real, i, n_ch


def _stack_pair(q_pair):
    lane = lax.broadcasted_iota(I32, (TQ, 128), 1)
    qf = q_pair.astype(F32)
    top = jnp.where(lane < 64, qf, 0.0).astype(BF16)
    bot = jnp.where(lane >= 64, qf, 0.0).astype(BF16)
    return jnp.concatenate([top, bot], axis=0)


def _meta_bias(is_real):
    j = lax.broadcasted_iota(I32, (META_KEYS, 2 * TQ), 0)
    t = lax.broadcasted_iota(I32, (META_KEYS, 2 * TQ), 1) % TQ
    real = is_real.astype(I32)
    lim = jnp.minimum(real * N_META + (1 - real) * (t + 1), N_META)
    return jnp.where(j < lim, BIG, NEG)


def _bcast_halves(row, n_sub):
    return jnp.concatenate([jnp.broadcast_to(row[:, :TQ], (n_sub, TQ)),
                            jnp.broadcast_to(row[:, TQ:], (n_sub, TQ))], axis=0)


SCORE_LOOKAHEAD = 2
MAX_LAG = 64.0


def _softmax_pv(segments, v_tiles, base):
    issued, out = {}, []
    for h in range(SCORE_LOOKAHEAD):
        issued[h] = [fn() for fn in segments[h]]
    for h in range(4):
        if h + SCORE_LOOKAHEAD < 4:
            issued[h + SCORE_LOOKAHEAD] = [fn() for fn in segments[h + SCORE_LOOKAHEAD]]
        sts = issued.pop(h)
        m_loc = jnp.max(sts[0], axis=0, keepdims=True)
        for st in sts[1:]:
            m_loc = jnp.maximum(m_loc, jnp.max(st, axis=0, keepdims=True))
        m_use = base(h, m_loc)
        l_new, pv = None, None
        for st, vt in zip(sts, v_tiles[h]):
            pc = jnp.exp2(st - m_use)
            s = jnp.sum(pc, axis=0, keepdims=True)
            d = jnp.dot(vt(), pc.astype(BF16), preferred_element_type=F32)
            l_new = s if l_new is None else l_new + s
            pv = d if pv is None else pv + d
        out.append((m_loc, l_new, pv))
    return out


def _attend(segments, v_tiles, m_ref, l_ref, acc_ref, fold, first):
    fold_pv, fold_row = fold
    if first:
        for h, (m_loc, l_new, pv) in enumerate(_softmax_pv(segments, v_tiles, lambda h, m_loc: m_loc)):
            m_ref[h] = m_loc
            l_ref[h] = l_new
            acc_ref[h] = fold_pv(pv)
        return

    def commit(parts, rebased):
        for h, (m_loc, l_new, pv) in enumerate(parts):
            m_new = jnp.maximum(m_ref[h], m_loc)
            alpha = jnp.exp2(m_ref[h] - m_new)
            if rebased:
                l_ref[h] = alpha * l_ref[h] + l_new
                acc_ref[h] = acc_ref[h] * fold_row(alpha) + fold_pv(pv)
            else:
                l_ref[h] = alpha * (l_ref[h] + l_new)
                acc_ref[h] = (acc_ref[h] + fold_pv(pv)) * fold_row(alpha)
            m_ref[h] = m_new

    parts = _softmax_pv(segments, v_tiles, lambda h, m_loc: m_ref[h])
    rise = parts[0][0] - m_ref[0]
    for h in range(1, 4):
        rise = jnp.maximum(rise, parts[h][0] - m_ref[h])
    safe = jnp.max(rise) <= MAX_LAG

    @pl.when(safe)
    def _():
        commit(parts, rebased=False)

    @pl.when(jnp.logical_not(safe))
    def _():
        commit(_softmax_pv(segments, v_tiles, lambda h, m_loc: jnp.maximum(m_ref[h], m_loc)), rebased=True)


GROUP_KEYS = 256
GROUPS_PER_CHUNK = TK // GROUP_KEYS


def _bit_transpose32(words):
    a = list(words)
    j, mask = 16, 0x0000FFFF
    while j:
        k = 0
        while k < 32:
            t = (a[k] ^ lax.shift_right_logical(a[k + j], jnp.int32(j))) & mask
            a[k] = a[k] ^ t
            a[k + j] = a[k + j] ^ lax.shift_left(t, jnp.int32(j))
            k = (k + j + 1) & ~j
        j >>= 1
        mask ^= mask << j
    return a


def _dsa_kernel(qa_ref, qi_ref, wit_ref, ki_ref, ka_ref, vat_ref, km_ref, vmt_ref, o_ref,
                qall_ref, qstk_ref, key_ref, plane_ref, bias_ref, tri_ref, m_ref, l_ref, acc_ref, *, seq, top_k):
    g = pl.program_id(0)

    @pl.when(g == 0)
    def _():
        plane_ref[...] = jnp.zeros(plane_ref.shape, I32)
        r = lax.broadcasted_iota(I32, (TK, TK), 0)
        c_ = lax.broadcasted_iota(I32, (TK, TK), 1)
        tri_ref[...] = jnp.where(c_ <= r, 1.0, 0.0).astype(BF16)

    is_real, i, n_ch = _step_geometry(g, seq)
    q_pos = i * TQ + lax.broadcasted_iota(I32, (TK, TQ), 1)
    k_off = lax.broadcasted_iota(I32, (TK, TQ), 0)

    qi = qi_ref[...]
    for h in range(IDX_HEADS):
        qall_ref[h * TQ:(h + 1) * TQ, :] = qi[:, h * IDX_DIM:(h + 1) * IDX_DIM]
    for p in range(4):
        qstk_ref[p] = _stack_pair(qa_ref[:, p * 128:(p + 1) * 128])

    wt = wit_ref[...]

    def score_body(c, carry):
        kc = ki_ref[pl.ds(pl.multiple_of(c * TK, TK), TK), :]
        logits = lax.dot_general(kc, qall_ref[...], _NT, preferred_element_type=F32)
        sc = jnp.zeros((TK, TQ), F32)
        for h in range(IDX_HEADS):
            sc = sc + wt[h:h + 1, :] * jnp.maximum(logits[:, h * TQ:(h + 1) * TQ], 0.0)
        sc = jnp.where(sc == 0.0, 0.0, sc)
        bits = pltpu.bitcast(sc, I32)
        key = bits ^ (lax.shift_right_arithmetic(bits, 31) & 0x7FFFFFFF)
        key = jnp.where(c * TK + k_off <= q_pos, key, INT_MIN)
        key_ref[c] = key
        u = key ^ INT_MIN
        for gi in range(GROUPS_PER_CHUNK):
            base = gi * GROUP_KEYS
            planes = _bit_transpose32([u[base + w * 8:base + (w + 1) * 8, :] for w in range(32)])
            for b in range(32):
                plane_ref[c * GROUPS_PER_CHUNK + gi, b] = planes[b]
        return carry

    lax.fori_loop(0, n_ch, score_body, 0)

    n_grp = n_ch * GROUPS_PER_CHUNK
    n_groups = plane_ref.shape[0]
    alive0 = tuple(jnp.full((8, TQ), -1, I32) * (gq < n_grp).astype(I32) for gq in range(n_groups))

    def bit_body(b, carry):
        alive, k_rem, thr_u = carry
        ones = [a & plane_ref[gq, b] for gq, a in enumerate(alive)]
        cnt8 = lax.population_count(ones[0])
        for o in ones[1:]:
            cnt8 = cnt8 + lax.population_count(o)
        cnt = jnp.sum(cnt8, axis=0, keepdims=True)
        take = cnt >= k_rem
        alive = tuple(jnp.where(take, o, a ^ o) for a, o in zip(alive, ones))
        k_rem = jnp.where(take, k_rem, k_rem - cnt)
        thr_u = thr_u | jnp.where(take, lax.shift_left(jnp.int32(1), 31 - b), 0)
        return alive, k_rem, thr_u

    alive, k_rem, thr_u = lax.fori_loop(
        0, 32, bit_body, (alive0, jnp.full((1, TQ), top_k, I32), jnp.zeros((1, TQ), I32)))
    thr = thr_u ^ INT_MIN
    eq8 = lax.population_count(alive[0])
    for a in alive[1:]:
        eq8 = eq8 + lax.population_count(a)
    cnt_eq = jnp.sum(eq8, axis=0, keepdims=True)
    excess = jnp.where(thr_u != 0, cnt_eq - k_rem, 0)
    has_excess = jnp.max(excess) > 0

    @pl.when(jnp.logical_not(has_excess))
    def _():
        thr_eff = jnp.maximum(thr, INT_MIN + 1)

        def cap_body(c, carry):
            bias_ref[c] = jnp.where(key_ref[c] >= thr_eff, BIG, NEG)
            return carry

        lax.fori_loop(0, n_ch, cap_body, 0)

    @pl.when(has_excess)
    def _():
        need = k_rem.astype(F32)

        def cap_body(c, run):
            k = key_ref[c]
            eq = k == thr
            pre = jnp.dot(tri_ref[...], jnp.where(eq, 1.0, 0.0).astype(BF16), preferred_element_type=F32)
            rank = run + pre
            b = jnp.where(k > thr, BIG, jnp.where(eq, jnp.where(rank <= need, BIG, NEG), NEG))
            bias_ref[c] = jnp.where(k == INT_MIN, NEG, b)
            return run + pre[TK - 1:TK, :]

        lax.fori_loop(0, n_ch, cap_body, jnp.zeros((1, TQ), F32))

    fold = (lambda ot: jnp.concatenate([ot[:64, :TQ], ot[64:, TQ:]], axis=0), lambda row: _bcast_halves(row, 64))
    pairs = [slice(p * 128, (p + 1) * 128) for p in range(4)]

    def score(p, k_ref, idx, cap):
        return lambda: jnp.minimum(lax.dot_general(k_ref[idx], qstk_ref[p], _NT, preferred_element_type=F32), cap)

    def tile(ref, idx):
        return lambda: ref[idx]

    c_last = jnp.maximum(n_ch - 1, 0)
    row_last = pl.ds(pl.multiple_of(c_last * TK, TK), TK)
    mcap = _meta_bias(is_real)
    cap = jnp.where(is_real, bias_ref[c_last], NEG)
    cap2 = jnp.concatenate([cap, cap], axis=1)
    _attend([[score(p, km_ref, (slice(None), ps), mcap), score(p, ka_ref, (row_last, ps), cap2)]
             for p, ps in enumerate(pairs)],
            [[tile(vmt_ref, (0, ps, slice(0, META_KEYS))), tile(vat_ref, (c_last, ps, slice(None)))] for ps in pairs],
            m_ref, l_ref, acc_ref, fold, first=True)

    def update(chunks):
        rows = [pl.ds(pl.multiple_of(c * TK, TK), TK) for c in chunks]
        caps = [jnp.concatenate([bias_ref[c]] * 2, axis=1) for c in chunks]
        _attend([[score(p, ka_ref, (row, ps), cap2) for row, cap2 in zip(rows, caps)] for p, ps in enumerate(pairs)],
                [[tile(vat_ref, (c, ps, slice(None))) for c in chunks] for ps in pairs],
                m_ref, l_ref, acc_ref, fold, first=False)

    def pair_body(j, carry):
        update([2 * j, 2 * j + 1])
        return carry

    lax.fori_loop(0, c_last // 2, pair_body, 0)

    @pl.when(c_last % 2 == 1)
    def _():
        update([c_last - 1])

    for p, ps in enumerate(pairs):
        o_ref[:, ps] = (acc_ref[p] * _bcast_halves(1.0 / l_ref[p], 64)).T.astype(BF16)


def _dsa(proj, ki, wit, vat, batch, seq):
    n = proj.shape[0]
    nq_seq = seq // TQ
    n_real_q = batch * nq_seq
    nch = seq // TK
    top_k = min(TOPK_MAX, seq // 4)
    bidx = lambda g: jnp.minimum(g // nq_seq, batch - 1)
    meta_k = batch * seq // META_KEYS
    meta_tile = batch * nch
    return pl.pallas_call(
        functools.partial(_dsa_kernel, seq=seq, top_k=top_k),
        grid=(n // TQ,),
        in_specs=[
            pl.BlockSpec((TQ, HEAD_W), lambda g: (g, 0)),
            pl.BlockSpec((TQ, HEAD_W), lambda g: (g, 2)),
            pl.BlockSpec((16, TQ), lambda g: (0, g)),
            pl.BlockSpec((seq, IDX_DIM), lambda g: (bidx(g), 0)),
            pl.BlockSpec((seq, HEAD_W), lambda g: (bidx(g), 1)),
            pl.BlockSpec((nch, HEAD_W, TK), lambda g: (bidx(g), 0, 0)),
            pl.BlockSpec((META_KEYS, HEAD_W), lambda g: (meta_k, 1)),
            pl.BlockSpec((1, HEAD_W, TK), lambda g: (meta_tile, 0, 0)),
        ],
        out_specs=pl.BlockSpec((TQ, HEAD_W), lambda g: (g, 0)),
        out_shape=jax.ShapeDtypeStruct((n, HEAD_W), BF16),
        scratch_shapes=[
            pltpu.VMEM((IDX_HEADS * TQ, IDX_DIM), BF16),
            pltpu.VMEM((4, 2 * TQ, 128), BF16),
            pltpu.VMEM((nch, TK, TQ), I32),
            pltpu.VMEM((nch * GROUPS_PER_CHUNK, 32, 8, TQ), I32),
            pltpu.VMEM((nch, TK, TQ), F32),
            pltpu.VMEM((TK, TK), BF16),
            pltpu.VMEM((4, 1, 2 * TQ), F32),
            pltpu.VMEM((4, 1, 2 * TQ), F32),
            pltpu.VMEM((4, 128, TQ), F32),
        ],
        compiler_params=_params(),
        name="dsa",
    )(proj, proj, wit, ki, proj, vat, proj, vat)


def _diff_kernel(lam_ref, g_ref, qb_ref, kb_ref, vbt_ref, km_ref, vmt_ref, o_ref,
                 qstk_ref, m_ref, l_ref, acc_ref, *, seq, lam_init):
    g = pl.program_id(0)
    is_real, i, n_ch = _step_geometry(g, seq)
    lp = lam_ref[...]
    lam = (jnp.exp(jnp.sum(lp[0:1] * lp[1:2], axis=1, keepdims=True))
           - jnp.exp(jnp.sum(lp[2:3] * lp[3:4], axis=1, keepdims=True)) + lam_init)
    heads = [slice(h * 128, (h + 1) * 128) for h in range(4)]
    for h, hs in enumerate(heads):
        qstk_ref[h] = _stack_pair(qb_ref[:, hs])

    fold = (lambda ot: ot, lambda row: row)

    def score(h, k_ref, idx, cap=None):
        def fn():
            st = lax.dot_general(k_ref[idx], qstk_ref[h], _NT, preferred_element_type=F32)
            return st if cap is None else jnp.minimum(st, cap)
        return fn

    def tile(ref, idx):
        return lambda: ref[idx]

    n_full = n_ch - is_real.astype(I32)
    row_diag = pl.ds(pl.multiple_of(n_full * TK, TK), TK)
    mcap = _meta_bias(is_real)
    q_pos = i * TQ + lax.broadcasted_iota(I32, (TK, 2 * TQ), 1) % TQ
    k_pos = n_full * TK + lax.broadcasted_iota(I32, (TK, 2 * TQ), 0)
    cap = jnp.where(jnp.where(k_pos <= q_pos, is_real.astype(I32), 0) > 0, BIG, NEG)
    _attend([[score(h, km_ref, (slice(None), hs), mcap), score(h, kb_ref, (row_diag, hs), cap)]
             for h, hs in enumerate(heads)],
            [[tile(vmt_ref, (0, hs, slice(0, META_KEYS))), tile(vbt_ref, (n_full, hs, slice(None)))] for hs in heads],
            m_ref, l_ref, acc_ref, fold, first=True)

    def update(chunks):
        rows = [pl.ds(pl.multiple_of(c * TK, TK), TK) for c in chunks]
        _attend([[score(h, kb_ref, (row, hs)) for row in rows] for h, hs in enumerate(heads)],
                [[tile(vbt_ref, (c, hs, slice(None))) for c in chunks] for hs in heads],
                m_ref, l_ref, acc_ref, fold, first=False)

    def pair_body(j, carry):
        update([2 * j, 2 * j + 1])
        return carry

    lax.fori_loop(0, n_full // 2, pair_body, 0)

    @pl.when(n_full % 2 == 1)
    def _():
        update([n_full - 1])

    for h, hs in enumerate(heads):
        on = acc_ref[h] * (1.0 / l_ref[h])
        o = on[:, :TQ] - lam * on[:, TQ:]
        y = o * lax.rsqrt(jnp.mean(o * o, axis=0, keepdims=True) + EPS) * g_ref[...] * (1.0 - lam_init)
        o_ref[:, hs] = y.T.astype(BF16)


def _diff(proj, vbt, lam_params, subln_g, batch, seq, lam_init):
    n = proj.shape[0]
    nq_seq = seq // TQ
    nch = seq // TK
    bidx = lambda g: jnp.minimum(g // nq_seq, batch - 1)
    meta_k = batch * seq // META_KEYS
    meta_tile = batch * nch
    return pl.pallas_call(
        functools.partial(_diff_kernel, seq=seq, lam_init=lam_init),
        grid=(n // TQ,),
        in_specs=[
            _resident((4, 64), lambda g: (0, 0)),
            _resident((128, 1), lambda g: (0, 0)),
            pl.BlockSpec((TQ, HEAD_W), lambda g: (g, 3)),
            pl.BlockSpec((seq, HEAD_W), lambda g: (bidx(g), 4)),
            pl.BlockSpec((nch, HEAD_W, TK), lambda g: (bidx(g), 0, 0)),
            pl.BlockSpec((META_KEYS, HEAD_W), lambda g: (meta_k, 4)),
            pl.BlockSpec((1, HEAD_W, TK), lambda g: (meta_tile, 0, 0)),
        ],
        out_specs=pl.BlockSpec((TQ, HEAD_W), lambda g: (g, 0)),
        out_shape=jax.ShapeDtypeStruct((n, HEAD_W), BF16),
        scratch_shapes=[
            pltpu.VMEM((4, 2 * TQ, 128), BF16),
            pltpu.VMEM((4, 1, 2 * TQ), F32),
            pltpu.VMEM((4, 1, 2 * TQ), F32),
            pltpu.VMEM((4, 128, 2 * TQ), F32),
        ],
        compiler_params=_params(),
        name="diff_attn",
    )(lam_params, subln_g.reshape(128, 1), proj, proj, vbt, proj, vbt)


def _out_proj_kernel(h_ref, oa_ref, ob_ref, wa_ref, wb_ref, o_ref):
    o_ref[...] = (h_ref[...]
                  + jnp.dot(oa_ref[...], wa_ref[...], preferred_element_type=F32)
                  + jnp.dot(ob_ref[...], wb_ref[...], preferred_element_type=F32))


def _out_proj(h, oa, ob, w_out):
    n = h.shape[0]
    return pl.pallas_call(
        _out_proj_kernel,
        grid=(n // TM,),
        in_specs=[
            pl.BlockSpec((TM, D), lambda i: (i, 0)),
            pl.BlockSpec((TM, HEAD_W), lambda i: (i, 0)),
            pl.BlockSpec((TM, HEAD_W), lambda i: (i, 0)),
            _resident((HEAD_W, D), lambda i: (0, 0)),
            _resident((HEAD_W, D), lambda i: (1, 0)),
        ],
        out_specs=pl.BlockSpec((TM, D), lambda i: (i, 0)),
        out_shape=jax.ShapeDtypeStruct((n, D), F32),
        compiler_params=_params(),
        name="attn_out_proj",
    )(h, oa, ob, w_out, w_out)


def _rglru_kernel(x_ref, g_ref, win_ref, cw_ref, cb_ref, gw_ref, gb_ref, lam_ref, wout_ref, o_ref,
                  xbuf, a_s, u_s, hstate, mstate, mhist, *, tiles_per_seq):
    g = pl.program_id(0)
    x = x_ref[...]
    xn = _rms(x, g_ref[...]).astype(BF16)
    yx = jnp.dot(xn, win_ref[...], preferred_element_type=F32)
    yb = yx[:, :D]
    y_br = 0.5 * yb * (1.0 + jnp.tanh(math.sqrt(2.0 / math.pi) * (yb + 0.044715 * (yb * yb * yb))))

    @pl.when(g == 0)
    def _():
        xbuf[0:8, :] = jnp.zeros((8, D), F32)
        hstate[...] = jnp.zeros(hstate.shape, F32)

    @pl.when(jnp.logical_and(g >= 1, (g - 1) % tiles_per_seq == 0))
    def _():
        xbuf[0:8, :] = mhist[...]
        hstate[...] = mstate[...]

    xbuf[8:, :] = yx[:, D:]
    xc = cb_ref[...] + cw_ref[0:1, :] * xbuf[5:5 + TM, :]
    for j in range(1, CONV_W):
        xc = xc + cw_ref[j:j + 1, :] * xbuf[5 + j:5 + j + TM, :]

    gx, ga = [], []
    for nb in range(LRU_BLOCKS):
        xb = xc[:, nb * LRU_BLOCK_W:(nb + 1) * LRU_BLOCK_W].astype(BF16)
        gx.append(jnp.dot(xb, gw_ref[0, nb], preferred_element_type=F32))
        ga.append(jnp.dot(xb, gw_ref[1, nb], preferred_element_type=F32))
    gate_x = _sigmoid(jnp.concatenate(gx, axis=1) + gb_ref[0:1, :])
    gate_a = _sigmoid(jnp.concatenate(ga, axis=1) + gb_ref[1:2, :])
    lam = lam_ref[...]
    log_sig = jnp.minimum(lam, 0.0) - jnp.log(1.0 + jnp.exp(-jnp.abs(lam)))
    log_a = RG_C * gate_a * log_sig
    a = jnp.exp(log_a)
    a_s[...] = a
    u_s[...] = jnp.sqrt(-jnp.tanh(log_a) * (a * a + 1.0)) * (gate_x * xc)

    def step(t, hcur):
        hcur = a_s[pl.ds(t, 1), :] * hcur + u_s[pl.ds(t, 1), :]
        u_s[pl.ds(t, 1), :] = hcur
        return hcur

    hstate[...] = lax.fori_loop(0, TM, step, hstate[...], unroll=8)

    @pl.when(g == 0)
    def _():
        mstate[...] = u_s[N_META - 1:N_META, :]
        mhist[...] = xbuf[N_META:N_META + 8, :]

    xbuf[0:8, :] = xbuf[TM:TM + 8, :]
    o_ref[...] = x + jnp.dot((u_s[...] * y_br).astype(BF16), wout_ref[...], preferred_element_type=F32)


def _rglru(h, g, w_in, conv_w, conv_b, gate_w, gate_b, lru_lambda, w_out, tiles_per_seq):
    n = h.shape[0]
    nt = n // TM
    tile = lambda s: (jnp.where(s == 0, nt - 1, s - 1), 0)
    const2 = lambda s: (0, 0)
    return pl.pallas_call(
        functools.partial(_rglru_kernel, tiles_per_seq=tiles_per_seq),
        grid=(nt,),
        in_specs=[
            pl.BlockSpec((TM, D), tile),
            _resident((1, D), const2),
            _resident((D, 2 * D), const2),
            _resident((CONV_W, D), const2),
            _resident((1, D), const2),
            _resident((2, LRU_BLOCKS, LRU_BLOCK_W, LRU_BLOCK_W), lambda s: (0, 0, 0, 0)),
            _resident((2, D), const2),
            _resident((1, D), const2),
            _resident((D, D), const2),
        ],
        out_specs=pl.BlockSpec((TM, D), tile),
        out_shape=jax.ShapeDtypeStruct((n, D), F32),
        scratch_shapes=[
            pltpu.VMEM((TM + 8, D), F32),
            pltpu.VMEM((TM, D), F32),
            pltpu.VMEM((TM, D), F32),
            pltpu.VMEM((1, D), F32),
            pltpu.VMEM((1, D), F32),
            pltpu.VMEM((8, D), F32),
        ],
        compiler_params=_params(),
        name="rglru",
    )(h, g.reshape(1, D), w_in, conv_w, conv_b.reshape(1, D), gate_w, gate_b, lru_lambda.reshape(1, D), w_out)


def _split_attn_w_in(w):
    a = HEAD_W
    o_ki = 4 * a
    o_wi = o_ki + IDX_DIM
    o_qb = o_wi + IDX_HEADS
    qa, ka, va, qi = w[:, 0:a], w[:, a:2 * a], w[:, 2 * a:3 * a], w[:, 3 * a:4 * a]
    ki, wi = w[:, o_ki:o_wi], w[:, o_wi:o_qb]
    qb, kb, vb = w[:, o_qb:o_qb + a], w[:, o_qb + a:o_qb + 2 * a], w[:, o_qb + 2 * a:o_qb + 3 * a]
    w_main = jnp.concatenate([qa, ka, qi, qb, kb, va, vb], axis=1).astype(BF16)
    w_wt = jnp.concatenate([wi.T, jnp.zeros((16 - IDX_HEADS, D), w.dtype)], axis=0).astype(BF16)
    return w_main, ki.astype(BF16), w_wt


def kernel(x, meta_tokens, norm_g, ffn_w_gu, ffn_w_down, attn_w_in, idx_k_ln_g, idx_k_ln_b, diff_lambda, diff_subln_g, attn_w_out, rec_w_in, rec_conv_w, rec_conv_b, rec_gate_w, rec_gate_b, rec_lambda, rec_w_out, final_norm_g):
    batch, seq, _ = x.shape
    depth = norm_g.shape[0]
    n_real = batch * seq
    tail = jnp.concatenate([meta_tokens.astype(x.dtype), jnp.zeros((TM - N_META, D), x.dtype)], axis=0)
    w_gu = ffn_w_gu.astype(BF16)
    w_down = ffn_w_down.astype(BF16)
    h = x.reshape(n_real, D)
    for i in range(depth):
        j = i // 2
        h = _ffn(h, norm_g[i, 0], w_gu, w_down, i, 0, tail=tail if i == 0 else None)
        if i % 2 == 0:
            lam_init = 0.8 - 0.6 * math.exp(-0.3 * i)
            w_main, w_ki, w_wt = _split_attn_w_in(attn_w_in[j])
            proj, ki, wit, vat, vbt = _attn_proj(h, norm_g[i, 1], w_main, w_ki, w_wt, idx_k_ln_g[j], idx_k_ln_b[j])
            oa = _dsa(proj, ki, wit, vat, batch, seq)
            ob = _diff(proj, vbt, diff_lambda[j], diff_subln_g[j], batch, seq, lam_init)
            h = _out_proj(h, oa, ob, attn_w_out[j].astype(BF16))
        else:
            h = _rglru(h, norm_g[i, 1], rec_w_in[j].astype(BF16), rec_conv_w[j], rec_conv_b[j],
                       rec_gate_w[j].astype(BF16), rec_gate_b[j], rec_lambda[j], rec_w_out[j].astype(BF16),
                       seq // TM)
        if i == depth - 1:
            h = _ffn(h, norm_g[i, 2], w_gu, w_down, i, 1, final_g=final_norm_g, n_rows=n_real)
        else:
            h = _ffn(h, norm_g[i, 2], w_gu, w_down, i, 1)
    return h.reshape(batch, seq, D)
```

```python
import functools
import math

import jax
import jax.numpy as jnp
from jax import lax
from jax.experimental import pallas as pl
from jax.experimental.pallas import tpu as pltpu

F32 = jnp.float32
BF16 = jnp.bfloat16
I32 = jnp.int32

D = 1024
N_META = 16
D_FF = 2816
EPS = 1e-6
HEAD_W = 512
IDX_DIM = 64
IDX_HEADS = 8
TOPK_MAX = 256
LRU_BLOCKS = 4
LRU_BLOCK_W = D // LRU_BLOCKS
CONV_W = 4
RG_C = 8.0

TM = 512
TQ = 256
META_KEYS = 128
TK = 512
SCAN_SEGS = 4
MXU_TILE = 256
FF_CHUNKS = ((0, 6 * MXU_TILE), (6 * MXU_TILE, D_FF))
NEG = -1e30
BIG = 1e30
QK_SCALE = IDX_DIM ** -0.5 * math.log2(math.e)
INT_MIN = -2 ** 31
VMEM_LIMIT = 56 * 1024 * 1024

_NT = (((1,), (1,)), ((), ()))


def _params():
    return pltpu.CompilerParams(dimension_semantics=("arbitrary",), vmem_limit_bytes=VMEM_LIMIT)


def _resident(shape, index_map):
    return pl.BlockSpec(shape, index_map, pipeline_mode=pl.Buffered(1))


def _rms(x, g):
    return x * lax.rsqrt(jnp.mean(x * x, axis=-1, keepdims=True) + EPS) * g


def _sigmoid(x):
    return 0.5 * jnp.tanh(0.5 * x) + 0.5


def _ffn_kernel(x_ref, *refs, final, n_real_tiles):
    refs = list(refs)
    o_ref = refs.pop()
    tail_ref = refs.pop(0) if n_real_tiles is not None else None
    g_ref, wg_ref, wu_ref, wd_ref = refs[:4]
    x = x_ref[...]
    if tail_ref is not None:
        x = jnp.where(pl.program_id(0) < n_real_tiles, x, tail_ref[...])
    xn = _rms(x, g_ref[...]).astype(BF16)
    acc = None
    for lo, hi in FF_CHUNKS:
        sl = slice(lo, hi)
        g = jnp.dot(xn, wg_ref[:, sl], preferred_element_type=F32)
        u = jnp.dot(xn, wu_ref[:, sl], preferred_element_type=F32)
        a = (g * _sigmoid(g) * u).astype(BF16)
        part = jnp.dot(a, wd_ref[sl, :], preferred_element_type=F32)
        acc = part if acc is None else acc + part
    y = x + 0.5 * acc
    if final:
        y = _rms(y, refs[4][...])
    o_ref[...] = y


def _ffn(h, g, w_gu, w_down, layer, half, tail=None, final_g=None, n_rows=None):
    n_real_tiles = None if tail is None else h.shape[0] // TM
    if n_rows is None:
        n_rows = h.shape[0] + (0 if tail is None else TM)
    final = final_g is not None
    if tail is None:
        in_specs, args = [pl.BlockSpec((TM, D), lambda i: (i, 0))], [h]
    else:
        in_specs = [pl.BlockSpec((TM, D), lambda i: (jnp.minimum(i, n_real_tiles - 1), 0)),
                    _resident((TM, D), lambda i: (0, 0))]
        args = [h, tail]
    in_specs += [
        _resident((1, D), lambda i: (0, 0)),
        _resident((None, None, D, D_FF), lambda i: (layer, half, 0, 0)),
        _resident((None, None, D, D_FF), lambda i: (layer, half, 0, 1)),
        _resident((None, None, D_FF, D), lambda i: (layer, half, 0, 0)),
    ]
    args += [g.reshape(1, D), w_gu, w_gu, w_down]
    if final:
        in_specs.append(_resident((1, D), lambda i: (0, 0)))
        args.append(final_g.reshape(1, D))
    return pl.pallas_call(
        functools.partial(_ffn_kernel, final=final, n_real_tiles=n_real_tiles),
        grid=(n_rows // TM,),
        in_specs=in_specs,
        out_specs=pl.BlockSpec((TM, D), lambda i: (i, 0)),
        out_shape=jax.ShapeDtypeStruct((n_rows, D), F32),
        compiler_params=_params(),
        name="ffn",
    )(*args)


N_PROJ = 5 * HEAD_W


def _attn_proj_kernel(x_ref, g_ref, w_ref, wki_ref, wwt_ref, lng_ref, lnb_ref,
                      proj_ref, ki_ref, wit_ref, vat_ref, vbt_ref):
    xn = _rms(x_ref[...], g_ref[...]).astype(BF16)
    y = jnp.dot(xn, w_ref[...], preferred_element_type=F32)
    scale = QK_SCALE
    proj_ref[:, 0:HEAD_W] = (y[:, 0:HEAD_W] * scale).astype(BF16)
    proj_ref[:, HEAD_W:3 * HEAD_W] = y[:, HEAD_W:3 * HEAD_W].astype(BF16)
    proj_ref[:, 3 * HEAD_W:4 * HEAD_W] = (y[:, 3 * HEAD_W:4 * HEAD_W] * scale).astype(BF16)
    proj_ref[:, 4 * HEAD_W:5 * HEAD_W] = y[:, 4 * HEAD_W:5 * HEAD_W].astype(BF16)
    vat_ref[0] = y[:, 5 * HEAD_W:6 * HEAD_W].T.astype(BF16)
    vbt_ref[0] = y[:, 6 * HEAD_W:7 * HEAD_W].T.astype(BF16)
    yk = jnp.dot(xn, wki_ref[...], preferred_element_type=F32)
    mu = jnp.mean(yk, axis=-1, keepdims=True)
    yc = yk - mu
    ln = yc * lax.rsqrt(jnp.mean(yc * yc, axis=-1, keepdims=True) + EPS)
    ki_ref[...] = (ln * lng_ref[...] + lnb_ref[...]).astype(BF16)
    wit = lax.dot_general(wwt_ref[...], xn, _NT, preferred_element_type=F32)
    wit_ref[...] = wit * (IDX_HEADS ** -0.5 * IDX_DIM ** -0.5)


def _attn_proj(h, g, w_main, w_ki, w_wt, ln_g, ln_b):
    n = h.shape[0]
    nt = n // TM
    const = lambda i: (0, 0)
    return pl.pallas_call(
        _attn_proj_kernel,
        grid=(nt,),
        in_specs=[
            pl.BlockSpec((TM, D), lambda i: (i, 0)),
            _resident((1, D), const),
            _resident((D, 7 * HEAD_W), const),
            _resident((D, IDX_DIM), const),
            _resident((16, D), const),
            _resident((1, IDX_DIM), const),
            _resident((1, IDX_DIM), const),
        ],
        out_specs=[
            pl.BlockSpec((TM, N_PROJ), lambda i: (i, 0)),
            pl.BlockSpec((TM, IDX_DIM), lambda i: (i, 0)),
            pl.BlockSpec((16, TM), lambda i: (0, i)),
            pl.BlockSpec((1, HEAD_W, TM), lambda i: (i, 0, 0)),
            pl.BlockSpec((1, HEAD_W, TM), lambda i: (i, 0, 0)),
        ],
        out_shape=[
            jax.ShapeDtypeStruct((n, N_PROJ), BF16),
            jax.ShapeDtypeStruct((n, IDX_DIM), BF16),
            jax.ShapeDtypeStruct((16, n), F32),
            jax.ShapeDtypeStruct((nt, HEAD_W, TM), BF16),
            jax.ShapeDtypeStruct((nt, HEAD_W, TM), BF16),
        ],
        compiler_params=_params(),
        name="attn_proj",
    )(h, g.reshape(1, D), w_main, w_ki, w_wt, ln_g.reshape(1, IDX_DIM), ln_b.reshape(1, IDX_DIM))


def _step_geometry(g, seq):
    nq_seq = seq // TQ
    n_real = pl.num_programs(0) - TM // TQ
    is_real = g < n_real
    i = g % nq_seq
    n_ch = jnp.where(is_real, (i * TQ + TQ + TK - 1) // TK, 0)
    return is_real, i, n_ch


def _stack_pair(q_pair):
    lane = lax.broadcasted_iota(I32, (TQ, 128), 1)
    qf = q_pair.astype(F32)
    top = jnp.where(lane < 64, qf, 0.0).astype(BF16)
    bot = jnp.where(lane >= 64, qf, 0.0).astype(BF16)
    return jnp.concatenate([top, bot], axis=0)


def _meta_bias(is_real):
    j = lax.broadcasted_iota(I32, (META_KEYS, 2 * TQ), 0)
    t = lax.broadcasted_iota(I32, (META_KEYS, 2 * TQ), 1) % TQ
    real = is_real.astype(I32)
    lim = jnp.minimum(real * N_META + (1 - real) * (t + 1), N_META)
    return jnp.where(j < lim, BIG, NEG)


def _bcast_halves(row, n_sub):
    return jnp.concatenate([jnp.broadcast_to(row[:, :TQ], (n_sub, TQ)),
                            jnp.broadcast_to(row[:, TQ:], (n_sub, TQ))], axis=0)


SCORE_LOOKAHEAD = 2
MAX_LAG = 64.0


def _softmax_pv(segments, v_tiles, base):
    issued, out = {}, []
    for h in range(SCORE_LOOKAHEAD):
        issued[h] = [fn() for fn in segments[h]]
    for h in range(4):
        if h + SCORE_LOOKAHEAD < 4:
            issued[h + SCORE_LOOKAHEAD] = [fn() for fn in segments[h + SCORE_LOOKAHEAD]]
        sts = issued.pop(h)
        m_loc = jnp.max(sts[0], axis=0, keepdims=True)
        for st in sts[1:]:
            m_loc = jnp.maximum(m_loc, jnp.max(st, axis=0, keepdims=True))
        m_use = base(h, m_loc)
        l_new, pv = None, None
        for st, vt in zip(sts, v_tiles[h]):
            pc = jnp.exp2(st - m_use)
            s = jnp.sum(pc, axis=0, keepdims=True)
            d = jnp.dot(vt(), pc.astype(BF16), preferred_element_type=F32)
            l_new = s if l_new is None else l_new + s
            pv = d if pv is None else pv + d
        out.append((m_loc, l_new, pv))
    return out


def _attend(segments, v_tiles, m_ref, l_ref, acc_ref, fold, first):
    fold_pv, fold_row = fold
    if first:
        for h, (m_loc, l_new, pv) in enumerate(_softmax_pv(segments, v_tiles, lambda h, m_loc: m_loc)):
            m_ref[h] = m_loc
            l_ref[h] = l_new
            acc_ref[h] = fold_pv(pv)
        return

    def commit(parts, rebased):
        for h, (m_loc, l_new, pv) in enumerate(parts):
            m_new = jnp.maximum(m_ref[h], m_loc)
            alpha = jnp.exp2(m_ref[h] - m_new)
            if rebased:
                l_ref[h] = alpha * l_ref[h] + l_new
                acc_ref[h] = acc_ref[h] * fold_row(alpha) + fold_pv(pv)
            else:
                l_ref[h] = alpha * (l_ref[h] + l_new)
                acc_ref[h] = (acc_ref[h] + fold_pv(pv)) * fold_row(alpha)
            m_ref[h] = m_new

    parts = _softmax_pv(segments, v_tiles, lambda h, m_loc: m_ref[h])
    rise = parts[0][0] - m_ref[0]
    for h in range(1, 4):
        rise = jnp.maximum(rise, parts[h][0] - m_ref[h])
    safe = jnp.max(rise) <= MAX_LAG

    @pl.when(safe)
    def _():
        commit(parts, rebased=False)

    @pl.when(jnp.logical_not(safe))
    def _():
        commit(_softmax_pv(segments, v_tiles, lambda h, m_loc: jnp.maximum(m_ref[h], m_loc)), rebased=True)


GROUP_KEYS = 256
GROUPS_PER_CHUNK = TK // GROUP_KEYS


def _bit_transpose32(words):
    a = list(words)
    j, mask = 16, 0x0000FFFF
    while j:
        k = 0
        while k < 32:
            t = (a[k] ^ lax.shift_right_logical(a[k + j], jnp.int32(j))) & mask
            a[k] = a[k] ^ t
            a[k + j] = a[k + j] ^ lax.shift_left(t, jnp.int32(j))
            k = (k + j + 1) & ~j
        j >>= 1
        mask ^= mask << j
    return a


def _dsa_kernel(qa_ref, qi_ref, wit_ref, ki_ref, ka_ref, vat_ref, km_ref, vmt_ref, o_ref,
                qall_ref, qstk_ref, key_ref, plane_ref, bias_ref, tri_ref, m_ref, l_ref, acc_ref, *, seq, top_k):
    g = pl.program_id(0)

    @pl.when(g == 0)
    def _():
        plane_ref[...] = jnp.zeros(plane_ref.shape, I32)
        r = lax.broadcasted_iota(I32, (TK, TK), 0)
        c_ = lax.broadcasted_iota(I32, (TK, TK), 1)
        tri_ref[...] = jnp.where(c_ <= r, 1.0, 0.0).astype(BF16)

    is_real, i, n_ch = _step_geometry(g, seq)
    q_pos = i * TQ + lax.broadcasted_iota(I32, (TK, TQ), 1)
    k_off = lax.broadcasted_iota(I32, (TK, TQ), 0)

    qi = qi_ref[...]
    for h in range(IDX_HEADS):
        qall_ref[h * TQ:(h + 1) * TQ, :] = qi[:, h * IDX_DIM:(h + 1) * IDX_DIM]
    for p in range(4):
        qstk_ref[p] = _stack_pair(qa_ref[:, p * 128:(p + 1) * 128])

    wt = wit_ref[...]

    def score_body(c, carry):
        kc = ki_ref[pl.ds(pl.multiple_of(c * TK, TK), TK), :]
        logits = lax.dot_general(kc, qall_ref[...], _NT, preferred_element_type=F32)
        sc = jnp.zeros((TK, TQ), F32)
        for h in range(IDX_HEADS):
            sc = sc + wt[h:h + 1, :] * jnp.maximum(logits[:, h * TQ:(h + 1) * TQ], 0.0)
        sc = jnp.where(sc == 0.0, 0.0, sc)
        bits = pltpu.bitcast(sc, I32)
        key = bits ^ (lax.shift_right_arithmetic(bits, 31) & 0x7FFFFFFF)
        key = jnp.where(c * TK + k_off <= q_pos, key, INT_MIN)
        key_ref[c] = key
        u = key ^ INT_MIN
        for gi in range(GROUPS_PER_CHUNK):
            base = gi * GROUP_KEYS
            planes = _bit_transpose32([u[base + w * 8:base + (w + 1) * 8, :] for w in range(32)])
            for b in range(32):
                plane_ref[c * GROUPS_PER_CHUNK + gi, b] = planes[b]
        return carry

    lax.fori_loop(0, n_ch, score_body, 0)

    n_grp = n_ch * GROUPS_PER_CHUNK
    n_groups = plane_ref.shape[0]
    alive0 = tuple(jnp.full((8, TQ), -1, I32) * (gq < n_grp).astype(I32) for gq in range(n_groups))

    def bit_body(b, carry):
        alive, k_rem, thr_u = carry
        ones = [a & plane_ref[gq, b] for gq, a in enumerate(alive)]
        cnt8 = lax.population_count(ones[0])
        for o in ones[1:]:
            cnt8 = cnt8 + lax.population_count(o)
        cnt = jnp.sum(cnt8, axis=0, keepdims=True)
        take = cnt >= k_rem
        alive = tuple(jnp.where(take, o, a ^ o) for a, o in zip(alive, ones))
        k_rem = jnp.where(take, k_rem, k_rem - cnt)
        thr_u = thr_u | jnp.where(take, lax.shift_left(jnp.int32(1), 31 - b), 0)
        return alive, k_rem, thr_u

    alive, k_rem, thr_u = lax.fori_loop(
        0, 32, bit_body, (alive0, jnp.full((1, TQ), top_k, I32), jnp.zeros((1, TQ), I32)))
    thr = thr_u ^ INT_MIN
    eq8 = lax.population_count(alive[0])
    for a in alive[1:]:
        eq8 = eq8 + lax.population_count(a)
    cnt_eq = jnp.sum(eq8, axis=0, keepdims=True)
    excess = jnp.where(thr_u != 0, cnt_eq - k_rem, 0)
    has_excess = jnp.max(excess) > 0

    @pl.when(jnp.logical_not(has_excess))
    def _():
        thr_eff = jnp.maximum(thr, INT_MIN + 1)

        def cap_body(c, carry):
            bias_ref[c] = jnp.where(key_ref[c] >= thr_eff, BIG, NEG)
            return carry

        lax.fori_loop(0, n_ch, cap_body, 0)

    @pl.when(has_excess)
    def _():
        need = k_rem.astype(F32)

        def cap_body(c, run):
            k = key_ref[c]
            eq = k == thr
            pre = jnp.dot(tri_ref[...], jnp.where(eq, 1.0, 0.0).astype(BF16), preferred_element_type=F32)
            rank = run + pre
            b = jnp.where(k > thr, BIG, jnp.where(eq, jnp.where(rank <= need, BIG, NEG), NEG))
            bias_ref[c] = jnp.where(k == INT_MIN, NEG, b)
            return run + pre[TK - 1:TK, :]

        lax.fori_loop(0, n_ch, cap_body, jnp.zeros((1, TQ), F32))

    fold = (lambda ot: jnp.concatenate([ot[:64, :TQ], ot[64:, TQ:]], axis=0), lambda row: _bcast_halves(row, 64))
    pairs = [slice(p * 128, (p + 1) * 128) for p in range(4)]

    def score(p, k_ref, idx, cap):
        return lambda: jnp.minimum(lax.dot_general(k_ref[idx], qstk_ref[p], _NT, preferred_element_type=F32), cap)

    def tile(ref, idx):
        return lambda: ref[idx]

    c_last = jnp.maximum(n_ch - 1, 0)
    row_last = pl.ds(pl.multiple_of(c_last * TK, TK), TK)
    mcap = _meta_bias(is_real)
    cap = jnp.where(is_real, bias_ref[c_last], NEG)
    cap2 = jnp.concatenate([cap, cap], axis=1)
    _attend([[score(p, km_ref, (slice(None), ps), mcap), score(p, ka_ref, (row_last, ps), cap2)]
             for p, ps in enumerate(pairs)],
            [[tile(vmt_ref, (0, ps, slice(0, META_KEYS))), tile(vat_ref, (c_last, ps, slice(None)))] for ps in pairs],
            m_ref, l_ref, acc_ref, fold, first=True)

    def update(chunks):
        rows = [pl.ds(pl.multiple_of(c * TK, TK), TK) for c in chunks]
        caps = [jnp.concatenate([bias_ref[c]] * 2, axis=1) for c in chunks]
        _attend([[score(p, ka_ref, (row, ps), cap2) for row, cap2 in zip(rows, caps)] for p, ps in enumerate(pairs)],
                [[tile(vat_ref, (c, ps, slice(None))) for c in chunks] for ps in pairs],
                m_ref, l_ref, acc_ref, fold, first=False)

    def pair_body(j, carry):
        update([2 * j, 2 * j + 1])
        return carry

    lax.fori_loop(0, c_last // 2, pair_body, 0)

    @pl.when(c_last % 2 == 1)
    def _():
        update([c_last - 1])

    for p, ps in enumerate(pairs):
        o_ref[:, ps] = (acc_ref[p] * _bcast_halves(1.0 / l_ref[p], 64)).T.astype(BF16)


def _dsa(proj, ki, wit, vat, batch, seq):
    n = proj.shape[0]
    nq_seq = seq // TQ
    n_real_q = batch * nq_seq
    nch = seq // TK
    top_k = min(TOPK_MAX, seq // 4)
    bidx = lambda g: jnp.minimum(g // nq_seq, batch - 1)
    meta_k = batch * seq // META_KEYS
    meta_tile = batch * nch
    return pl.pallas_call(
        functools.partial(_dsa_kernel, seq=seq, top_k=top_k),
        grid=(n // TQ,),
        in_specs=[
            pl.BlockSpec((TQ, HEAD_W), lambda g: (g, 0)),
            pl.BlockSpec((TQ, HEAD_W), lambda g: (g, 2)),
            pl.BlockSpec((16, TQ), lambda g: (0, g)),
            pl.BlockSpec((seq, IDX_DIM), lambda g: (bidx(g), 0)),
            pl.BlockSpec((seq, HEAD_W), lambda g: (bidx(g), 1)),
            pl.BlockSpec((nch, HEAD_W, TK), lambda g: (bidx(g), 0, 0)),
            pl.BlockSpec((META_KEYS, HEAD_W), lambda g: (meta_k, 1)),
            pl.BlockSpec((1, HEAD_W, TK), lambda g: (meta_tile, 0, 0)),
        ],
        out_specs=pl.BlockSpec((TQ, HEAD_W), lambda g: (g, 0)),
        out_shape=jax.ShapeDtypeStruct((n, HEAD_W), BF16),
        scratch_shapes=[
            pltpu.VMEM((IDX_HEADS * TQ, IDX_DIM), BF16),
            pltpu.VMEM((4, 2 * TQ, 128), BF16),
            pltpu.VMEM((nch, TK, TQ), I32),
            pltpu.VMEM((nch * GROUPS_PER_CHUNK, 32, 8, TQ), I32),
            pltpu.VMEM((nch, TK, TQ), F32),
            pltpu.VMEM((TK, TK), BF16),
            pltpu.VMEM((4, 1, 2 * TQ), F32),
            pltpu.VMEM((4, 1, 2 * TQ), F32),
            pltpu.VMEM((4, 128, TQ), F32),
        ],
        compiler_params=_params(),
        name="dsa",
    )(proj, proj, wit, ki, proj, vat, proj, vat)


def _diff_kernel(lam_ref, g_ref, qb_ref, kb_ref, vbt_ref, km_ref, vmt_ref, o_ref,
                 qstk_ref, m_ref, l_ref, acc_ref, *, seq, lam_init):
    g = pl.program_id(0)
    is_real, i, n_ch = _step_geometry(g, seq)
    lp = lam_ref[...]
    lam = (jnp.exp(jnp.sum(lp[0:1] * lp[1:2], axis=1, keepdims=True))
           - jnp.exp(jnp.sum(lp[2:3] * lp[3:4], axis=1, keepdims=True)) + lam_init)
    heads = [slice(h * 128, (h + 1) * 128) for h in range(4)]
    for h, hs in enumerate(heads):
        qstk_ref[h] = _stack_pair(qb_ref[:, hs])

    fold = (lambda ot: ot, lambda row: row)

    def score(h, k_ref, idx, cap=None):
        def fn():
            st = lax.dot_general(k_ref[idx], qstk_ref[h], _NT, preferred_element_type=F32)
            return st if cap is None else jnp.minimum(st, cap)
        return fn

    def tile(ref, idx):
        return lambda: ref[idx]

    n_full = n_ch - is_real.astype(I32)
    row_diag = pl.ds(pl.multiple_of(n_full * TK, TK), TK)
    mcap = _meta_bias(is_real)
    q_pos = i * TQ + lax.broadcasted_iota(I32, (TK, 2 * TQ), 1) % TQ
    k_pos = n_full * TK + lax.broadcasted_iota(I32, (TK, 2 * TQ), 0)
    cap = jnp.where(jnp.where(k_pos <= q_pos, is_real.astype(I32), 0) > 0, BIG, NEG)
    _attend([[score(h, km_ref, (slice(None), hs), mcap), score(h, kb_ref, (row_diag, hs), cap)]
             for h, hs in enumerate(heads)],
            [[tile(vmt_ref, (0, hs, slice(0, META_KEYS))), tile(vbt_ref, (n_full, hs, slice(None)))] for hs in heads],
            m_ref, l_ref, acc_ref, fold, first=True)

    def update(chunks):
        rows = [pl.ds(pl.multiple_of(c * TK, TK), TK) for c in chunks]
        _attend([[score(h, kb_ref, (row, hs)) for row in rows] for h, hs in enumerate(heads)],
                [[tile(vbt_ref, (c, hs, slice(None))) for c in chunks] for hs in heads],
                m_ref, l_ref, acc_ref, fold, first=False)

    def pair_body(j, carry):
        update([2 * j, 2 * j + 1])
        return carry

    lax.fori_loop(0, n_full // 2, pair_body, 0)

    @pl.when(n_full % 2 == 1)
    def _():
        update([n_full - 1])

    for h, hs in enumerate(heads):
        on = acc_ref[h] * (1.0 / l_ref[h])
        o = on[:, :TQ] - lam * on[:, TQ:]
        y = o * lax.rsqrt(jnp.mean(o * o, axis=0, keepdims=True) + EPS) * g_ref[...] * (1.0 - lam_init)
        o_ref[:, hs] = y.T.astype(BF16)


def _diff(proj, vbt, lam_params, subln_g, batch, seq, lam_init):
    n = proj.shape[0]
    nq_seq = seq // TQ
    nch = seq // TK
    bidx = lambda g: jnp.minimum(g // nq_seq, batch - 1)
    meta_k = batch * seq // META_KEYS
    meta_tile = batch * nch
    return pl.pallas_call(
        functools.partial(_diff_kernel, seq=seq, lam_init=lam_init),
        grid=(n // TQ,),
        in_specs=[
            _resident((4, 64), lambda g: (0, 0)),
            _resident((128, 1), lambda g: (0, 0)),
            pl.BlockSpec((TQ, HEAD_W), lambda g: (g, 3)),
            pl.BlockSpec((seq, HEAD_W), lambda g: (bidx(g), 4)),
            pl.BlockSpec((nch, HEAD_W, TK), lambda g: (bidx(g), 0, 0)),
            pl.BlockSpec((META_KEYS, HEAD_W), lambda g: (meta_k, 4)),
            pl.BlockSpec((1, HEAD_W, TK), lambda g: (meta_tile, 0, 0)),
        ],
        out_specs=pl.BlockSpec((TQ, HEAD_W), lambda g: (g, 0)),
        out_shape=jax.ShapeDtypeStruct((n, HEAD_W), BF16),
        scratch_shapes=[
            pltpu.VMEM((4, 2 * TQ, 128), BF16),
            pltpu.VMEM((4, 1, 2 * TQ), F32),
            pltpu.VMEM((4, 1, 2 * TQ), F32),
            pltpu.VMEM((4, 128, 2 * TQ), F32),
        ],
        compiler_params=_params(),
        name="diff_attn",
    )(lam_params, subln_g.reshape(128, 1), proj, proj, vbt, proj, vbt)


def _out_proj_kernel(h_ref, oa_ref, ob_ref, wa_ref, wb_ref, o_ref):
    o_ref[...] = (h_ref[...]
                  + jnp.dot(oa_ref[...], wa_ref[...], preferred_element_type=F32)
                  + jnp.dot(ob_ref[...], wb_ref[...], preferred_element_type=F32))


def _out_proj(h, oa, ob, w_out):
    n = h.shape[0]
    return pl.pallas_call(
        _out_proj_kernel,
        grid=(n // TM,),
        in_specs=[
            pl.BlockSpec((TM, D), lambda i: (i, 0)),
            pl.BlockSpec((TM, HEAD_W), lambda i: (i, 0)),
            pl.BlockSpec((TM, HEAD_W), lambda i: (i, 0)),
            _resident((HEAD_W, D), lambda i: (0, 0)),
            _resident((HEAD_W, D), lambda i: (1, 0)),
        ],
        out_specs=pl.BlockSpec((TM, D), lambda i: (i, 0)),
        out_shape=jax.ShapeDtypeStruct((n, D), F32),
        compiler_params=_params(),
        name="attn_out_proj",
    )(h, oa, ob, w_out, w_out)


def _rglru_kernel(x_ref, g_ref, win_ref, cw_ref, cb_ref, gw_ref, gb_ref, lam_ref, wout_ref, o_ref,
                  xbuf, a_s, u_s, h_s, p_s, hstate, mstate, mhist, *, tiles_per_seq):
    g = pl.program_id(0)
    x = x_ref[...]
    xn = _rms(x, g_ref[...]).astype(BF16)
    yx = jnp.dot(xn, win_ref[...], preferred_element_type=F32)
    yb = yx[:, :D]
    y_br = 0.5 * yb * (1.0 + jnp.tanh(math.sqrt(2.0 / math.pi) * (yb + 0.044715 * (yb * yb * yb))))

    @pl.when(g == 0)
    def _():
        xbuf[0:8, :] = jnp.zeros((8, D), F32)
        hstate[...] = jnp.zeros(hstate.shape, F32)

    @pl.when(jnp.logical_and(g >= 1, (g - 1) % tiles_per_seq == 0))
    def _():
        xbuf[0:8, :] = mhist[...]
        hstate[...] = mstate[...]

    xbuf[8:, :] = yx[:, D:]
    xc = cb_ref[...] + cw_ref[0:1, :] * xbuf[5:5 + TM, :]
    for j in range(1, CONV_W):
        xc = xc + cw_ref[j:j + 1, :] * xbuf[5 + j:5 + j + TM, :]

    gx, ga = [], []
    for nb in range(LRU_BLOCKS):
        xb = xc[:, nb * LRU_BLOCK_W:(nb + 1) * LRU_BLOCK_W].astype(BF16)
        gx.append(jnp.dot(xb, gw_ref[0, nb], preferred_element_type=F32))
        ga.append(jnp.dot(xb, gw_ref[1, nb], preferred_element_type=F32))
    gate_x = _sigmoid(jnp.concatenate(gx, axis=1) + gb_ref[0:1, :])
    gate_a = _sigmoid(jnp.concatenate(ga, axis=1) + gb_ref[1:2, :])
    lam = lam_ref[...]
    log_sig = jnp.minimum(lam, 0.0) - jnp.log(1.0 + jnp.exp(-jnp.abs(lam)))
    log_a = RG_C * gate_a * log_sig
    a = jnp.exp(log_a)
    a_s[...] = a
    u_s[...] = jnp.sqrt(-jnp.tanh(log_a) * (a * a + 1.0)) * (gate_x * xc)

    seg_rows = TM // SCAN_SEGS

    def step(r, carry):
        hs, ps = carry
        new_h, new_p = [], []
        for s in range(SCAN_SEGS):
            row = pl.ds(s * seg_rows + r, 1)
            a_r = a_s[row, :]
            h_r = a_r * hs[s] + u_s[row, :]
            h_s[row, :] = h_r
            new_h.append(h_r)
            if s > 0:
                p_r = a_r * ps[s - 1]
                p_s[row, :] = p_r
                new_p.append(p_r)
        return tuple(new_h), tuple(new_p)

    zero = jnp.zeros((1, D), F32)
    hs, _ = lax.fori_loop(0, seg_rows, step,
                          ((hstate[...],) + (zero,) * (SCAN_SEGS - 1), (zero + 1.0,) * (SCAN_SEGS - 1)), unroll=4)
    h_in = hs[0]
    for s in range(1, SCAN_SEGS):
        seg = slice(s * seg_rows, (s + 1) * seg_rows)
        h_seg = h_s[seg, :] + p_s[seg, :] * h_in
        h_s[seg, :] = h_seg
        h_in = h_seg[seg_rows - 1:seg_rows, :]
    hstate[...] = h_in

    @pl.when(g == 0)
    def _():
        mstate[...] = h_s[N_META - 1:N_META, :]
        mhist[...] = xbuf[N_META:N_META + 8, :]

    xbuf[0:8, :] = xbuf[TM:TM + 8, :]
    o_ref[...] = x + jnp.dot((h_s[...] * y_br).astype(BF16), wout_ref[...], preferred_element_type=F32)


def _rglru(h, g, w_in, conv_w, conv_b, gate_w, gate_b, lru_lambda, w_out, tiles_per_seq):
    n = h.shape[0]
    nt = n // TM
    tile = lambda s: (jnp.where(s == 0, nt - 1, s - 1), 0)
    const2 = lambda s: (0, 0)
    return pl.pallas_call(
        functools.partial(_rglru_kernel, tiles_per_seq=tiles_per_seq),
        grid=(nt,),
        in_specs=[
            pl.BlockSpec((TM, D), tile),
            _resident((1, D), const2),
            _resident((D, 2 * D), const2),
            _resident((CONV_W, D), const2),
            _resident((1, D), const2),
            _resident((2, LRU_BLOCKS, LRU_BLOCK_W, LRU_BLOCK_W), lambda s: (0, 0, 0, 0)),
            _resident((2, D), const2),
            _resident((1, D), const2),
            _resident((D, D), const2),
        ],
        out_specs=pl.BlockSpec((TM, D), tile),
        out_shape=jax.ShapeDtypeStruct((n, D), F32),
        scratch_shapes=[
            pltpu.VMEM((TM + 8, D), F32),
            pltpu.VMEM((TM, D), F32),
            pltpu.VMEM((TM, D), F32),
            pltpu.VMEM((TM, D), F32),
            pltpu.VMEM((TM, D), F32),
            pltpu.VMEM((1, D), F32),
            pltpu.VMEM((1, D), F32),
            pltpu.VMEM((8, D), F32),
        ],
        compiler_params=_params(),
        name="rglru",
    )(h, g.reshape(1, D), w_in, conv_w, conv_b.reshape(1, D), gate_w, gate_b, lru_lambda.reshape(1, D), w_out)


def _split_attn_w_in(w):
    a = HEAD_W
    o_ki = 4 * a
    o_wi = o_ki + IDX_DIM
    o_qb = o_wi + IDX_HEADS
    qa, ka, va, qi = w[:, 0:a], w[:, a:2 * a], w[:, 2 * a:3 * a], w[:, 3 * a:4 * a]
    ki, wi = w[:, o_ki:o_wi], w[:, o_wi:o_qb]
    qb, kb, vb = w[:, o_qb:o_qb + a], w[:, o_qb + a:o_qb + 2 * a], w[:, o_qb + 2 * a:o_qb + 3 * a]
    w_main = jnp.concatenate([qa, ka, qi, qb, kb, va, vb], axis=1).astype(BF16)
    w_wt = jnp.concatenate([wi.T, jnp.zeros((16 - IDX_HEADS, D), w.dtype)], axis=0).astype(BF16)
    return w_main, ki.astype(BF16), w_wt


def kernel(x, meta_tokens, norm_g, ffn_w_gu, ffn_w_down, attn_w_in, idx_k_ln_g, idx_k_ln_b, diff_lambda, diff_subln_g, attn_w_out, rec_w_in, rec_conv_w, rec_conv_b, rec_gate_w, rec_gate_b, rec_lambda, rec_w_out, final_norm_g):
    batch, seq, _ = x.shape
    depth = norm_g.shape[0]
    n_real = batch * seq
    tail = jnp.concatenate([meta_tokens.astype(x.dtype), jnp.zeros((TM - N_META, D), x.dtype)], axis=0)
    w_gu = ffn_w_gu.astype(BF16)
    w_down = ffn_w_down.astype(BF16)
    h = x.reshape(n_real, D)
    for i in range(depth):
        j = i // 2
        h = _ffn(h, norm_g[i, 0], w_gu, w_down, i, 0, tail=tail if i == 0 else None)
        if i % 2 == 0:
            lam_init = 0.8 - 0.6 * math.exp(-0.3 * i)
            w_main, w_ki, w_wt = _split_attn_w_in(attn_w_in[j])
            proj, ki, wit, vat, vbt = _attn_proj(h, norm_g[i, 1], w_main, w_ki, w_wt, idx_k_ln_g[j], idx_k_ln_b[j])
            oa = _dsa(proj, ki, wit, vat, batch, seq)
            ob = _diff(proj, vbt, diff_lambda[j], diff_subln_g[j], batch, seq, lam_init)
            h = _out_proj(h, oa, ob, attn_w_out[j].astype(BF16))
        else:
            h = _rglru(h, norm_g[i, 1], rec_w_in[j].astype(BF16), rec_conv_w[j], rec_conv_b[j],
                       rec_gate_w[j].astype(BF16), rec_gate_b[j], rec_lambda[j], rec_w_out[j].astype(BF16),
                       seq // TM)
        if i == depth - 1:
            h = _ffn(h, norm_g[i, 2], w_gu, w_down, i, 1, final_g=final_norm_g, n_rows=n_real)
        else:
            h = _ffn(h, norm_g[i, 2], w_gu, w_down, i, 1)
    return h.reshape(batch, seq, D)
```

```python
import functools
import math

import jax
import jax.numpy as jnp
from jax import lax
from jax.experimental import pallas as pl
from jax.experimental.pallas import tpu as pltpu

F32 = jnp.float32
BF16 = jnp.bfloat16
I32 = jnp.int32

D = 1024
N_META = 16
D_FF = 2816
EPS = 1e-6
HEAD_W = 512
IDX_DIM = 64
IDX_HEADS = 8
TOPK_MAX = 256
LRU_BLOCKS = 4
LRU_BLOCK_W = D // LRU_BLOCKS
CONV_W = 4
RG_C = 8.0

TM = 512
TQ = 256
META_KEYS = 128
TK = 512
SCAN_SEGS = 4
MXU_TILE = 256
FF_CHUNKS = ((0, 6 * MXU_TILE), (6 * MXU_TILE, D_FF))
NEG = -1e30
BIG = 1e30
QK_SCALE = IDX_DIM ** -0.5 * math.log2(math.e)
INT_MIN = -2 ** 31
VMEM_LIMIT = 56 * 1024 * 1024

_NT = (((1,), (1,)), ((), ()))


def _params():
    return pltpu.CompilerParams(dimension_semantics=("arbitrary",), vmem_limit_bytes=VMEM_LIMIT)


def _resident(shape, index_map):
    return pl.BlockSpec(shape, index_map, pipeline_mode=pl.Buffered(1))


def _rms(x, g):
    return x * lax.rsqrt(jnp.mean(x * x, axis=-1, keepdims=True) + EPS) * g


def _sigmoid(x):
    return 0.5 * jnp.tanh(0.5 * x) + 0.5


def _ffn_kernel(*refs, final, n_real_tiles, mixer):
    refs = list(refs)
    o_ref = refs.pop()
    x = refs.pop(0)[...]
    if n_real_tiles is not None:
        x = jnp.where(pl.program_id(0) < n_real_tiles, x, refs.pop(0)[...])
    if mixer:
        oa_ref, ob_ref, wa_ref, wb_ref = refs[:4]
        del refs[:4]
        x = (x + jnp.dot(oa_ref[...], wa_ref[...], preferred_element_type=F32)
             + jnp.dot(ob_ref[...], wb_ref[...], preferred_element_type=F32))
    g_ref, wg_ref, wu_ref, wd_ref = refs[:4]
    xn = _rms(x, g_ref[...]).astype(BF16)
    acc = None
    for lo, hi in FF_CHUNKS:
        sl = slice(lo, hi)
        g = jnp.dot(xn, wg_ref[:, sl], preferred_element_type=F32)
        u = jnp.dot(xn, wu_ref[:, sl], preferred_element_type=F32)
        a = (g * _sigmoid(g) * u).astype(BF16)
        part = jnp.dot(a, wd_ref[sl, :], preferred_element_type=F32)
        acc = part if acc is None else acc + part
    y = x + 0.5 * acc
    if final:
        y = _rms(y, refs[4][...])
    o_ref[...] = y


def _ffn(h, g, w_gu, w_down, layer, half, tail=None, mixer=None, final_g=None, n_rows=None):
    n_real_tiles = None if tail is None else h.shape[0] // TM
    if n_rows is None:
        n_rows = h.shape[0] + (0 if tail is None else TM)
    row_tile = lambda i: (i, 0)
    const = lambda i: (0, 0)
    if tail is None:
        in_specs, args = [pl.BlockSpec((TM, D), row_tile)], [h]
    else:
        in_specs = [pl.BlockSpec((TM, D), lambda i: (jnp.minimum(i, n_real_tiles - 1), 0)), _resident((TM, D), const)]
        args = [h, tail]
    if mixer is not None:
        oa, ob, w_out = mixer
        in_specs += [pl.BlockSpec((TM, HEAD_W), row_tile), pl.BlockSpec((TM, HEAD_W), row_tile),
                     _resident((HEAD_W, D), const), _resident((HEAD_W, D), lambda i: (1, 0))]
        args += [oa, ob, w_out, w_out]
    in_specs += [
        _resident((1, D), const),
        _resident((None, None, D, D_FF), lambda i: (layer, half, 0, 0)),
        _resident((None, None, D, D_FF), lambda i: (layer, half, 0, 1)),
        _resident((None, None, D_FF, D), lambda i: (layer, half, 0, 0)),
    ]
    args += [g.reshape(1, D), w_gu, w_gu, w_down]
    if final_g is not None:
        in_specs.append(_resident((1, D), const))
        args.append(final_g.reshape(1, D))
    return pl.pallas_call(
        functools.partial(_ffn_kernel, final=final_g is not None, n_real_tiles=n_real_tiles,
                          mixer=mixer is not None),
        grid=(n_rows // TM,),
        in_specs=in_specs,
        out_specs=pl.BlockSpec((TM, D), row_tile),
        out_shape=jax.ShapeDtypeStruct((n_rows, D), F32),
        compiler_params=_params(),
        name="ffn",
    )(*args)


N_PROJ = 5 * HEAD_W


def _attn_proj_kernel(x_ref, g_ref, w_ref, wki_ref, wwt_ref, lng_ref, lnb_ref,
                      proj_ref, ki_ref, wit_ref, vat_ref, vbt_ref):
    xn = _rms(x_ref[...], g_ref[...]).astype(BF16)
    y = jnp.dot(xn, w_ref[...], preferred_element_type=F32)
    scale = QK_SCALE
    proj_ref[:, 0:HEAD_W] = (y[:, 0:HEAD_W] * scale).astype(BF16)
    proj_ref[:, HEAD_W:3 * HEAD_W] = y[:, HEAD_W:3 * HEAD_W].astype(BF16)
    proj_ref[:, 3 * HEAD_W:4 * HEAD_W] = (y[:, 3 * HEAD_W:4 * HEAD_W] * scale).astype(BF16)
    proj_ref[:, 4 * HEAD_W:5 * HEAD_W] = y[:, 4 * HEAD_W:5 * HEAD_W].astype(BF16)
    vat_ref[0] = y[:, 5 * HEAD_W:6 * HEAD_W].T.astype(BF16)
    vbt_ref[0] = y[:, 6 * HEAD_W:7 * HEAD_W].T.astype(BF16)
    yk = jnp.dot(xn, wki_ref[...], preferred_element_type=F32)
    mu = jnp.mean(yk, axis=-1, keepdims=True)
    yc = yk - mu
    ln = yc * lax.rsqrt(jnp.mean(yc * yc, axis=-1, keepdims=True) + EPS)
    ki_ref[...] = (ln * lng_ref[...] + lnb_ref[...]).astype(BF16)
    wit = lax.dot_general(wwt_ref[...], xn, _NT, preferred_element_type=F32)
    wit_ref[...] = wit * (IDX_HEADS ** -0.5 * IDX_DIM ** -0.5)


def _attn_proj(h, g, w_main, w_ki, w_wt, ln_g, ln_b):
    n = h.shape[0]
    nt = n // TM
    const = lambda i: (0, 0)
    return pl.pallas_call(
        _attn_proj_kernel,
        grid=(nt,),
        in_specs=[
            pl.BlockSpec((TM, D), lambda i: (i, 0)),
            _resident((1, D), const),
            _resident((D, 7 * HEAD_W), const),
            _resident((D, IDX_DIM), const),
            _resident((16, D), const),
            _resident((1, IDX_DIM), const),
            _resident((1, IDX_DIM), const),
        ],
        out_specs=[
            pl.BlockSpec((TM, N_PROJ), lambda i: (i, 0)),
            pl.BlockSpec((TM, IDX_DIM), lambda i: (i, 0)),
            pl.BlockSpec((16, TM), lambda i: (0, i)),
            pl.BlockSpec((1, HEAD_W, TM), lambda i: (i, 0, 0)),
            pl.BlockSpec((1, HEAD_W, TM), lambda i: (i, 0, 0)),
        ],
        out_shape=[
            jax.ShapeDtypeStruct((n, N_PROJ), BF16),
            jax.ShapeDtypeStruct((n, IDX_DIM), BF16),
            jax.ShapeDtypeStruct((16, n), F32),
            jax.ShapeDtypeStruct((nt, HEAD_W, TM), BF16),
            jax.ShapeDtypeStruct((nt, HEAD_W, TM), BF16),
        ],
        compiler_params=_params(),
        name="attn_proj",
    )(h, g.reshape(1, D), w_main, w_ki, w_wt, ln_g.reshape(1, IDX_DIM), ln_b.reshape(1, IDX_DIM))


def _step_geometry(g, seq):
    nq_seq = seq // TQ
    n_real = pl.num_programs(0) - TM // TQ
    is_real = g < n_real
    i = g % nq_seq
    n_ch = jnp.where(is_real, (i * TQ + TQ + TK - 1) // TK, 0)
    return is_real, i, n_ch


def _stack_pair(q_pair):
    lane = lax.broadcasted_iota(I32, (TQ, 128), 1)
    qf = q_pair.astype(F32)
    top = jnp.where(lane < 64, qf, 0.0).astype(BF16)
    bot = jnp.where(lane >= 64, qf, 0.0).astype(BF16)
    return jnp.concatenate([top, bot], axis=0)


def _meta_bias(is_real):
    j = lax.broadcasted_iota(I32, (META_KEYS, 2 * TQ), 0)
    t = lax.broadcasted_iota(I32, (META_KEYS, 2 * TQ), 1) % TQ
    real = is_real.astype(I32)
    lim = jnp.minimum(real * N_META + (1 - real) * (t + 1), N_META)
    return jnp.where(j < lim, BIG, NEG)


def _bcast_halves(row, n_sub):
    return jnp.concatenate([jnp.broadcast_to(row[:, :TQ], (n_sub, TQ)),
                            jnp.broadcast_to(row[:, TQ:], (n_sub, TQ))], axis=0)


SCORE_LOOKAHEAD = 2
MAX_LAG = 64.0


def _softmax_pv(segments, v_tiles, base):
    issued, out = {}, []
    for h in range(SCORE_LOOKAHEAD):
        issued[h] = [fn() for fn in segments[h]]
    for h in range(4):
        if h + SCORE_LOOKAHEAD < 4:
            issued[h + SCORE_LOOKAHEAD] = [fn() for fn in segments[h + SCORE_LOOKAHEAD]]
        sts = issued.pop(h)
        m_loc = jnp.max(sts[0], axis=0, keepdims=True)
        for st in sts[1:]:
            m_loc = jnp.maximum(m_loc, jnp.max(st, axis=0, keepdims=True))
        m_use = base(h, m_loc)
        l_new, pv = None, None
        for st, vt in zip(sts, v_tiles[h]):
            pc = jnp.exp2(st - m_use)
            s = jnp.sum(pc, axis=0, keepdims=True)
            d = jnp.dot(vt(), pc.astype(BF16), preferred_element_type=F32)
            l_new = s if l_new is None else l_new + s
            pv = d if pv is None else pv + d
        out.append((m_loc, l_new, pv))
    return out


def _attend(segments, v_tiles, m_ref, l_ref, acc_ref, fold, first):
    fold_pv, fold_row = fold
    if first:
        for h, (m_loc, l_new, pv) in enumerate(_softmax_pv(segments, v_tiles, lambda h, m_loc: m_loc)):
            m_ref[h] = m_loc
            l_ref[h] = l_new
            acc_ref[h] = fold_pv(pv)
        return

    def commit(parts, rebased):
        for h, (m_loc, l_new, pv) in enumerate(parts):
            m_new = jnp.maximum(m_ref[h], m_loc)
            alpha = jnp.exp2(m_ref[h] - m_new)
            if rebased:
                l_ref[h] = alpha * l_ref[h] + l_new
                acc_ref[h] = acc_ref[h] * fold_row(alpha) + fold_pv(pv)
            else:
                l_ref[h] = alpha * (l_ref[h] + l_new)
                acc_ref[h] = (acc_ref[h] + fold_pv(pv)) * fold_row(alpha)
            m_ref[h] = m_new

    parts = _softmax_pv(segments, v_tiles, lambda h, m_loc: m_ref[h])
    rise = parts[0][0] - m_ref[0]
    for h in range(1, 4):
        rise = jnp.maximum(rise, parts[h][0] - m_ref[h])
    safe = jnp.max(rise) <= MAX_LAG

    @pl.when(safe)
    def _():
        commit(parts, rebased=False)

    @pl.when(jnp.logical_not(safe))
    def _():
        commit(_softmax_pv(segments, v_tiles, lambda h, m_loc: jnp.maximum(m_ref[h], m_loc)), rebased=True)


GROUP_KEYS = 256
GROUPS_PER_CHUNK = TK // GROUP_KEYS


def _bit_transpose32(words):
    a = list(words)
    j, mask = 16, 0x0000FFFF
    while j:
        k = 0
        while k < 32:
            t = (a[k] ^ lax.shift_right_logical(a[k + j], jnp.int32(j))) & mask
            a[k] = a[k] ^ t
            a[k + j] = a[k + j] ^ lax.shift_left(t, jnp.int32(j))
            k = (k + j + 1) & ~j
        j >>= 1
        mask ^= mask << j
    return a


def _dsa_kernel(qa_ref, qi_ref, wit_ref, ki_ref, ka_ref, vat_ref, km_ref, vmt_ref, o_ref,
                qall_ref, qstk_ref, key_ref, plane_ref, bias_ref, tri_ref, m_ref, l_ref, acc_ref, *, seq, top_k):
    g = pl.program_id(0)

    @pl.when(g == 0)
    def _():
        plane_ref[...] = jnp.zeros(plane_ref.shape, I32)
        r = lax.broadcasted_iota(I32, (TK, TK), 0)
        c_ = lax.broadcasted_iota(I32, (TK, TK), 1)
        tri_ref[...] = jnp.where(c_ <= r, 1.0, 0.0).astype(BF16)

    is_real, i, n_ch = _step_geometry(g, seq)
    q_pos = i * TQ + lax.broadcasted_iota(I32, (TK, TQ), 1)
    k_off = lax.broadcasted_iota(I32, (TK, TQ), 0)

    qi = qi_ref[...]
    for h in range(IDX_HEADS):
        qall_ref[h * TQ:(h + 1) * TQ, :] = qi[:, h * IDX_DIM:(h + 1) * IDX_DIM]
    for p in range(4):
        qstk_ref[p] = _stack_pair(qa_ref[:, p * 128:(p + 1) * 128])

    wt = wit_ref[...]

    def score_body(c, carry):
        kc = ki_ref[pl.ds(pl.multiple_of(c * TK, TK), TK), :]
        logits = lax.dot_general(kc, qall_ref[...], _NT, preferred_element_type=F32)
        sc = jnp.zeros((TK, TQ), F32)
        for h in range(IDX_HEADS):
            sc = sc + wt[h:h + 1, :] * jnp.maximum(logits[:, h * TQ:(h + 1) * TQ], 0.0)
        sc = jnp.where(sc == 0.0, 0.0, sc)
        bits = pltpu.bitcast(sc, I32)
        key = bits ^ (lax.shift_right_arithmetic(bits, 31) & 0x7FFFFFFF)
        key = jnp.where(c * TK + k_off <= q_pos, key, INT_MIN)
        key_ref[c] = key
        u = key ^ INT_MIN
        for gi in range(GROUPS_PER_CHUNK):
            base = gi * GROUP_KEYS
            planes = _bit_transpose32([u[base + w * 8:base + (w + 1) * 8, :] for w in range(32)])
            for b in range(32):
                plane_ref[c * GROUPS_PER_CHUNK + gi, b] = planes[b]
        return carry

    lax.fori_loop(0, n_ch, score_body, 0)

    n_grp = n_ch * GROUPS_PER_CHUNK
    n_groups = plane_ref.shape[0]
    alive0 = tuple(jnp.full((8, TQ), -1, I32) * (gq < n_grp).astype(I32) for gq in range(n_groups))

    def bit_body(b, carry):
        alive, k_rem, thr_u = carry
        ones = [a & plane_ref[gq, b] for gq, a in enumerate(alive)]
        cnt8 = lax.population_count(ones[0])
        for o in ones[1:]:
            cnt8 = cnt8 + lax.population_count(o)
        cnt = jnp.sum(cnt8, axis=0, keepdims=True)
        take = cnt >= k_rem
        alive = tuple(jnp.where(take, o, a ^ o) for a, o in zip(alive, ones))
        k_rem = jnp.where(take, k_rem, k_rem - cnt)
        thr_u = thr_u | jnp.where(take, lax.shift_left(jnp.int32(1), 31 - b), 0)
        return alive, k_rem, thr_u

    alive, k_rem, thr_u = lax.fori_loop(
        0, 32, bit_body, (alive0, jnp.full((1, TQ), top_k, I32), jnp.zeros((1, TQ), I32)))
    thr = thr_u ^ INT_MIN
    eq8 = lax.population_count(alive[0])
    for a in alive[1:]:
        eq8 = eq8 + lax.population_count(a)
    cnt_eq = jnp.sum(eq8, axis=0, keepdims=True)
    excess = jnp.where(thr_u != 0, cnt_eq - k_rem, 0)
    has_excess = jnp.max(excess) > 0

    @pl.when(jnp.logical_not(has_excess))
    def _():
        thr_eff = jnp.maximum(thr, INT_MIN + 1)

        def cap_body(c, carry):
            bias_ref[c] = jnp.where(key_ref[c] >= thr_eff, BIG, NEG)
            return carry

        lax.fori_loop(0, n_ch, cap_body, 0)

    @pl.when(has_excess)
    def _():
        need = k_rem.astype(F32)

        def cap_body(c, run):
            k = key_ref[c]
            eq = k == thr
            pre = jnp.dot(tri_ref[...], jnp.where(eq, 1.0, 0.0).astype(BF16), preferred_element_type=F32)
            rank = run + pre
            b = jnp.where(k > thr, BIG, jnp.where(eq, jnp.where(rank <= need, BIG, NEG), NEG))
            bias_ref[c] = jnp.where(k == INT_MIN, NEG, b)
            return run + pre[TK - 1:TK, :]

        lax.fori_loop(0, n_ch, cap_body, jnp.zeros((1, TQ), F32))

    fold = (lambda ot: jnp.concatenate([ot[:64, :TQ], ot[64:, TQ:]], axis=0), lambda row: _bcast_halves(row, 64))
    pairs = [slice(p * 128, (p + 1) * 128) for p in range(4)]

    def score(p, k_ref, idx, cap):
        return lambda: jnp.minimum(lax.dot_general(k_ref[idx], qstk_ref[p], _NT, preferred_element_type=F32), cap)

    def tile(ref, idx):
        return lambda: ref[idx]

    c_last = jnp.maximum(n_ch - 1, 0)
    row_last = pl.ds(pl.multiple_of(c_last * TK, TK), TK)
    mcap = _meta_bias(is_real)
    cap = jnp.where(is_real, bias_ref[c_last], NEG)
    cap2 = jnp.concatenate([cap, cap], axis=1)
    _attend([[score(p, km_ref, (slice(None), ps), mcap), score(p, ka_ref, (row_last, ps), cap2)]
             for p, ps in enumerate(pairs)],
            [[tile(vmt_ref, (0, ps, slice(0, META_KEYS))), tile(vat_ref, (c_last, ps, slice(None)))] for ps in pairs],
            m_ref, l_ref, acc_ref, fold, first=True)

    def update(chunks):
        rows = [pl.ds(pl.multiple_of(c * TK, TK), TK) for c in chunks]
        caps = [jnp.concatenate([bias_ref[c]] * 2, axis=1) for c in chunks]
        _attend([[score(p, ka_ref, (row, ps), cap2) for row, cap2 in zip(rows, caps)] for p, ps in enumerate(pairs)],
                [[tile(vat_ref, (c, ps, slice(None))) for c in chunks] for ps in pairs],
                m_ref, l_ref, acc_ref, fold, first=False)

    def pair_body(j, carry):
        update([2 * j, 2 * j + 1])
        return carry

    lax.fori_loop(0, c_last // 2, pair_body, 0)

    @pl.when(c_last % 2 == 1)
    def _():
        update([c_last - 1])

    for p, ps in enumerate(pairs):
        o_ref[:, ps] = (acc_ref[p] * _bcast_halves(1.0 / l_ref[p], 64)).T.astype(BF16)


def _dsa(proj, ki, wit, vat, batch, seq):
    n = proj.shape[0]
    nq_seq = seq // TQ
    n_real_q = batch * nq_seq
    nch = seq // TK
    top_k = min(TOPK_MAX, seq // 4)
    bidx = lambda g: jnp.minimum(g // nq_seq, batch - 1)
    meta_k = batch * seq // META_KEYS
    meta_tile = batch * nch
    return pl.pallas_call(
        functools.partial(_dsa_kernel, seq=seq, top_k=top_k),
        grid=(n // TQ,),
        in_specs=[
            pl.BlockSpec((TQ, HEAD_W), lambda g: (g, 0)),
            pl.BlockSpec((TQ, HEAD_W), lambda g: (g, 2)),
            pl.BlockSpec((16, TQ), lambda g: (0, g)),
            pl.BlockSpec((seq, IDX_DIM), lambda g: (bidx(g), 0)),
            pl.BlockSpec((seq, HEAD_W), lambda g: (bidx(g), 1)),
            pl.BlockSpec((nch, HEAD_W, TK), lambda g: (bidx(g), 0, 0)),
            pl.BlockSpec((META_KEYS, HEAD_W), lambda g: (meta_k, 1)),
            pl.BlockSpec((1, HEAD_W, TK), lambda g: (meta_tile, 0, 0)),
        ],
        out_specs=pl.BlockSpec((TQ, HEAD_W), lambda g: (g, 0)),
        out_shape=jax.ShapeDtypeStruct((n, HEAD_W), BF16),
        scratch_shapes=[
            pltpu.VMEM((IDX_HEADS * TQ, IDX_DIM), BF16),
            pltpu.VMEM((4, 2 * TQ, 128), BF16),
            pltpu.VMEM((nch, TK, TQ), I32),
            pltpu.VMEM((nch * GROUPS_PER_CHUNK, 32, 8, TQ), I32),
            pltpu.VMEM((nch, TK, TQ), F32),
            pltpu.VMEM((TK, TK), BF16),
            pltpu.VMEM((4, 1, 2 * TQ), F32),
            pltpu.VMEM((4, 1, 2 * TQ), F32),
            pltpu.VMEM((4, 128, TQ), F32),
        ],
        compiler_params=_params(),
        name="dsa",
    )(proj, proj, wit, ki, proj, vat, proj, vat)


def _diff_kernel(lam_ref, g_ref, qb_ref, kb_ref, vbt_ref, km_ref, vmt_ref, o_ref,
                 qstk_ref, m_ref, l_ref, acc_ref, *, seq, lam_init):
    g = pl.program_id(0)
    is_real, i, n_ch = _step_geometry(g, seq)
    lp = lam_ref[...]
    lam = (jnp.exp(jnp.sum(lp[0:1] * lp[1:2], axis=1, keepdims=True))
           - jnp.exp(jnp.sum(lp[2:3] * lp[3:4], axis=1, keepdims=True)) + lam_init)
    heads = [slice(h * 128, (h + 1) * 128) for h in range(4)]
    for h, hs in enumerate(heads):
        qstk_ref[h] = _stack_pair(qb_ref[:, hs])

    fold = (lambda ot: ot, lambda row: row)

    def score(h, k_ref, idx, cap=None):
        def fn():
            st = lax.dot_general(k_ref[idx], qstk_ref[h], _NT, preferred_element_type=F32)
            return st if cap is None else jnp.minimum(st, cap)
        return fn

    def tile(ref, idx):
        return lambda: ref[idx]

    n_full = n_ch - is_real.astype(I32)
    row_diag = pl.ds(pl.multiple_of(n_full * TK, TK), TK)
    mcap = _meta_bias(is_real)
    q_pos = i * TQ + lax.broadcasted_iota(I32, (TK, 2 * TQ), 1) % TQ
    k_pos = n_full * TK + lax.broadcasted_iota(I32, (TK, 2 * TQ), 0)
    cap = jnp.where(jnp.where(k_pos <= q_pos, is_real.astype(I32), 0) > 0, BIG, NEG)
    _attend([[score(h, km_ref, (slice(None), hs), mcap), score(h, kb_ref, (row_diag, hs), cap)]
             for h, hs in enumerate(heads)],
            [[tile(vmt_ref, (0, hs, slice(0, META_KEYS))), tile(vbt_ref, (n_full, hs, slice(None)))] for hs in heads],
            m_ref, l_ref, acc_ref, fold, first=True)

    def update(chunks):
        rows = [pl.ds(pl.multiple_of(c * TK, TK), TK) for c in chunks]
        _attend([[score(h, kb_ref, (row, hs)) for row in rows] for h, hs in enumerate(heads)],
                [[tile(vbt_ref, (c, hs, slice(None))) for c in chunks] for hs in heads],
                m_ref, l_ref, acc_ref, fold, first=False)

    def pair_body(j, carry):
        update([2 * j, 2 * j + 1])
        return carry

    lax.fori_loop(0, n_full // 2, pair_body, 0)

    @pl.when(n_full % 2 == 1)
    def _():
        update([n_full - 1])

    for h, hs in enumerate(heads):
        on = acc_ref[h] * (1.0 / l_ref[h])
        o = on[:, :TQ] - lam * on[:, TQ:]
        y = o * lax.rsqrt(jnp.mean(o * o, axis=0, keepdims=True) + EPS) * g_ref[...] * (1.0 - lam_init)
        o_ref[:, hs] = y.T.astype(BF16)


def _diff(proj, vbt, lam_params, subln_g, batch, seq, lam_init):
    n = proj.shape[0]
    nq_seq = seq // TQ
    nch = seq // TK
    bidx = lambda g: jnp.minimum(g // nq_seq, batch - 1)
    meta_k = batch * seq // META_KEYS
    meta_tile = batch * nch
    return pl.pallas_call(
        functools.partial(_diff_kernel, seq=seq, lam_init=lam_init),
        grid=(n // TQ,),
        in_specs=[
            _resident((4, 64), lambda g: (0, 0)),
            _resident((128, 1), lambda g: (0, 0)),
            pl.BlockSpec((TQ, HEAD_W), lambda g: (g, 3)),
            pl.BlockSpec((seq, HEAD_W), lambda g: (bidx(g), 4)),
            pl.BlockSpec((nch, HEAD_W, TK), lambda g: (bidx(g), 0, 0)),
            pl.BlockSpec((META_KEYS, HEAD_W), lambda g: (meta_k, 4)),
            pl.BlockSpec((1, HEAD_W, TK), lambda g: (meta_tile, 0, 0)),
        ],
        out_specs=pl.BlockSpec((TQ, HEAD_W), lambda g: (g, 0)),
        out_shape=jax.ShapeDtypeStruct((n, HEAD_W), BF16),
        scratch_shapes=[
            pltpu.VMEM((4, 2 * TQ, 128), BF16),
            pltpu.VMEM((4, 1, 2 * TQ), F32),
            pltpu.VMEM((4, 1, 2 * TQ), F32),
            pltpu.VMEM((4, 128, 2 * TQ), F32),
        ],
        compiler_params=_params(),
        name="diff_attn",
    )(lam_params, subln_g.reshape(128, 1), proj, proj, vbt, proj, vbt)


def _rglru_kernel(x_ref, g_ref, win_ref, cw_ref, cb_ref, gw_ref, gb_ref, lam_ref, wout_ref, o_ref,
                  xbuf, a_s, u_s, h_s, p_s, hstate, mstate, mhist, *, tiles_per_seq):
    g = pl.program_id(0)
    x = x_ref[...]
    xn = _rms(x, g_ref[...]).astype(BF16)
    yx = jnp.dot(xn, win_ref[...], preferred_element_type=F32)
    yb = yx[:, :D]
    y_br = 0.5 * yb * (1.0 + jnp.tanh(math.sqrt(2.0 / math.pi) * (yb + 0.044715 * (yb * yb * yb))))

    @pl.when(g == 0)
    def _():
        xbuf[0:8, :] = jnp.zeros((8, D), F32)
        hstate[...] = jnp.zeros(hstate.shape, F32)

    @pl.when(jnp.logical_and(g >= 1, (g - 1) % tiles_per_seq == 0))
    def _():
        xbuf[0:8, :] = mhist[...]
        hstate[...] = mstate[...]

    xbuf[8:, :] = yx[:, D:]
    xc = cb_ref[...] + cw_ref[0:1, :] * xbuf[5:5 + TM, :]
    for j in range(1, CONV_W):
        xc = xc + cw_ref[j:j + 1, :] * xbuf[5 + j:5 + j + TM, :]

    gx, ga = [], []
    for nb in range(LRU_BLOCKS):
        xb = xc[:, nb * LRU_BLOCK_W:(nb + 1) * LRU_BLOCK_W].astype(BF16)
        gx.append(jnp.dot(xb, gw_ref[0, nb], preferred_element_type=F32))
        ga.append(jnp.dot(xb, gw_ref[1, nb], preferred_element_type=F32))
    gate_x = _sigmoid(jnp.concatenate(gx, axis=1) + gb_ref[0:1, :])
    gate_a = _sigmoid(jnp.concatenate(ga, axis=1) + gb_ref[1:2, :])
    lam = lam_ref[...]
    log_sig = jnp.minimum(lam, 0.0) - jnp.log(1.0 + jnp.exp(-jnp.abs(lam)))
    log_a = RG_C * gate_a * log_sig
    a = jnp.exp(log_a)
    a_s[...] = a
    u_s[...] = jnp.sqrt(-jnp.tanh(log_a) * (a * a + 1.0)) * (gate_x * xc)

    seg_rows = TM // SCAN_SEGS

    def step(r, carry):
        hs, ps = carry
        new_h, new_p = [], []
        for s in range(SCAN_SEGS):
            row = pl.ds(s * seg_rows + r, 1)
            a_r = a_s[row, :]
            h_r = a_r * hs[s] + u_s[row, :]
            h_s[row, :] = h_r
            new_h.append(h_r)
            if s > 0:
                p_r = a_r * ps[s - 1]
                p_s[row, :] = p_r
                new_p.append(p_r)
        return tuple(new_h), tuple(new_p)

    zero = jnp.zeros((1, D), F32)
    hs, _ = lax.fori_loop(0, seg_rows, step,
                          ((hstate[...],) + (zero,) * (SCAN_SEGS - 1), (zero + 1.0,) * (SCAN_SEGS - 1)), unroll=4)
    h_in = hs[0]
    for s in range(1, SCAN_SEGS):
        seg = slice(s * seg_rows, (s + 1) * seg_rows)
        h_seg = h_s[seg, :] + p_s[seg, :] * h_in
        h_s[seg, :] = h_seg
        h_in = h_seg[seg_rows - 1:seg_rows, :]
    hstate[...] = h_in

    @pl.when(g == 0)
    def _():
        mstate[...] = h_s[N_META - 1:N_META, :]
        mhist[...] = xbuf[N_META:N_META + 8, :]

    xbuf[0:8, :] = xbuf[TM:TM + 8, :]
    o_ref[...] = x + jnp.dot((h_s[...] * y_br).astype(BF16), wout_ref[...], preferred_element_type=F32)


def _rglru(h, g, w_in, conv_w, conv_b, gate_w, gate_b, lru_lambda, w_out, tiles_per_seq):
    n = h.shape[0]
    nt = n // TM
    tile = lambda s: (jnp.where(s == 0, nt - 1, s - 1), 0)
    const2 = lambda s: (0, 0)
    return pl.pallas_call(
        functools.partial(_rglru_kernel, tiles_per_seq=tiles_per_seq),
        grid=(nt,),
        in_specs=[
            pl.BlockSpec((TM, D), tile),
            _resident((1, D), const2),
            _resident((D, 2 * D), const2),
            _resident((CONV_W, D), const2),
            _resident((1, D), const2),
            _resident((2, LRU_BLOCKS, LRU_BLOCK_W, LRU_BLOCK_W), lambda s: (0, 0, 0, 0)),
            _resident((2, D), const2),
            _resident((1, D), const2),
            _resident((D, D), const2),
        ],
        out_specs=pl.BlockSpec((TM, D), tile),
        out_shape=jax.ShapeDtypeStruct((n, D), F32),
        scratch_shapes=[
            pltpu.VMEM((TM + 8, D), F32),
            pltpu.VMEM((TM, D), F32),
            pltpu.VMEM((TM, D), F32),
            pltpu.VMEM((TM, D), F32),
            pltpu.VMEM((TM, D), F32),
            pltpu.VMEM((1, D), F32),
            pltpu.VMEM((1, D), F32),
            pltpu.VMEM((8, D), F32),
        ],
        compiler_params=_params(),
        name="rglru",
    )(h, g.reshape(1, D), w_in, conv_w, conv_b.reshape(1, D), gate_w, gate_b, lru_lambda.reshape(1, D), w_out)


def _split_attn_w_in(w):
    a = HEAD_W
    o_ki = 4 * a
    o_wi = o_ki + IDX_DIM
    o_qb = o_wi + IDX_HEADS
    qa, ka, va, qi = w[:, 0:a], w[:, a:2 * a], w[:, 2 * a:3 * a], w[:, 3 * a:4 * a]
    ki, wi = w[:, o_ki:o_wi], w[:, o_wi:o_qb]
    qb, kb, vb = w[:, o_qb:o_qb + a], w[:, o_qb + a:o_qb + 2 * a], w[:, o_qb + 2 * a:o_qb + 3 * a]
    w_main = jnp.concatenate([qa, ka, qi, qb, kb, va, vb], axis=1).astype(BF16)
    w_wt = jnp.concatenate([wi.T, jnp.zeros((16 - IDX_HEADS, D), w.dtype)], axis=0).astype(BF16)
    return w_main, ki.astype(BF16), w_wt


def kernel(x, meta_tokens, norm_g, ffn_w_gu, ffn_w_down, attn_w_in, idx_k_ln_g, idx_k_ln_b, diff_lambda, diff_subln_g, attn_w_out, rec_w_in, rec_conv_w, rec_conv_b, rec_gate_w, rec_gate_b, rec_lambda, rec_w_out, final_norm_g):
    batch, seq, _ = x.shape
    depth = norm_g.shape[0]
    n_real = batch * seq
    tail = jnp.concatenate([meta_tokens.astype(x.dtype), jnp.zeros((TM - N_META, D), x.dtype)], axis=0)
    w_gu = ffn_w_gu.astype(BF16)
    w_down = ffn_w_down.astype(BF16)
    h = x.reshape(n_real, D)
    for i in range(depth):
        j = i // 2
        h = _ffn(h, norm_g[i, 0], w_gu, w_down, i, 0, tail=tail if i == 0 else None)
        if i % 2 == 0:
            lam_init = 0.8 - 0.6 * math.exp(-0.3 * i)
            w_main, w_ki, w_wt = _split_attn_w_in(attn_w_in[j])
            proj, ki, wit, vat, vbt = _attn_proj(h, norm_g[i, 1], w_main, w_ki, w_wt, idx_k_ln_g[j], idx_k_ln_b[j])
            oa = _dsa(proj, ki, wit, vat, batch, seq)
            ob = _diff(proj, vbt, diff_lambda[j], diff_subln_g[j], batch, seq, lam_init)
            mixer = (oa, ob, attn_w_out[j].astype(BF16))
        else:
            mixer = None
            h = _rglru(h, norm_g[i, 1], rec_w_in[j].astype(BF16), rec_conv_w[j], rec_conv_b[j],
                       rec_gate_w[j].astype(BF16), rec_gate_b[j], rec_lambda[j], rec_w_out[j].astype(BF16),
                       seq // TM)
        if i == depth - 1:
            h = _ffn(h, norm_g[i, 2], w_gu, w_down, i, 1, mixer=mixer, final_g=final_norm_g, n_rows=n_real)
        else:
            h = _ffn(h, norm_g[i, 2], w_gu, w_down, i, 1, mixer=mixer)
    return h.reshape(batch, seq, D)
```

```python
import functools
import math

import jax
import jax.numpy as jnp
from jax import lax
from jax.experimental import pallas as pl
from jax.experimental.pallas import tpu as pltpu

F32 = jnp.float32
BF16 = jnp.bfloat16
I32 = jnp.int32

D = 1024
N_META = 16
D_FF = 2816
EPS = 1e-6
HEAD_W = 512
IDX_DIM = 64
IDX_HEADS = 8
TOPK_MAX = 256
LRU_BLOCKS = 4
LRU_BLOCK_W = D // LRU_BLOCKS
CONV_W = 4
RG_C = 8.0

TM = 512
TQ = 256
META_KEYS = 128
TK = 512
SCAN_SEGS = 4
MXU_TILE = 256
FF_SPLIT = (D_FF // MXU_TILE + 1) // 2 * MXU_TILE
FF_CHUNKS = ((0, FF_SPLIT), (FF_SPLIT, D_FF))
NEG = -1e30
BIG = 1e30
QK_SCALE = IDX_DIM ** -0.5 * math.log2(math.e)
INT_MIN = -2 ** 31
VMEM_LIMIT = 56 * 1024 * 1024

_NT = (((1,), (1,)), ((), ()))


def _params():
    return pltpu.CompilerParams(dimension_semantics=("arbitrary",), vmem_limit_bytes=VMEM_LIMIT)


def _resident(shape, index_map):
    return pl.BlockSpec(shape, index_map, pipeline_mode=pl.Buffered(1))


def _rms(x, g):
    return x * lax.rsqrt(jnp.mean(x * x, axis=-1, keepdims=True) + EPS) * g


def _sigmoid(x):
    return 0.5 * jnp.tanh(0.5 * x) + 0.5


def _ffn_kernel(*refs, final, n_real_tiles, mixer):
    refs = list(refs)
    o_ref = refs.pop()
    x = refs.pop(0)[...]
    if n_real_tiles is not None:
        x = jnp.where(pl.program_id(0) < n_real_tiles, x, refs.pop(0)[...])
    if mixer:
        oa_ref, ob_ref, wa_ref, wb_ref = refs[:4]
        del refs[:4]
        x = (x + jnp.dot(oa_ref[...], wa_ref[...], preferred_element_type=F32)
             + jnp.dot(ob_ref[...], wb_ref[...], preferred_element_type=F32))
    g_ref, wg_ref, wu_ref, wd_ref = refs[:4]
    xn = _rms(x, g_ref[...]).astype(BF16)
    acc = None
    for lo, hi in FF_CHUNKS:
        sl = slice(lo, hi)
        g = jnp.dot(xn, wg_ref[:, sl], preferred_element_type=F32)
        u = jnp.dot(xn, wu_ref[:, sl], preferred_element_type=F32)
        a = (g * _sigmoid(g) * u).astype(BF16)
        part = jnp.dot(a, wd_ref[sl, :], preferred_element_type=F32)
        acc = part if acc is None else acc + part
    y = x + 0.5 * acc
    if final:
        y = _rms(y, refs[4][...])
    o_ref[...] = y


def _ffn(h, g, w_gu, w_down, layer, half, tail=None, mixer=None, final_g=None, n_rows=None):
    n_real_tiles = None if tail is None else h.shape[0] // TM
    if n_rows is None:
        n_rows = h.shape[0] + (0 if tail is None else TM)
    row_tile = lambda i: (i, 0)
    const = lambda i: (0, 0)
    if tail is None:
        in_specs, args = [pl.BlockSpec((TM, D), row_tile)], [h]
    else:
        in_specs = [pl.BlockSpec((TM, D), lambda i: (jnp.minimum(i, n_real_tiles - 1), 0)), _resident((TM, D), const)]
        args = [h, tail]
    if mixer is not None:
        oa, ob, w_out = mixer
        in_specs += [pl.BlockSpec((TM, HEAD_W), row_tile), pl.BlockSpec((TM, HEAD_W), row_tile),
                     _resident((HEAD_W, D), const), _resident((HEAD_W, D), lambda i: (1, 0))]
        args += [oa, ob, w_out, w_out]
    in_specs += [
        _resident((1, D), const),
        _resident((None, None, D, D_FF), lambda i: (layer, half, 0, 0)),
        _resident((None, None, D, D_FF), lambda i: (layer, half, 0, 1)),
        _resident((None, None, D_FF, D), lambda i: (layer, half, 0, 0)),
    ]
    args += [g.reshape(1, D), w_gu, w_gu, w_down]
    if final_g is not None:
        in_specs.append(_resident((1, D), const))
        args.append(final_g.reshape(1, D))
    return pl.pallas_call(
        functools.partial(_ffn_kernel, final=final_g is not None, n_real_tiles=n_real_tiles,
                          mixer=mixer is not None),
        grid=(n_rows // TM,),
        in_specs=in_specs,
        out_specs=pl.BlockSpec((TM, D), row_tile),
        out_shape=jax.ShapeDtypeStruct((n_rows, D), F32),
        compiler_params=_params(),
        name="ffn",
    )(*args)


N_PROJ = 5 * HEAD_W


def _attn_proj_kernel(x_ref, g_ref, w_ref, wki_ref, wwt_ref, lng_ref, lnb_ref,
                      proj_ref, ki_ref, wit_ref, vat_ref, vbt_ref):
    xn = _rms(x_ref[...], g_ref[...]).astype(BF16)
    y = jnp.dot(xn, w_ref[...], preferred_element_type=F32)
    scale = QK_SCALE
    proj_ref[:, 0:HEAD_W] = (y[:, 0:HEAD_W] * scale).astype(BF16)
    proj_ref[:, HEAD_W:3 * HEAD_W] = y[:, HEAD_W:3 * HEAD_W].astype(BF16)
    proj_ref[:, 3 * HEAD_W:4 * HEAD_W] = (y[:, 3 * HEAD_W:4 * HEAD_W] * scale).astype(BF16)
    proj_ref[:, 4 * HEAD_W:5 * HEAD_W] = y[:, 4 * HEAD_W:5 * HEAD_W].astype(BF16)
    vat_ref[0] = y[:, 5 * HEAD_W:6 * HEAD_W].T.astype(BF16)
    vbt_ref[0] = y[:, 6 * HEAD_W:7 * HEAD_W].T.astype(BF16)
    yk = jnp.dot(xn, wki_ref[...], preferred_element_type=F32)
    mu = jnp.mean(yk, axis=-1, keepdims=True)
    yc = yk - mu
    ln = yc * lax.rsqrt(jnp.mean(yc * yc, axis=-1, keepdims=True) + EPS)
    ki_ref[...] = (ln * lng_ref[...] + lnb_ref[...]).astype(BF16)
    wit = lax.dot_general(wwt_ref[...], xn, _NT, preferred_element_type=F32)
    wit_ref[...] = wit * (IDX_HEADS ** -0.5 * IDX_DIM ** -0.5)


def _attn_proj(h, g, w_main, w_ki, w_wt, ln_g, ln_b):
    n = h.shape[0]
    nt = n // TM
    const = lambda i: (0, 0)
    return pl.pallas_call(
        _attn_proj_kernel,
        grid=(nt,),
        in_specs=[
            pl.BlockSpec((TM, D), lambda i: (i, 0)),
            _resident((1, D), const),
            _resident((D, 7 * HEAD_W), const),
            _resident((D, IDX_DIM), const),
            _resident((16, D), const),
            _resident((1, IDX_DIM), const),
            _resident((1, IDX_DIM), const),
        ],
        out_specs=[
            pl.BlockSpec((TM, N_PROJ), lambda i: (i, 0)),
            pl.BlockSpec((TM, IDX_DIM), lambda i: (i, 0)),
            pl.BlockSpec((16, TM), lambda i: (0, i)),
            pl.BlockSpec((1, HEAD_W, TM), lambda i: (i, 0, 0)),
            pl.BlockSpec((1, HEAD_W, TM), lambda i: (i, 0, 0)),
        ],
        out_shape=[
            jax.ShapeDtypeStruct((n, N_PROJ), BF16),
            jax.ShapeDtypeStruct((n, IDX_DIM), BF16),
            jax.ShapeDtypeStruct((16, n), F32),
            jax.ShapeDtypeStruct((nt, HEAD_W, TM), BF16),
            jax.ShapeDtypeStruct((nt, HEAD_W, TM), BF16),
        ],
        compiler_params=_params(),
        name="attn_proj",
    )(h, g.reshape(1, D), w_main, w_ki, w_wt, ln_g.reshape(1, IDX_DIM), ln_b.reshape(1, IDX_DIM))


def _step_geometry(g, seq):
    nq_seq = seq // TQ
    n_real = pl.num_programs(0) - TM // TQ
    is_real = g < n_real
    i = g % nq_seq
    n_ch = jnp.where(is_real, (i * TQ + TQ + TK - 1) // TK, 0)
    return is_real, i, n_ch


def _stack_pair(q_pair):
    lane = lax.broadcasted_iota(I32, (TQ, 128), 1)
    qf = q_pair.astype(F32)
    top = jnp.where(lane < 64, qf, 0.0).astype(BF16)
    bot = jnp.where(lane >= 64, qf, 0.0).astype(BF16)
    return jnp.concatenate([top, bot], axis=0)


def _meta_bias(is_real):
    j = lax.broadcasted_iota(I32, (META_KEYS, 2 * TQ), 0)
    t = lax.broadcasted_iota(I32, (META_KEYS, 2 * TQ), 1) % TQ
    real = is_real.astype(I32)
    lim = jnp.minimum(real * N_META + (1 - real) * (t + 1), N_META)
    return jnp.where(j < lim, BIG, NEG)


def _bcast_halves(row, n_sub):
    return jnp.concatenate([jnp.broadcast_to(row[:, :TQ], (n_sub, TQ)),
                            jnp.broadcast_to(row[:, TQ:], (n_sub, TQ))], axis=0)


SCORE_LOOKAHEAD = 3
MAX_LAG = 64.0


def _softmax_pv(segments, v_tiles, base):
    issued, out = {}, []
    for h in range(SCORE_LOOKAHEAD):
        issued[h] = [fn() for fn in segments[h]]
    for h in range(4):
        if h + SCORE_LOOKAHEAD < 4:
            issued[h + SCORE_LOOKAHEAD] = [fn() for fn in segments[h + SCORE_LOOKAHEAD]]
        sts = issued.pop(h)
        m_loc = jnp.max(sts[0], axis=0, keepdims=True)
        for st in sts[1:]:
            m_loc = jnp.maximum(m_loc, jnp.max(st, axis=0, keepdims=True))
        m_use = base(h, m_loc)
        l_new, pv = None, None
        for st, vt in zip(sts, v_tiles[h]):
            pc = jnp.exp2(st - m_use)
            s = jnp.sum(pc, axis=0, keepdims=True)
            d = jnp.dot(vt(), pc.astype(BF16), preferred_element_type=F32)
            l_new = s if l_new is None else l_new + s
            pv = d if pv is None else pv + d
        out.append((m_loc, l_new, pv))
    return out


def _attend(segments, v_tiles, m_ref, l_ref, acc_ref, fold, first):
    fold_pv, fold_row = fold
    if first:
        for h, (m_loc, l_new, pv) in enumerate(_softmax_pv(segments, v_tiles, lambda h, m_loc: m_loc)):
            m_ref[h] = m_loc
            l_ref[h] = l_new
            acc_ref[h] = fold_pv(pv)
        return

    def commit(parts, rebased):
        for h, (m_loc, l_new, pv) in enumerate(parts):
            m_new = jnp.maximum(m_ref[h], m_loc)
            alpha = jnp.exp2(m_ref[h] - m_new)
            if rebased:
                l_ref[h] = alpha * l_ref[h] + l_new
                acc_ref[h] = acc_ref[h] * fold_row(alpha) + fold_pv(pv)
            else:
                l_ref[h] = alpha * (l_ref[h] + l_new)
                acc_ref[h] = (acc_ref[h] + fold_pv(pv)) * fold_row(alpha)
            m_ref[h] = m_new

    parts = _softmax_pv(segments, v_tiles, lambda h, m_loc: m_ref[h])
    rise = parts[0][0] - m_ref[0]
    for h in range(1, 4):
        rise = jnp.maximum(rise, parts[h][0] - m_ref[h])
    safe = jnp.max(rise) <= MAX_LAG

    @pl.when(safe)
    def _():
        commit(parts, rebased=False)

    @pl.when(jnp.logical_not(safe))
    def _():
        commit(_softmax_pv(segments, v_tiles, lambda h, m_loc: jnp.maximum(m_ref[h], m_loc)), rebased=True)


GROUP_KEYS = 256
GROUPS_PER_CHUNK = TK // GROUP_KEYS


def _bit_transpose32(words):
    a = list(words)
    j, mask = 16, 0x0000FFFF
    while j:
        k = 0
        while k < 32:
            t = (a[k] ^ lax.shift_right_logical(a[k + j], jnp.int32(j))) & mask
            a[k] = a[k] ^ t
            a[k + j] = a[k + j] ^ lax.shift_left(t, jnp.int32(j))
            k = (k + j + 1) & ~j
        j >>= 1
        mask ^= mask << j
    return a


def _dsa_kernel(qa_ref, qi_ref, wit_ref, ki_ref, ka_ref, vat_ref, km_ref, vmt_ref, o_ref,
                qall_ref, qstk_ref, key_ref, plane_ref, bias_ref, tri_ref, m_ref, l_ref, acc_ref, *, seq, top_k):
    g = pl.program_id(0)

    @pl.when(g == 0)
    def _():
        plane_ref[...] = jnp.zeros(plane_ref.shape, I32)
        r = lax.broadcasted_iota(I32, (TK, TK), 0)
        c_ = lax.broadcasted_iota(I32, (TK, TK), 1)
        tri_ref[...] = jnp.where(c_ <= r, 1.0, 0.0).astype(BF16)

    is_real, i, n_ch = _step_geometry(g, seq)
    q_pos = i * TQ + lax.broadcasted_iota(I32, (TK, TQ), 1)
    k_off = lax.broadcasted_iota(I32, (TK, TQ), 0)

    qi = qi_ref[...]
    for h in range(IDX_HEADS):
        qall_ref[h * TQ:(h + 1) * TQ, :] = qi[:, h * IDX_DIM:(h + 1) * IDX_DIM]
    for p in range(4):
        qstk_ref[p] = _stack_pair(qa_ref[:, p * 128:(p + 1) * 128])

    wt = wit_ref[...]

    def score_body(c, carry):
        kc = ki_ref[pl.ds(pl.multiple_of(c * TK, TK), TK), :]
        logits = lax.dot_general(kc, qall_ref[...], _NT, preferred_element_type=F32)
        sc = jnp.zeros((TK, TQ), F32)
        for h in range(IDX_HEADS):
            sc = sc + wt[h:h + 1, :] * jnp.maximum(logits[:, h * TQ:(h + 1) * TQ], 0.0)
        sc = jnp.where(sc == 0.0, 0.0, sc)
        bits = pltpu.bitcast(sc, I32)
        key = bits ^ (lax.shift_right_arithmetic(bits, 31) & 0x7FFFFFFF)
        key = jnp.where(c * TK + k_off <= q_pos, key, INT_MIN)
        key_ref[c] = key
        u = key ^ INT_MIN
        for gi in range(GROUPS_PER_CHUNK):
            base = gi * GROUP_KEYS
            planes = _bit_transpose32([u[base + w * 8:base + (w + 1) * 8, :] for w in range(32)])
            for b in range(32):
                plane_ref[c * GROUPS_PER_CHUNK + gi, b] = planes[b]
        return carry

    lax.fori_loop(0, n_ch, score_body, 0)

    n_grp = n_ch * GROUPS_PER_CHUNK
    n_groups = plane_ref.shape[0]
    alive0 = tuple(jnp.full((8, TQ), -1, I32) * (gq < n_grp).astype(I32) for gq in range(n_groups))

    def bit_body(b, carry):
        alive, k_rem, thr_u = carry
        ones = [a & plane_ref[gq, b] for gq, a in enumerate(alive)]
        cnt8 = lax.population_count(ones[0])
        for o in ones[1:]:
            cnt8 = cnt8 + lax.population_count(o)
        cnt = jnp.sum(cnt8, axis=0, keepdims=True)
        take = cnt >= k_rem
        alive = tuple(jnp.where(take, o, a ^ o) for a, o in zip(alive, ones))
        k_rem = jnp.where(take, k_rem, k_rem - cnt)
        thr_u = thr_u | jnp.where(take, lax.shift_left(jnp.int32(1), 31 - b), 0)
        return alive, k_rem, thr_u

    alive, k_rem, thr_u = lax.fori_loop(
        0, 32, bit_body, (alive0, jnp.full((1, TQ), top_k, I32), jnp.zeros((1, TQ), I32)))
    thr = thr_u ^ INT_MIN
    eq8 = lax.population_count(alive[0])
    for a in alive[1:]:
        eq8 = eq8 + lax.population_count(a)
    cnt_eq = jnp.sum(eq8, axis=0, keepdims=True)
    excess = jnp.where(thr_u != 0, cnt_eq - k_rem, 0)
    has_excess = jnp.max(excess) > 0

    @pl.when(jnp.logical_not(has_excess))
    def _():
        thr_eff = jnp.maximum(thr, INT_MIN + 1)

        def cap_body(c, carry):
            bias_ref[c] = jnp.where(key_ref[c] >= thr_eff, BIG, NEG)
            return carry

        lax.fori_loop(0, n_ch, cap_body, 0)

    @pl.when(has_excess)
    def _():
        need = k_rem.astype(F32)

        def cap_body(c, run):
            k = key_ref[c]
            eq = k == thr
            pre = jnp.dot(tri_ref[...], jnp.where(eq, 1.0, 0.0).astype(BF16), preferred_element_type=F32)
            rank = run + pre
            b = jnp.where(k > thr, BIG, jnp.where(eq, jnp.where(rank <= need, BIG, NEG), NEG))
            bias_ref[c] = jnp.where(k == INT_MIN, NEG, b)
            return run + pre[TK - 1:TK, :]

        lax.fori_loop(0, n_ch, cap_body, jnp.zeros((1, TQ), F32))

    fold = (lambda ot: jnp.concatenate([ot[:64, :TQ], ot[64:, TQ:]], axis=0), lambda row: _bcast_halves(row, 64))
    pairs = [slice(p * 128, (p + 1) * 128) for p in range(4)]

    def score(p, k_ref, idx, cap):
        return lambda: jnp.minimum(lax.dot_general(k_ref[idx], qstk_ref[p], _NT, preferred_element_type=F32), cap)

    def tile(ref, idx):
        return lambda: ref[idx]

    mcap = _meta_bias(is_real)
    _attend([[score(p, km_ref, (slice(None), ps), mcap)] for p, ps in enumerate(pairs)],
            [[tile(vmt_ref, (0, ps, slice(0, META_KEYS)))] for ps in pairs],
            m_ref, l_ref, acc_ref, fold, first=True)

    def update(chunks):
        rows = [pl.ds(pl.multiple_of(c * TK, TK), TK) for c in chunks]
        caps = [jnp.concatenate([bias_ref[c]] * 2, axis=1) for c in chunks]
        _attend([[score(p, ka_ref, (row, ps), cap2) for row, cap2 in zip(rows, caps)] for p, ps in enumerate(pairs)],
                [[tile(vat_ref, (c, ps, slice(None))) for c in chunks] for ps in pairs],
                m_ref, l_ref, acc_ref, fold, first=False)

    def pair_body(j, carry):
        update([2 * j, 2 * j + 1])
        return carry

    lax.fori_loop(0, n_ch // 2, pair_body, 0)

    @pl.when(n_ch % 2 == 1)
    def _():
        update([n_ch - 1])

    for p, ps in enumerate(pairs):
        o_ref[:, ps] = (acc_ref[p] * _bcast_halves(1.0 / l_ref[p], 64)).T.astype(BF16)


def _dsa(proj, ki, wit, vat, batch, seq):
    n = proj.shape[0]
    nq_seq = seq // TQ
    n_real_q = batch * nq_seq
    nch = seq // TK
    top_k = min(TOPK_MAX, seq // 4)
    bidx = lambda g: jnp.minimum(g // nq_seq, batch - 1)
    meta_k = batch * seq // META_KEYS
    meta_tile = batch * nch
    return pl.pallas_call(
        functools.partial(_dsa_kernel, seq=seq, top_k=top_k),
        grid=(n // TQ,),
        in_specs=[
            pl.BlockSpec((TQ, HEAD_W), lambda g: (g, 0)),
            pl.BlockSpec((TQ, HEAD_W), lambda g: (g, 2)),
            pl.BlockSpec((16, TQ), lambda g: (0, g)),
            pl.BlockSpec((seq, IDX_DIM), lambda g: (bidx(g), 0)),
            pl.BlockSpec((seq, HEAD_W), lambda g: (bidx(g), 1)),
            pl.BlockSpec((nch, HEAD_W, TK), lambda g: (bidx(g), 0, 0)),
            pl.BlockSpec((META_KEYS, HEAD_W), lambda g: (meta_k, 1)),
            pl.BlockSpec((1, HEAD_W, TK), lambda g: (meta_tile, 0, 0)),
        ],
        out_specs=pl.BlockSpec((TQ, HEAD_W), lambda g: (g, 0)),
        out_shape=jax.ShapeDtypeStruct((n, HEAD_W), BF16),
        scratch_shapes=[
            pltpu.VMEM((IDX_HEADS * TQ, IDX_DIM), BF16),
            pltpu.VMEM((4, 2 * TQ, 128), BF16),
            pltpu.VMEM((nch, TK, TQ), I32),
            pltpu.VMEM((nch * GROUPS_PER_CHUNK, 32, 8, TQ), I32),
            pltpu.VMEM((nch, TK, TQ), F32),
            pltpu.VMEM((TK, TK), BF16),
            pltpu.VMEM((4, 1, 2 * TQ), F32),
            pltpu.VMEM((4, 1, 2 * TQ), F32),
            pltpu.VMEM((4, 128, TQ), F32),
        ],
        compiler_params=_params(),
        name="dsa",
    )(proj, proj, wit, ki, proj, vat, proj, vat)


def _diff_kernel(lam_ref, g_ref, qb_ref, kb_ref, vbt_ref, km_ref, vmt_ref, o_ref,
                 qstk_ref, m_ref, l_ref, acc_ref, *, seq, lam_init):
    g = pl.program_id(0)
    is_real, i, n_ch = _step_geometry(g, seq)
    lp = lam_ref[...]
    lam = (jnp.exp(jnp.sum(lp[0:1] * lp[1:2], axis=1, keepdims=True))
           - jnp.exp(jnp.sum(lp[2:3] * lp[3:4], axis=1, keepdims=True)) + lam_init)
    heads = [slice(h * 128, (h + 1) * 128) for h in range(4)]
    for h, hs in enumerate(heads):
        qstk_ref[h] = _stack_pair(qb_ref[:, hs])

    fold = (lambda ot: ot, lambda row: row)

    def score(h, k_ref, idx, cap=None):
        def fn():
            st = lax.dot_general(k_ref[idx], qstk_ref[h], _NT, preferred_element_type=F32)
            return st if cap is None else jnp.minimum(st, cap)
        return fn

    def tile(ref, idx):
        return lambda: ref[idx]

    n_full = n_ch - is_real.astype(I32)
    row_diag = pl.ds(pl.multiple_of(n_full * TK, TK), TK)
    mcap = _meta_bias(is_real)
    q_pos = i * TQ + lax.broadcasted_iota(I32, (TK, 2 * TQ), 1) % TQ
    k_pos = n_full * TK + lax.broadcasted_iota(I32, (TK, 2 * TQ), 0)
    cap = jnp.where(jnp.where(k_pos <= q_pos, is_real.astype(I32), 0) > 0, BIG, NEG)
    _attend([[score(h, km_ref, (slice(None), hs), mcap), score(h, kb_ref, (row_diag, hs), cap)]
             for h, hs in enumerate(heads)],
            [[tile(vmt_ref, (0, hs, slice(0, META_KEYS))), tile(vbt_ref, (n_full, hs, slice(None)))] for hs in heads],
            m_ref, l_ref, acc_ref, fold, first=True)

    def update(chunks):
        rows = [pl.ds(pl.multiple_of(c * TK, TK), TK) for c in chunks]
        _attend([[score(h, kb_ref, (row, hs)) for row in rows] for h, hs in enumerate(heads)],
                [[tile(vbt_ref, (c, hs, slice(None))) for c in chunks] for hs in heads],
                m_ref, l_ref, acc_ref, fold, first=False)

    def pair_body(j, carry):
        update([2 * j, 2 * j + 1])
        return carry

    lax.fori_loop(0, n_full // 2, pair_body, 0)

    @pl.when(n_full % 2 == 1)
    def _():
        update([n_full - 1])

    for h, hs in enumerate(heads):
        on = acc_ref[h] * (1.0 / l_ref[h])
        o = on[:, :TQ] - lam * on[:, TQ:]
        y = o * lax.rsqrt(jnp.mean(o * o, axis=0, keepdims=True) + EPS) * g_ref[...] * (1.0 - lam_init)
        o_ref[:, hs] = y.T.astype(BF16)


def _diff(proj, vbt, lam_params, subln_g, batch, seq, lam_init):
    n = proj.shape[0]
    nq_seq = seq // TQ
    nch = seq // TK
    bidx = lambda g: jnp.minimum(g // nq_seq, batch - 1)
    meta_k = batch * seq // META_KEYS
    meta_tile = batch * nch
    return pl.pallas_call(
        functools.partial(_diff_kernel, seq=seq, lam_init=lam_init),
        grid=(n // TQ,),
        in_specs=[
            _resident((4, 64), lambda g: (0, 0)),
            _resident((128, 1), lambda g: (0, 0)),
            pl.BlockSpec((TQ, HEAD_W), lambda g: (g, 3)),
            pl.BlockSpec((seq, HEAD_W), lambda g: (bidx(g), 4)),
            pl.BlockSpec((nch, HEAD_W, TK), lambda g: (bidx(g), 0, 0)),
            pl.BlockSpec((META_KEYS, HEAD_W), lambda g: (meta_k, 4)),
            pl.BlockSpec((1, HEAD_W, TK), lambda g: (meta_tile, 0, 0)),
        ],
        out_specs=pl.BlockSpec((TQ, HEAD_W), lambda g: (g, 0)),
        out_shape=jax.ShapeDtypeStruct((n, HEAD_W), BF16),
        scratch_shapes=[
            pltpu.VMEM((4, 2 * TQ, 128), BF16),
            pltpu.VMEM((4, 1, 2 * TQ), F32),
            pltpu.VMEM((4, 1, 2 * TQ), F32),
            pltpu.VMEM((4, 128, 2 * TQ), F32),
        ],
        compiler_params=_params(),
        name="diff_attn",
    )(lam_params, subln_g.reshape(128, 1), proj, proj, vbt, proj, vbt)


def _rglru_kernel(x_ref, g_ref, win_ref, cw_ref, cb_ref, gw_ref, gb_ref, lam_ref, wout_ref, o_ref,
                  xbuf, a_s, u_s, h_s, p_s, hstate, mstate, mhist, *, tiles_per_seq):
    g = pl.program_id(0)
    x = x_ref[...]
    xn = _rms(x, g_ref[...]).astype(BF16)
    yx = jnp.dot(xn, win_ref[...], preferred_element_type=F32)
    yb = yx[:, :D]
    y_br = 0.5 * yb * (1.0 + jnp.tanh(math.sqrt(2.0 / math.pi) * (yb + 0.044715 * (yb * yb * yb))))

    @pl.when(g == 0)
    def _():
        xbuf[0:8, :] = jnp.zeros((8, D), F32)
        hstate[...] = jnp.zeros(hstate.shape, F32)

    @pl.when(jnp.logical_and(g >= 1, (g - 1) % tiles_per_seq == 0))
    def _():
        xbuf[0:8, :] = mhist[...]
        hstate[...] = mstate[...]

    xbuf[8:, :] = yx[:, D:]
    xc = cb_ref[...] + cw_ref[0:1, :] * xbuf[5:5 + TM, :]
    for j in range(1, CONV_W):
        xc = xc + cw_ref[j:j + 1, :] * xbuf[5 + j:5 + j + TM, :]

    gx, ga = [], []
    for nb in range(LRU_BLOCKS):
        xb = xc[:, nb * LRU_BLOCK_W:(nb + 1) * LRU_BLOCK_W].astype(BF16)
        gx.append(jnp.dot(xb, gw_ref[0, nb], preferred_element_type=F32))
        ga.append(jnp.dot(xb, gw_ref[1, nb], preferred_element_type=F32))
    gate_x = _sigmoid(jnp.concatenate(gx, axis=1) + gb_ref[0:1, :])
    gate_a = _sigmoid(jnp.concatenate(ga, axis=1) + gb_ref[1:2, :])
    lam = lam_ref[...]
    log_sig = jnp.minimum(lam, 0.0) - jnp.log(1.0 + jnp.exp(-jnp.abs(lam)))
    log_a = RG_C * gate_a * log_sig
    a = jnp.exp(log_a)
    a_s[...] = a
    u_s[...] = jnp.sqrt(-jnp.tanh(log_a) * (a * a + 1.0)) * (gate_x * xc)

    seg_rows = TM // SCAN_SEGS

    def step(r, carry):
        hs, ps = carry
        new_h, new_p = [], []
        for s in range(SCAN_SEGS):
            row = pl.ds(s * seg_rows + r, 1)
            a_r = a_s[row, :]
            h_r = a_r * hs[s] + u_s[row, :]
            h_s[row, :] = h_r
            new_h.append(h_r)
            if s > 0:
                p_r = a_r * ps[s - 1]
                p_s[row, :] = p_r
                new_p.append(p_r)
        return tuple(new_h), tuple(new_p)

    zero = jnp.zeros((1, D), F32)
    hs, _ = lax.fori_loop(0, seg_rows, step,
                          ((hstate[...],) + (zero,) * (SCAN_SEGS - 1), (zero + 1.0,) * (SCAN_SEGS - 1)), unroll=4)
    h_in = hs[0]
    for s in range(1, SCAN_SEGS):
        seg = slice(s * seg_rows, (s + 1) * seg_rows)
        h_seg = h_s[seg, :] + p_s[seg, :] * h_in
        h_s[seg, :] = h_seg
        h_in = h_seg[seg_rows - 1:seg_rows, :]
    hstate[...] = h_in

    @pl.when(g == 0)
    def _():
        mstate[...] = h_s[N_META - 1:N_META, :]
        mhist[...] = xbuf[N_META:N_META + 8, :]

    xbuf[0:8, :] = xbuf[TM:TM + 8, :]
    o_ref[...] = x + jnp.dot((h_s[...] * y_br).astype(BF16), wout_ref[...], preferred_element_type=F32)


def _rglru(h, g, w_in, conv_w, conv_b, gate_w, gate_b, lru_lambda, w_out, tiles_per_seq):
    n = h.shape[0]
    nt = n // TM
    tile = lambda s: (jnp.where(s == 0, nt - 1, s - 1), 0)
    const2 = lambda s: (0, 0)
    return pl.pallas_call(
        functools.partial(_rglru_kernel, tiles_per_seq=tiles_per_seq),
        grid=(nt,),
        in_specs=[
            pl.BlockSpec((TM, D), tile),
            _resident((1, D), const2),
            _resident((D, 2 * D), const2),
            _resident((CONV_W, D), const2),
            _resident((1, D), const2),
            _resident((2, LRU_BLOCKS, LRU_BLOCK_W, LRU_BLOCK_W), lambda s: (0, 0, 0, 0)),
            _resident((2, D), const2),
            _resident((1, D), const2),
            _resident((D, D), const2),
        ],
        out_specs=pl.BlockSpec((TM, D), tile),
        out_shape=jax.ShapeDtypeStruct((n, D), F32),
        scratch_shapes=[
            pltpu.VMEM((TM + 8, D), F32),
            pltpu.VMEM((TM, D), F32),
            pltpu.VMEM((TM, D), F32),
            pltpu.VMEM((TM, D), F32),
            pltpu.VMEM((TM, D), F32),
            pltpu.VMEM((1, D), F32),
            pltpu.VMEM((1, D), F32),
            pltpu.VMEM((8, D), F32),
        ],
        compiler_params=_params(),
        name="rglru",
    )(h, g.reshape(1, D), w_in, conv_w, conv_b.reshape(1, D), gate_w, gate_b, lru_lambda.reshape(1, D), w_out)


def _split_attn_w_in(w):
    a = HEAD_W
    o_ki = 4 * a
    o_wi = o_ki + IDX_DIM
    o_qb = o_wi + IDX_HEADS
    qa, ka, va, qi = w[:, 0:a], w[:, a:2 * a], w[:, 2 * a:3 * a], w[:, 3 * a:4 * a]
    ki, wi = w[:, o_ki:o_wi], w[:, o_wi:o_qb]
    qb, kb, vb = w[:, o_qb:o_qb + a], w[:, o_qb + a:o_qb + 2 * a], w[:, o_qb + 2 * a:o_qb + 3 * a]
    w_main = jnp.concatenate([qa, ka, qi, qb, kb, va, vb], axis=1).astype(BF16)
    w_wt = jnp.concatenate([wi.T, jnp.zeros((16 - IDX_HEADS, D), w.dtype)], axis=0).astype(BF16)
    return w_main, ki.astype(BF16), w_wt


def kernel(x, meta_tokens, norm_g, ffn_w_gu, ffn_w_down, attn_w_in, idx_k_ln_g, idx_k_ln_b, diff_lambda, diff_subln_g, attn_w_out, rec_w_in, rec_conv_w, rec_conv_b, rec_gate_w, rec_gate_b, rec_lambda, rec_w_out, final_norm_g):
    batch, seq, _ = x.shape
    depth = norm_g.shape[0]
    n_real = batch * seq
    tail = jnp.concatenate([meta_tokens.astype(x.dtype), jnp.zeros((TM - N_META, D), x.dtype)], axis=0)
    w_gu = ffn_w_gu.astype(BF16)
    w_down = ffn_w_down.astype(BF16)
    h = x.reshape(n_real, D)
    for i in range(depth):
        j = i // 2
        h = _ffn(h, norm_g[i, 0], w_gu, w_down, i, 0, tail=tail if i == 0 else None)
        if i % 2 == 0:
            lam_init = 0.8 - 0.6 * math.exp(-0.3 * i)
            w_main, w_ki, w_wt = _split_attn_w_in(attn_w_in[j])
            proj, ki, wit, vat, vbt = _attn_proj(h, norm_g[i, 1], w_main, w_ki, w_wt, idx_k_ln_g[j], idx_k_ln_b[j])
            oa = _dsa(proj, ki, wit, vat, batch, seq)
            ob = _diff(proj, vbt, diff_lambda[j], diff_subln_g[j], batch, seq, lam_init)
            mixer = (oa, ob, attn_w_out[j].astype(BF16))
        else:
            mixer = None
            h = _rglru(h, norm_g[i, 1], rec_w_in[j].astype(BF16), rec_conv_w[j], rec_conv_b[j],
                       rec_gate_w[j].astype(BF16), rec_gate_b[j], rec_lambda[j], rec_w_out[j].astype(BF16),
                       seq // TM)
        if i == depth - 1:
            h = _ffn(h, norm_g[i, 2], w_gu, w_down, i, 1, mixer=mixer, final_g=final_norm_g, n_rows=n_real)
        else:
            h = _ffn(h, norm_g[i, 2], w_gu, w_down, i, 1, mixer=mixer)
    return h.reshape(batch, seq, D)
```

```python
import functools
import math

import jax
import jax.numpy as jnp
from jax import lax
from jax.experimental import pallas as pl
from jax.experimental.pallas import tpu as pltpu

F32 = jnp.float32
BF16 = jnp.bfloat16
I32 = jnp.int32

D = 1024
N_META = 16
D_FF = 2816
EPS = 1e-6
HEAD_W = 512
IDX_DIM = 64
IDX_HEADS = 8
TOPK_MAX = 256
LRU_BLOCKS = 4
LRU_BLOCK_W = D // LRU_BLOCKS
CONV_W = 4
RG_C = 8.0

TM = 512
TQ = 256
META_KEYS = 128
TK = 512
SCAN_SEGS = 4
MXU_TILE = 256
FF_SPLIT = (D_FF // MXU_TILE + 1) // 2 * MXU_TILE
FF_CHUNKS = ((0, FF_SPLIT), (FF_SPLIT, D_FF))
NEG = -1e30
BIG = 1e30
QK_SCALE = IDX_DIM ** -0.5 * math.log2(math.e)
INT_MIN = -2 ** 31
VMEM_LIMIT = 56 * 1024 * 1024

_NT = (((1,), (1,)), ((), ()))


def _params():
    return pltpu.CompilerParams(dimension_semantics=("arbitrary",), vmem_limit_bytes=VMEM_LIMIT)


def _resident(shape, index_map):
    return pl.BlockSpec(shape, index_map, pipeline_mode=pl.Buffered(1))


def _rms(x, g):
    return x * lax.rsqrt(jnp.mean(x * x, axis=-1, keepdims=True) + EPS) * g


def _sigmoid(x):
    return 0.5 * jnp.tanh(0.5 * x) + 0.5


def _ffn_kernel(*refs, final, n_real_tiles, mixer):
    refs = list(refs)
    o_ref = refs.pop()
    x = refs.pop(0)[...]
    if n_real_tiles is not None:
        x = jnp.where(pl.program_id(0) < n_real_tiles, x, refs.pop(0)[...])
    if mixer:
        oa_ref, ob_ref, wa_ref, wb_ref = refs[:4]
        del refs[:4]
        x = (x + jnp.dot(oa_ref[...], wa_ref[...], preferred_element_type=F32)
             + jnp.dot(ob_ref[...], wb_ref[...], preferred_element_type=F32))
    g_ref, wg_ref, wu_ref, wd_ref = refs[:4]
    xn = _rms(x, g_ref[...]).astype(BF16)
    acc = None
    for lo, hi in FF_CHUNKS:
        sl = slice(lo, hi)
        g = jnp.dot(xn, wg_ref[:, sl], preferred_element_type=F32)
        u = jnp.dot(xn, wu_ref[:, sl], preferred_element_type=F32)
        a = (g * _sigmoid(g) * u).astype(BF16)
        part = jnp.dot(a, wd_ref[sl, :], preferred_element_type=F32)
        acc = part if acc is None else acc + part
    y = x + 0.5 * acc
    if final:
        y = _rms(y, refs[4][...])
    o_ref[...] = y


def _ffn(h, g, w_gu, w_down, layer, half, tail=None, mixer=None, final_g=None, n_rows=None):
    n_real_tiles = None if tail is None else h.shape[0] // TM
    if n_rows is None:
        n_rows = h.shape[0] + (0 if tail is None else TM)
    row_tile = lambda i: (i, 0)
    const = lambda i: (0, 0)
    if tail is None:
        in_specs, args = [pl.BlockSpec((TM, D), row_tile)], [h]
    else:
        in_specs = [pl.BlockSpec((TM, D), lambda i: (jnp.minimum(i, n_real_tiles - 1), 0)), _resident((TM, D), const)]
        args = [h, tail]
    if mixer is not None:
        oa, ob, w_out = mixer
        in_specs += [pl.BlockSpec((TM, HEAD_W), row_tile), pl.BlockSpec((TM, HEAD_W), row_tile),
                     _resident((HEAD_W, D), const), _resident((HEAD_W, D), lambda i: (1, 0))]
        args += [oa, ob, w_out, w_out]
    in_specs += [
        _resident((1, D), const),
        _resident((None, None, D, D_FF), lambda i: (layer, half, 0, 0)),
        _resident((None, None, D, D_FF), lambda i: (layer, half, 0, 1)),
        _resident((None, None, D_FF, D), lambda i: (layer, half, 0, 0)),
    ]
    args += [g.reshape(1, D), w_gu, w_gu, w_down]
    if final_g is not None:
        in_specs.append(_resident((1, D), const))
        args.append(final_g.reshape(1, D))
    return pl.pallas_call(
        functools.partial(_ffn_kernel, final=final_g is not None, n_real_tiles=n_real_tiles,
                          mixer=mixer is not None),
        grid=(n_rows // TM,),
        in_specs=in_specs,
        out_specs=pl.BlockSpec((TM, D), row_tile),
        out_shape=jax.ShapeDtypeStruct((n_rows, D), F32),
        compiler_params=_params(),
        name="ffn",
    )(*args)


N_PROJ = 5 * HEAD_W


def _attn_proj_kernel(x_ref, g_ref, w_ref, wki_ref, wwt_ref, lng_ref, lnb_ref,
                      proj_ref, ki_ref, wit_ref, vat_ref, vbt_ref):
    xn = _rms(x_ref[...], g_ref[...]).astype(BF16)
    y = jnp.dot(xn, w_ref[...], preferred_element_type=F32)
    scale = QK_SCALE
    proj_ref[:, 0:HEAD_W] = (y[:, 0:HEAD_W] * scale).astype(BF16)
    proj_ref[:, HEAD_W:3 * HEAD_W] = y[:, HEAD_W:3 * HEAD_W].astype(BF16)
    proj_ref[:, 3 * HEAD_W:4 * HEAD_W] = (y[:, 3 * HEAD_W:4 * HEAD_W] * scale).astype(BF16)
    proj_ref[:, 4 * HEAD_W:5 * HEAD_W] = y[:, 4 * HEAD_W:5 * HEAD_W].astype(BF16)
    vat_ref[0] = y[:, 5 * HEAD_W:6 * HEAD_W].T.astype(BF16)
    vbt_ref[0] = y[:, 6 * HEAD_W:7 * HEAD_W].T.astype(BF16)
    yk = jnp.dot(xn, wki_ref[...], preferred_element_type=F32)
    mu = jnp.mean(yk, axis=-1, keepdims=True)
    yc = yk - mu
    ln = yc * lax.rsqrt(jnp.mean(yc * yc, axis=-1, keepdims=True) + EPS)
    ki_ref[...] = (ln * lng_ref[...] + lnb_ref[...]).astype(BF16)
    wit = lax.dot_general(wwt_ref[...], xn, _NT, preferred_element_type=F32)
    wit_ref[...] = wit * (IDX_HEADS ** -0.5 * IDX_DIM ** -0.5)


def _attn_proj(h, g, w_main, w_ki, w_wt, ln_g, ln_b):
    n = h.shape[0]
    nt = n // TM
    const = lambda i: (0, 0)
    return pl.pallas_call(
        _attn_proj_kernel,
        grid=(nt,),
        in_specs=[
            pl.BlockSpec((TM, D), lambda i: (i, 0)),
            _resident((1, D), const),
            _resident((D, 7 * HEAD_W), const),
            _resident((D, IDX_DIM), const),
            _resident((16, D), const),
            _resident((1, IDX_DIM), const),
            _resident((1, IDX_DIM), const),
        ],
        out_specs=[
            pl.BlockSpec((TM, N_PROJ), lambda i: (i, 0)),
            pl.BlockSpec((TM, IDX_DIM), lambda i: (i, 0)),
            pl.BlockSpec((16, TM), lambda i: (0, i)),
            pl.BlockSpec((1, HEAD_W, TM), lambda i: (i, 0, 0)),
            pl.BlockSpec((1, HEAD_W, TM), lambda i: (i, 0, 0)),
        ],
        out_shape=[
            jax.ShapeDtypeStruct((n, N_PROJ), BF16),
            jax.ShapeDtypeStruct((n, IDX_DIM), BF16),
            jax.ShapeDtypeStruct((16, n), F32),
            jax.ShapeDtypeStruct((nt, HEAD_W, TM), BF16),
            jax.ShapeDtypeStruct((nt, HEAD_W, TM), BF16),
        ],
        compiler_params=_params(),
        name="attn_proj",
    )(h, g.reshape(1, D), w_main, w_ki, w_wt, ln_g.reshape(1, IDX_DIM), ln_b.reshape(1, IDX_DIM))


def _step_geometry(g, seq):
    nq_seq = seq // TQ
    n_real = pl.num_programs(0) - TM // TQ
    is_real = g < n_real
    i = g % nq_seq
    n_ch = jnp.where(is_real, (i * TQ + TQ + TK - 1) // TK, 0)
    return is_real, i, n_ch


def _stack_pair(q_pair):
    lane = lax.broadcasted_iota(I32, (TQ, 128), 1)
    qf = q_pair.astype(F32)
    top = jnp.where(lane < 64, qf, 0.0).astype(BF16)
    bot = jnp.where(lane >= 64, qf, 0.0).astype(BF16)
    return jnp.concatenate([top, bot], axis=0)


def _meta_bias(is_real):
    j = lax.broadcasted_iota(I32, (META_KEYS, 2 * TQ), 0)
    t = lax.broadcasted_iota(I32, (META_KEYS, 2 * TQ), 1) % TQ
    real = is_real.astype(I32)
    lim = jnp.minimum(real * N_META + (1 - real) * (t + 1), N_META)
    return jnp.where(j < lim, BIG, NEG)


def _bcast_halves(row, n_sub):
    return jnp.concatenate([jnp.broadcast_to(row[:, :TQ], (n_sub, TQ)),
                            jnp.broadcast_to(row[:, TQ:], (n_sub, TQ))], axis=0)


SCORE_LOOKAHEAD = 3
MAX_LAG = 64.0


def _softmax_pv(segments, v_tiles, base):
    issued, out = {}, []
    for h in range(SCORE_LOOKAHEAD):
        issued[h] = [fn() for fn in segments[h]]
    for h in range(4):
        if h + SCORE_LOOKAHEAD < 4:
            issued[h + SCORE_LOOKAHEAD] = [fn() for fn in segments[h + SCORE_LOOKAHEAD]]
        sts = issued.pop(h)
        m_loc = jnp.max(sts[0], axis=0, keepdims=True)
        for st in sts[1:]:
            m_loc = jnp.maximum(m_loc, jnp.max(st, axis=0, keepdims=True))
        m_use = base(h, m_loc)
        l_new, pv = None, None
        for st, vt in zip(sts, v_tiles[h]):
            pc = jnp.exp2(st - m_use)
            s = jnp.sum(pc, axis=0, keepdims=True)
            d = jnp.dot(vt(), pc.astype(BF16), preferred_element_type=F32)
            l_new = s if l_new is None else l_new + s
            pv = d if pv is None else pv + d
        out.append((m_loc, l_new, pv))
    return out


def _attend(segments, v_tiles, m_ref, l_ref, acc_ref, fold, first):
    fold_pv, fold_row = fold
    if first:
        for h, (m_loc, l_new, pv) in enumerate(_softmax_pv(segments, v_tiles, lambda h, m_loc: m_loc)):
            m_ref[h] = m_loc
            l_ref[h] = l_new
            acc_ref[h] = fold_pv(pv)
        return

    def commit(parts, rebased):
        for h, (m_loc, l_new, pv) in enumerate(parts):
            m_new = jnp.maximum(m_ref[h], m_loc)
            alpha = jnp.exp2(m_ref[h] - m_new)
            if rebased:
                l_ref[h] = alpha * l_ref[h] + l_new
                acc_ref[h] = acc_ref[h] * fold_row(alpha) + fold_pv(pv)
            else:
                l_ref[h] = alpha * (l_ref[h] + l_new)
                acc_ref[h] = (acc_ref[h] + fold_pv(pv)) * fold_row(alpha)
            m_ref[h] = m_new

    parts = _softmax_pv(segments, v_tiles, lambda h, m_loc: m_ref[h])
    rise = parts[0][0] - m_ref[0]
    for h in range(1, 4):
        rise = jnp.maximum(rise, parts[h][0] - m_ref[h])
    safe = jnp.max(rise) <= MAX_LAG

    @pl.when(safe)
    def _():
        commit(parts, rebased=False)

    @pl.when(jnp.logical_not(safe))
    def _():
        commit(_softmax_pv(segments, v_tiles, lambda h, m_loc: jnp.maximum(m_ref[h], m_loc)), rebased=True)


GROUP_KEYS = 256
GROUPS_PER_CHUNK = TK // GROUP_KEYS


def _bit_transpose32(words):
    a = list(words)
    j, mask = 16, 0x0000FFFF
    while j:
        k = 0
        while k < 32:
            t = (a[k] ^ lax.shift_right_logical(a[k + j], jnp.int32(j))) & mask
            a[k] = a[k] ^ t
            a[k + j] = a[k + j] ^ lax.shift_left(t, jnp.int32(j))
            k = (k + j + 1) & ~j
        j >>= 1
        mask ^= mask << j
    return a


def _dsa_kernel(qa_ref, qi_ref, wit_ref, ki_ref, ka_ref, vat_ref, km_ref, vmt_ref, o_ref,
                qall_ref, qstk_ref, key_ref, plane_ref, bias_ref, tri_ref, m_ref, l_ref, acc_ref, *, seq, top_k):
    g = pl.program_id(0)

    @pl.when(g == 0)
    def _():
        plane_ref[...] = jnp.zeros(plane_ref.shape, I32)
        r = lax.broadcasted_iota(I32, (TK, TK), 0)
        c_ = lax.broadcasted_iota(I32, (TK, TK), 1)
        tri_ref[...] = jnp.where(c_ <= r, 1.0, 0.0).astype(BF16)

    is_real, i, n_ch = _step_geometry(g, seq)
    q_pos = i * TQ + lax.broadcasted_iota(I32, (TK, TQ), 1)
    k_off = lax.broadcasted_iota(I32, (TK, TQ), 0)

    qi = qi_ref[...]
    for h in range(IDX_HEADS):
        qall_ref[h * TQ:(h + 1) * TQ, :] = qi[:, h * IDX_DIM:(h + 1) * IDX_DIM]
    for p in range(4):
        qstk_ref[p] = _stack_pair(qa_ref[:, p * 128:(p + 1) * 128])

    wt = wit_ref[...]

    def score_body(c, carry):
        kc = ki_ref[pl.ds(pl.multiple_of(c * TK, TK), TK), :]
        logits = lax.dot_general(kc, qall_ref[...], _NT, preferred_element_type=F32)
        sc = jnp.zeros((TK, TQ), F32)
        for h in range(IDX_HEADS):
            sc = sc + wt[h:h + 1, :] * jnp.maximum(logits[:, h * TQ:(h + 1) * TQ], 0.0)
        sc = jnp.where(sc == 0.0, 0.0, sc)
        bits = pltpu.bitcast(sc, I32)
        key = bits ^ (lax.shift_right_arithmetic(bits, 31) & 0x7FFFFFFF)
        key = jnp.where(c * TK + k_off <= q_pos, key, INT_MIN)
        key_ref[c] = key
        u = key ^ INT_MIN
        for gi in range(GROUPS_PER_CHUNK):
            base = gi * GROUP_KEYS
            planes = _bit_transpose32([u[base + w * 8:base + (w + 1) * 8, :] for w in range(32)])
            for b in range(32):
                plane_ref[c * GROUPS_PER_CHUNK + gi, b] = planes[b]
        return carry

    lax.fori_loop(0, n_ch, score_body, 0)

    n_grp = n_ch * GROUPS_PER_CHUNK
    n_groups = plane_ref.shape[0]
    alive0 = tuple(jnp.full((8, TQ), -1, I32) * (gq < n_grp).astype(I32) for gq in range(n_groups))

    def bit_body(b, carry):
        alive, k_rem, thr_u = carry
        ones = [a & plane_ref[gq, b] for gq, a in enumerate(alive)]
        cnt8 = lax.population_count(ones[0])
        for o in ones[1:]:
            cnt8 = cnt8 + lax.population_count(o)
        cnt = jnp.sum(cnt8, axis=0, keepdims=True)
        take = cnt >= k_rem
        alive = tuple(jnp.where(take, o, a ^ o) for a, o in zip(alive, ones))
        k_rem = jnp.where(take, k_rem, k_rem - cnt)
        thr_u = thr_u | jnp.where(take, lax.shift_left(jnp.int32(1), 31 - b), 0)
        return alive, k_rem, thr_u

    alive, k_rem, thr_u = lax.fori_loop(
        0, 32, bit_body, (alive0, jnp.full((1, TQ), top_k, I32), jnp.zeros((1, TQ), I32)))
    thr = thr_u ^ INT_MIN
    eq8 = lax.population_count(alive[0])
    for a in alive[1:]:
        eq8 = eq8 + lax.population_count(a)
    cnt_eq = jnp.sum(eq8, axis=0, keepdims=True)
    excess = jnp.where(thr_u != 0, cnt_eq - k_rem, 0)
    has_excess = jnp.max(excess) > 0

    @pl.when(jnp.logical_not(has_excess))
    def _():
        thr_eff = jnp.maximum(thr, INT_MIN + 1)

        def cap_body(c, carry):
            bias_ref[c] = jnp.where(key_ref[c] >= thr_eff, BIG, NEG)
            return carry

        lax.fori_loop(0, n_ch, cap_body, 0)

    @pl.when(has_excess)
    def _():
        need = k_rem.astype(F32)

        def cap_body(c, run):
            k = key_ref[c]
            eq = k == thr
            pre = jnp.dot(tri_ref[...], jnp.where(eq, 1.0, 0.0).astype(BF16), preferred_element_type=F32)
            rank = run + pre
            b = jnp.where(k > thr, BIG, jnp.where(eq, jnp.where(rank <= need, BIG, NEG), NEG))
            bias_ref[c] = jnp.where(k == INT_MIN, NEG, b)
            return run + pre[TK - 1:TK, :]

        lax.fori_loop(0, n_ch, cap_body, jnp.zeros((1, TQ), F32))

    fold = (lambda ot: jnp.concatenate([ot[:64, :TQ], ot[64:, TQ:]], axis=0), lambda row: _bcast_halves(row, 64))
    pairs = [slice(p * 128, (p + 1) * 128) for p in range(4)]

    def score(p, k_ref, idx, cap):
        return lambda: jnp.minimum(lax.dot_general(k_ref[idx], qstk_ref[p], _NT, preferred_element_type=F32), cap)

    def tile(ref, idx):
        return lambda: ref[idx]

    mcap = _meta_bias(is_real)
    _attend([[score(p, km_ref, (slice(None), ps), mcap)] for p, ps in enumerate(pairs)],
            [[tile(vmt_ref, (0, ps, slice(0, META_KEYS)))] for ps in pairs],
            m_ref, l_ref, acc_ref, fold, first=True)

    def update(chunks):
        rows = [pl.ds(pl.multiple_of(c * TK, TK), TK) for c in chunks]
        caps = [jnp.concatenate([bias_ref[c]] * 2, axis=1) for c in chunks]
        _attend([[score(p, ka_ref, (row, ps), cap2) for row, cap2 in zip(rows, caps)] for p, ps in enumerate(pairs)],
                [[tile(vat_ref, (c, ps, slice(None))) for c in chunks] for ps in pairs],
                m_ref, l_ref, acc_ref, fold, first=False)

    def pair_body(j, carry):
        update([2 * j, 2 * j + 1])
        return carry

    lax.fori_loop(0, n_ch // 2, pair_body, 0)

    @pl.when(n_ch % 2 == 1)
    def _():
        update([n_ch - 1])

    for p, ps in enumerate(pairs):
        o_ref[:, ps] = (acc_ref[p] * _bcast_halves(1.0 / l_ref[p], 64)).T.astype(BF16)


def _dsa(proj, ki, wit, vat, batch, seq):
    n = proj.shape[0]
    nq_seq = seq // TQ
    n_real_q = batch * nq_seq
    nch = seq // TK
    top_k = min(TOPK_MAX, seq // 4)
    bidx = lambda g: jnp.minimum(g // nq_seq, batch - 1)
    meta_k = batch * seq // META_KEYS
    meta_tile = batch * nch
    return pl.pallas_call(
        functools.partial(_dsa_kernel, seq=seq, top_k=top_k),
        grid=(n // TQ,),
        in_specs=[
            pl.BlockSpec((TQ, HEAD_W), lambda g: (g, 0)),
            pl.BlockSpec((TQ, HEAD_W), lambda g: (g, 2)),
            pl.BlockSpec((16, TQ), lambda g: (0, g)),
            pl.BlockSpec((seq, IDX_DIM), lambda g: (bidx(g), 0)),
            pl.BlockSpec((seq, HEAD_W), lambda g: (bidx(g), 1)),
            pl.BlockSpec((nch, HEAD_W, TK), lambda g: (bidx(g), 0, 0)),
            pl.BlockSpec((META_KEYS, HEAD_W), lambda g: (meta_k, 1)),
            pl.BlockSpec((1, HEAD_W, TK), lambda g: (meta_tile, 0, 0)),
        ],
        out_specs=pl.BlockSpec((TQ, HEAD_W), lambda g: (g, 0)),
        out_shape=jax.ShapeDtypeStruct((n, HEAD_W), BF16),
        scratch_shapes=[
            pltpu.VMEM((IDX_HEADS * TQ, IDX_DIM), BF16),
            pltpu.VMEM((4, 2 * TQ, 128), BF16),
            pltpu.VMEM((nch, TK, TQ), I32),
            pltpu.VMEM((nch * GROUPS_PER_CHUNK, 32, 8, TQ), I32),
            pltpu.VMEM((nch, TK, TQ), F32),
            pltpu.VMEM((TK, TK), BF16),
            pltpu.VMEM((4, 1, 2 * TQ), F32),
            pltpu.VMEM((4, 1, 2 * TQ), F32),
            pltpu.VMEM((4, 128, TQ), F32),
        ],
        compiler_params=_params(),
        name="dsa",
    )(proj, proj, wit, ki, proj, vat, proj, vat)


def _diff_kernel(lam_ref, g_ref, qb_ref, kb_ref, vbt_ref, km_ref, vmt_ref, o_ref,
                 qstk_ref, m_ref, l_ref, acc_ref, *, seq, lam_init):
    g = pl.program_id(0)
    is_real, i, n_ch = _step_geometry(g, seq)
    lp = lam_ref[...]
    lam = (jnp.exp(jnp.sum(lp[0:1] * lp[1:2], axis=1, keepdims=True))
           - jnp.exp(jnp.sum(lp[2:3] * lp[3:4], axis=1, keepdims=True)) + lam_init)
    heads = [slice(h * 128, (h + 1) * 128) for h in range(4)]
    for h, hs in enumerate(heads):
        qstk_ref[h] = _stack_pair(qb_ref[:, hs])

    fold = (lambda ot: ot, lambda row: row)

    def score(h, k_ref, idx, cap=None):
        def fn():
            st = lax.dot_general(k_ref[idx], qstk_ref[h], _NT, preferred_element_type=F32)
            return st if cap is None else jnp.minimum(st, cap)
        return fn

    def tile(ref, idx):
        return lambda: ref[idx]

    n_full = n_ch - is_real.astype(I32)
    mcap = _meta_bias(is_real)
    _attend([[score(h, km_ref, (slice(None), hs), mcap)] for h, hs in enumerate(heads)],
            [[tile(vmt_ref, (0, hs, slice(0, META_KEYS)))] for hs in heads],
            m_ref, l_ref, acc_ref, fold, first=True)

    def update(chunks, caps):
        rows = [pl.ds(pl.multiple_of(c * TK, TK), TK) for c in chunks]
        _attend([[score(h, kb_ref, (row, hs), cap) for row, cap in zip(rows, caps)] for h, hs in enumerate(heads)],
                [[tile(vbt_ref, (c, hs, slice(None))) for c in chunks] for hs in heads],
                m_ref, l_ref, acc_ref, fold, first=False)

    def pair_body(j, carry):
        update([2 * j, 2 * j + 1], [None, None])
        return carry

    lax.fori_loop(0, n_full // 2, pair_body, 0)

    q_pos = i * TQ + lax.broadcasted_iota(I32, (TK, 2 * TQ), 1) % TQ
    k_pos = n_full * TK + lax.broadcasted_iota(I32, (TK, 2 * TQ), 0)
    cap = jnp.where(jnp.where(k_pos <= q_pos, is_real.astype(I32), 0) > 0, BIG, NEG)

    @pl.when(n_full % 2 == 1)
    def _():
        update([n_full - 1, n_full], [None, cap])

    @pl.when(n_full % 2 == 0)
    def _():
        update([n_full], [cap])

    for h, hs in enumerate(heads):
        on = acc_ref[h] * (1.0 / l_ref[h])
        o = on[:, :TQ] - lam * on[:, TQ:]
        y = o * lax.rsqrt(jnp.mean(o * o, axis=0, keepdims=True) + EPS) * g_ref[...] * (1.0 - lam_init)
        o_ref[:, hs] = y.T.astype(BF16)


def _diff(proj, vbt, lam_params, subln_g, batch, seq, lam_init):
    n = proj.shape[0]
    nq_seq = seq // TQ
    nch = seq // TK
    bidx = lambda g: jnp.minimum(g // nq_seq, batch - 1)
    meta_k = batch * seq // META_KEYS
    meta_tile = batch * nch
    return pl.pallas_call(
        functools.partial(_diff_kernel, seq=seq, lam_init=lam_init),
        grid=(n // TQ,),
        in_specs=[
            _resident((4, 64), lambda g: (0, 0)),
            _resident((128, 1), lambda g: (0, 0)),
            pl.BlockSpec((TQ, HEAD_W), lambda g: (g, 3)),
            pl.BlockSpec((seq, HEAD_W), lambda g: (bidx(g), 4)),
            pl.BlockSpec((nch, HEAD_W, TK), lambda g: (bidx(g), 0, 0)),
            pl.BlockSpec((META_KEYS, HEAD_W), lambda g: (meta_k, 4)),
            pl.BlockSpec((1, HEAD_W, TK), lambda g: (meta_tile, 0, 0)),
        ],
        out_specs=pl.BlockSpec((TQ, HEAD_W), lambda g: (g, 0)),
        out_shape=jax.ShapeDtypeStruct((n, HEAD_W), BF16),
        scratch_shapes=[
            pltpu.VMEM((4, 2 * TQ, 128), BF16),
            pltpu.VMEM((4, 1, 2 * TQ), F32),
            pltpu.VMEM((4, 1, 2 * TQ), F32),
            pltpu.VMEM((4, 128, 2 * TQ), F32),
        ],
        compiler_params=_params(),
        name="diff_attn",
    )(lam_params, subln_g.reshape(128, 1), proj, proj, vbt, proj, vbt)


def _rglru_kernel(x_ref, g_ref, win_ref, cw_ref, cb_ref, gw_ref, gb_ref, lam_ref, wout_ref, o_ref,
                  xbuf, a_s, u_s, h_s, p_s, hstate, mstate, mhist, *, tiles_per_seq):
    g = pl.program_id(0)
    x = x_ref[...]
    xn = _rms(x, g_ref[...]).astype(BF16)
    yx = jnp.dot(xn, win_ref[...], preferred_element_type=F32)
    yb = yx[:, :D]
    y_br = 0.5 * yb * (1.0 + jnp.tanh(math.sqrt(2.0 / math.pi) * (yb + 0.044715 * (yb * yb * yb))))

    @pl.when(g == 0)
    def _():
        xbuf[0:8, :] = jnp.zeros((8, D), F32)
        hstate[...] = jnp.zeros(hstate.shape, F32)

    @pl.when(jnp.logical_and(g >= 1, (g - 1) % tiles_per_seq == 0))
    def _():
        xbuf[0:8, :] = mhist[...]
        hstate[...] = mstate[...]

    xbuf[8:, :] = yx[:, D:]
    xc = cb_ref[...] + cw_ref[0:1, :] * xbuf[5:5 + TM, :]
    for j in range(1, CONV_W):
        xc = xc + cw_ref[j:j + 1, :] * xbuf[5 + j:5 + j + TM, :]

    gx, ga = [], []
    for nb in range(LRU_BLOCKS):
        xb = xc[:, nb * LRU_BLOCK_W:(nb + 1) * LRU_BLOCK_W].astype(BF16)
        gx.append(jnp.dot(xb, gw_ref[0, nb], preferred_element_type=F32))
        ga.append(jnp.dot(xb, gw_ref[1, nb], preferred_element_type=F32))
    gate_x = _sigmoid(jnp.concatenate(gx, axis=1) + gb_ref[0:1, :])
    gate_a = _sigmoid(jnp.concatenate(ga, axis=1) + gb_ref[1:2, :])
    lam = lam_ref[...]
    log_sig = jnp.minimum(lam, 0.0) - jnp.log(1.0 + jnp.exp(-jnp.abs(lam)))
    log_a = RG_C * gate_a * log_sig
    a = jnp.exp(log_a)
    a_s[...] = a
    u_s[...] = jnp.sqrt(-jnp.tanh(log_a) * (a * a + 1.0)) * (gate_x * xc)

    seg_rows = TM // SCAN_SEGS

    def step(r, carry):
        hs, ps = carry
        new_h, new_p = [], []
        for s in range(SCAN_SEGS):
            row = pl.ds(s * seg_rows + r, 1)
            a_r = a_s[row, :]
            h_r = a_r * hs[s] + u_s[row, :]
            h_s[row, :] = h_r
            new_h.append(h_r)
            if s > 0:
                p_r = a_r * ps[s - 1]
                p_s[row, :] = p_r
                new_p.append(p_r)
        return tuple(new_h), tuple(new_p)

    zero = jnp.zeros((1, D), F32)
    hs, _ = lax.fori_loop(0, seg_rows, step,
                          ((hstate[...],) + (zero,) * (SCAN_SEGS - 1), (zero + 1.0,) * (SCAN_SEGS - 1)), unroll=4)
    h_in = hs[0]
    for s in range(1, SCAN_SEGS):
        seg = slice(s * seg_rows, (s + 1) * seg_rows)
        h_seg = h_s[seg, :] + p_s[seg, :] * h_in
        h_s[seg, :] = h_seg
        h_in = h_seg[seg_rows - 1:seg_rows, :]
    hstate[...] = h_in

    @pl.when(g == 0)
    def _():
        mstate[...] = h_s[N_META - 1:N_META, :]
        mhist[...] = xbuf[N_META:N_META + 8, :]

    xbuf[0:8, :] = xbuf[TM:TM + 8, :]
    o_ref[...] = x + jnp.dot((h_s[...] * y_br).astype(BF16), wout_ref[...], preferred_element_type=F32)


def _rglru(h, g, w_in, conv_w, conv_b, gate_w, gate_b, lru_lambda, w_out, tiles_per_seq):
    n = h.shape[0]
    nt = n // TM
    tile = lambda s: (jnp.where(s == 0, nt - 1, s - 1), 0)
    const2 = lambda s: (0, 0)
    return pl.pallas_call(
        functools.partial(_rglru_kernel, tiles_per_seq=tiles_per_seq),
        grid=(nt,),
        in_specs=[
            pl.BlockSpec((TM, D), tile),
            _resident((1, D), const2),
            _resident((D, 2 * D), const2),
            _resident((CONV_W, D), const2),
            _resident((1, D), const2),
            _resident((2, LRU_BLOCKS, LRU_BLOCK_W, LRU_BLOCK_W), lambda s: (0, 0, 0, 0)),
            _resident((2, D), const2),
            _resident((1, D), const2),
            _resident((D, D), const2),
        ],
        out_specs=pl.BlockSpec((TM, D), tile),
        out_shape=jax.ShapeDtypeStruct((n, D), F32),
        scratch_shapes=[
            pltpu.VMEM((TM + 8, D), F32),
            pltpu.VMEM((TM, D), F32),
            pltpu.VMEM((TM, D), F32),
            pltpu.VMEM((TM, D), F32),
            pltpu.VMEM((TM, D), F32),
            pltpu.VMEM((1, D), F32),
            pltpu.VMEM((1, D), F32),
            pltpu.VMEM((8, D), F32),
        ],
        compiler_params=_params(),
        name="rglru",
    )(h, g.reshape(1, D), w_in, conv_w, conv_b.reshape(1, D), gate_w, gate_b, lru_lambda.reshape(1, D), w_out)


def _split_attn_w_in(w):
    a = HEAD_W
    o_ki = 4 * a
    o_wi = o_ki + IDX_DIM
    o_qb = o_wi + IDX_HEADS
    qa, ka, va, qi = w[:, 0:a], w[:, a:2 * a], w[:, 2 * a:3 * a], w[:, 3 * a:4 * a]
    ki, wi = w[:, o_ki:o_wi], w[:, o_wi:o_qb]
    qb, kb, vb = w[:, o_qb:o_qb + a], w[:, o_qb + a:o_qb + 2 * a], w[:, o_qb + 2 * a:o_qb + 3 * a]
    w_main = jnp.concatenate([qa, ka, qi, qb, kb, va, vb], axis=1).astype(BF16)
    w_wt = jnp.concatenate([wi.T, jnp.zeros((16 - IDX_HEADS, D), w.dtype)], axis=0).astype(BF16)
    return w_main, ki.astype(BF16), w_wt


def kernel(x, meta_tokens, norm_g, ffn_w_gu, ffn_w_down, attn_w_in, idx_k_ln_g, idx_k_ln_b, diff_lambda, diff_subln_g, attn_w_out, rec_w_in, rec_conv_w, rec_conv_b, rec_gate_w, rec_gate_b, rec_lambda, rec_w_out, final_norm_g):
    batch, seq, _ = x.shape
    depth = norm_g.shape[0]
    n_real = batch * seq
    tail = jnp.concatenate([meta_tokens.astype(x.dtype), jnp.zeros((TM - N_META, D), x.dtype)], axis=0)
    w_gu = ffn_w_gu.astype(BF16)
    w_down = ffn_w_down.astype(BF16)
    h = x.reshape(n_real, D)
    for i in range(depth):
        j = i // 2
        h = _ffn(h, norm_g[i, 0], w_gu, w_down, i, 0, tail=tail if i == 0 else None)
        if i % 2 == 0:
            lam_init = 0.8 - 0.6 * math.exp(-0.3 * i)
            w_main, w_ki, w_wt = _split_attn_w_in(attn_w_in[j])
            proj, ki, wit, vat, vbt = _attn_proj(h, norm_g[i, 1], w_main, w_ki, w_wt, idx_k_ln_g[j], idx_k_ln_b[j])
            oa = _dsa(proj, ki, wit, vat, batch, seq)
            ob = _diff(proj, vbt, diff_lambda[j], diff_subln_g[j], batch, seq, lam_init)
            mixer = (oa, ob, attn_w_out[j].astype(BF16))
        else:
            mixer = None
            h = _rglru(h, norm_g[i, 1], rec_w_in[j].astype(BF16), rec_conv_w[j], rec_conv_b[j],
                       rec_gate_w[j].astype(BF16), rec_gate_b[j], rec_lambda[j], rec_w_out[j].astype(BF16),
                       seq // TM)
        if i == depth - 1:
            h = _ffn(h, norm_g[i, 2], w_gu, w_down, i, 1, mixer=mixer, final_g=final_norm_g, n_rows=n_real)
        else:
            h = _ffn(h, norm_g[i, 2], w_gu, w_down, i, 1, mixer=mixer)
    return h.reshape(batch, seq, D)
```

```python
import functools
import math

import jax
import jax.numpy as jnp
from jax import lax
from jax.experimental import pallas as pl
from jax.experimental.pallas import tpu as pltpu

F32 = jnp.float32
BF16 = jnp.bfloat16
I32 = jnp.int32

D = 1024
N_META = 16
D_FF = 2816
EPS = 1e-6
HEAD_W = 512
IDX_DIM = 64
IDX_HEADS = 8
TOPK_MAX = 256
LRU_BLOCKS = 4
LRU_BLOCK_W = D // LRU_BLOCKS
CONV_W = 4
RG_C = 8.0

TM = 512
TQ = 256
META_KEYS = 128
TK = 512
SCAN_SEGS = 4
MXU_TILE = 256
FF_SPLIT = (D_FF // MXU_TILE + 1) // 2 * MXU_TILE
FF_CHUNKS = ((0, FF_SPLIT), (FF_SPLIT, D_FF))
NEG = -1e30
BIG = 1e30
QK_SCALE = IDX_DIM ** -0.5 * math.log2(math.e)
INT_MIN = -2 ** 31
VMEM_LIMIT = 56 * 1024 * 1024

_NT = (((1,), (1,)), ((), ()))


def _params():
    return pltpu.CompilerParams(dimension_semantics=("arbitrary",), vmem_limit_bytes=VMEM_LIMIT)


def _resident(shape, index_map):
    return pl.BlockSpec(shape, index_map, pipeline_mode=pl.Buffered(1))


def _rms(x, g):
    return x * lax.rsqrt(jnp.mean(x * x, axis=-1, keepdims=True) + EPS) * g


def _sigmoid(x):
    return 0.5 * jnp.tanh(0.5 * x) + 0.5


def _ffn_kernel(*refs, final, n_real_tiles, mixer):
    refs = list(refs)
    o_ref = refs.pop()
    x = refs.pop(0)[...]
    if n_real_tiles is not None:
        x = jnp.where(pl.program_id(0) < n_real_tiles, x, refs.pop(0)[...])
    if mixer:
        oa_ref, ob_ref, wa_ref, wb_ref = refs[:4]
        del refs[:4]
        x = (x + jnp.dot(oa_ref[...], wa_ref[...], preferred_element_type=F32)
             + jnp.dot(ob_ref[...], wb_ref[...], preferred_element_type=F32))
    g_ref, wg_ref, wu_ref, wd_ref = refs[:4]
    xn = _rms(x, g_ref[...]).astype(BF16)
    acc = None
    for lo, hi in FF_CHUNKS:
        sl = slice(lo, hi)
        g = jnp.dot(xn, wg_ref[:, sl], preferred_element_type=F32)
        u = jnp.dot(xn, wu_ref[:, sl], preferred_element_type=F32)
        a = (g * _sigmoid(g) * u).astype(BF16)
        part = jnp.dot(a, wd_ref[sl, :], preferred_element_type=F32)
        acc = part if acc is None else acc + part
    y = x + 0.5 * acc
    if final:
        y = _rms(y, refs[4][...])
    o_ref[...] = y


def _ffn(h, g, w_gu, w_down, layer, half, tail=None, mixer=None, final_g=None, n_rows=None):
    n_real_tiles = None if tail is None else h.shape[0] // TM
    if n_rows is None:
        n_rows = h.shape[0] + (0 if tail is None else TM)
    row_tile = lambda i: (i, 0)
    const = lambda i: (0, 0)
    if tail is None:
        in_specs, args = [pl.BlockSpec((TM, D), row_tile)], [h]
    else:
        in_specs = [pl.BlockSpec((TM, D), lambda i: (jnp.minimum(i, n_real_tiles - 1), 0)), _resident((TM, D), const)]
        args = [h, tail]
    if mixer is not None:
        oa, ob, w_out = mixer
        in_specs += [pl.BlockSpec((TM, HEAD_W), row_tile), pl.BlockSpec((TM, HEAD_W), row_tile),
                     _resident((HEAD_W, D), const), _resident((HEAD_W, D), lambda i: (1, 0))]
        args += [oa, ob, w_out, w_out]
    in_specs += [
        _resident((1, D), const),
        _resident((None, None, D, D_FF), lambda i: (layer, half, 0, 0)),
        _resident((None, None, D, D_FF), lambda i: (layer, half, 0, 1)),
        _resident((None, None, D_FF, D), lambda i: (layer, half, 0, 0)),
    ]
    args += [g.reshape(1, D), w_gu, w_gu, w_down]
    if final_g is not None:
        in_specs.append(_resident((1, D), const))
        args.append(final_g.reshape(1, D))
    return pl.pallas_call(
        functools.partial(_ffn_kernel, final=final_g is not None, n_real_tiles=n_real_tiles,
                          mixer=mixer is not None),
        grid=(n_rows // TM,),
        in_specs=in_specs,
        out_specs=pl.BlockSpec((TM, D), row_tile),
        out_shape=jax.ShapeDtypeStruct((n_rows, D), F32),
        compiler_params=_params(),
        name="ffn",
    )(*args)


N_PROJ = 5 * HEAD_W


def _attn_proj_kernel(x_ref, g_ref, w_ref, wki_ref, wwt_ref, lng_ref, lnb_ref,
                      proj_ref, ki_ref, wit_ref, vat_ref, vbt_ref):
    xn = _rms(x_ref[...], g_ref[...]).astype(BF16)
    y = jnp.dot(xn, w_ref[...], preferred_element_type=F32)
    scale = QK_SCALE
    proj_ref[:, 0:HEAD_W] = (y[:, 0:HEAD_W] * scale).astype(BF16)
    proj_ref[:, HEAD_W:3 * HEAD_W] = y[:, HEAD_W:3 * HEAD_W].astype(BF16)
    proj_ref[:, 3 * HEAD_W:4 * HEAD_W] = (y[:, 3 * HEAD_W:4 * HEAD_W] * scale).astype(BF16)
    proj_ref[:, 4 * HEAD_W:5 * HEAD_W] = y[:, 4 * HEAD_W:5 * HEAD_W].astype(BF16)
    vat_ref[0] = y[:, 5 * HEAD_W:6 * HEAD_W].T.astype(BF16)
    vbt_ref[0] = y[:, 6 * HEAD_W:7 * HEAD_W].T.astype(BF16)
    yk = jnp.dot(xn, wki_ref[...], preferred_element_type=F32)
    mu = jnp.mean(yk, axis=-1, keepdims=True)
    yc = yk - mu
    ln = yc * lax.rsqrt(jnp.mean(yc * yc, axis=-1, keepdims=True) + EPS)
    ki_ref[...] = (ln * lng_ref[...] + lnb_ref[...]).astype(BF16)
    wit = lax.dot_general(wwt_ref[...], xn, _NT, preferred_element_type=F32)
    wit_ref[...] = wit * (IDX_HEADS ** -0.5 * IDX_DIM ** -0.5)


def _attn_proj(h, g, w_main, w_ki, w_wt, ln_g, ln_b):
    n = h.shape[0]
    nt = n // TM
    const = lambda i: (0, 0)
    return pl.pallas_call(
        _attn_proj_kernel,
        grid=(nt,),
        in_specs=[
            pl.BlockSpec((TM, D), lambda i: (i, 0)),
            _resident((1, D), const),
            _resident((D, 7 * HEAD_W), const),
            _resident((D, IDX_DIM), const),
            _resident((16, D), const),
            _resident((1, IDX_DIM), const),
            _resident((1, IDX_DIM), const),
        ],
        out_specs=[
            pl.BlockSpec((TM, N_PROJ), lambda i: (i, 0)),
            pl.BlockSpec((TM, IDX_DIM), lambda i: (i, 0)),
            pl.BlockSpec((16, TM), lambda i: (0, i)),
            pl.BlockSpec((1, HEAD_W, TM), lambda i: (i, 0, 0)),
            pl.BlockSpec((1, HEAD_W, TM), lambda i: (i, 0, 0)),
        ],
        out_shape=[
            jax.ShapeDtypeStruct((n, N_PROJ), BF16),
            jax.ShapeDtypeStruct((n, IDX_DIM), BF16),
            jax.ShapeDtypeStruct((16, n), F32),
            jax.ShapeDtypeStruct((nt, HEAD_W, TM), BF16),
            jax.ShapeDtypeStruct((nt, HEAD_W, TM), BF16),
        ],
        compiler_params=_params(),
        name="attn_proj",
    )(h, g.reshape(1, D), w_main, w_ki, w_wt, ln_g.reshape(1, IDX_DIM), ln_b.reshape(1, IDX_DIM))


def _step_geometry(g, seq):
    nq_seq = seq // TQ
    n_real = pl.num_programs(0) - TM // TQ
    is_real = g < n_real
    i = g % nq_seq
    n_ch = jnp.where(is_real, (i * TQ + TQ + TK - 1) // TK, 0)
    return is_real, i, n_ch


def _stack_pair(q_pair):
    lane = lax.broadcasted_iota(I32, (TQ, 128), 1)
    qf = q_pair.astype(F32)
    top = jnp.where(lane < 64, qf, 0.0).astype(BF16)
    bot = jnp.where(lane >= 64, qf, 0.0).astype(BF16)
    return jnp.concatenate([top, bot], axis=0)


def _meta_bias(is_real):
    j = lax.broadcasted_iota(I32, (META_KEYS, 2 * TQ), 0)
    t = lax.broadcasted_iota(I32, (META_KEYS, 2 * TQ), 1) % TQ
    real = is_real.astype(I32)
    lim = jnp.minimum(real * N_META + (1 - real) * (t + 1), N_META)
    return jnp.where(j < lim, BIG, NEG)


def _bcast_halves(row, n_sub):
    return jnp.concatenate([jnp.broadcast_to(row[:, :TQ], (n_sub, TQ)),
                            jnp.broadcast_to(row[:, TQ:], (n_sub, TQ))], axis=0)


SCORE_LOOKAHEAD = 3
MAX_LAG = 64.0


def _softmax_pv(segments, v_tiles, base):
    issued, out = {}, []
    for h in range(SCORE_LOOKAHEAD):
        issued[h] = [fn() for fn in segments[h]]
    for h in range(4):
        if h + SCORE_LOOKAHEAD < 4:
            issued[h + SCORE_LOOKAHEAD] = [fn() for fn in segments[h + SCORE_LOOKAHEAD]]
        sts = issued.pop(h)
        m_loc = jnp.max(sts[0], axis=0, keepdims=True)
        for st in sts[1:]:
            m_loc = jnp.maximum(m_loc, jnp.max(st, axis=0, keepdims=True))
        m_use = base(h, m_loc)
        l_new, pv = None, None
        for st, vt in zip(sts, v_tiles[h]):
            pc = jnp.exp2(st - m_use)
            s = jnp.sum(pc, axis=0, keepdims=True)
            d = jnp.dot(vt(), pc.astype(BF16), preferred_element_type=F32)
            l_new = s if l_new is None else l_new + s
            pv = d if pv is None else pv + d
        out.append((m_loc, l_new, pv))
    return out


def _attend(segments, v_tiles, m_ref, l_ref, acc_ref, fold, first):
    fold_pv, fold_row = fold
    if first:
        for h, (m_loc, l_new, pv) in enumerate(_softmax_pv(segments, v_tiles, lambda h, m_loc: m_loc)):
            m_ref[h] = m_loc
            l_ref[h] = l_new
            acc_ref[h] = fold_pv(pv)
        return

    def commit(parts, rebased):
        for h, (m_loc, l_new, pv) in enumerate(parts):
            m_new = jnp.maximum(m_ref[h], m_loc)
            alpha = jnp.exp2(m_ref[h] - m_new)
            if rebased:
                l_ref[h] = alpha * l_ref[h] + l_new
                acc_ref[h] = acc_ref[h] * fold_row(alpha) + fold_pv(pv)
            else:
                l_ref[h] = alpha * (l_ref[h] + l_new)
                acc_ref[h] = (acc_ref[h] + fold_pv(pv)) * fold_row(alpha)
            m_ref[h] = m_new

    parts = _softmax_pv(segments, v_tiles, lambda h, m_loc: m_ref[h])
    rise = parts[0][0] - m_ref[0]
    for h in range(1, 4):
        rise = jnp.maximum(rise, parts[h][0] - m_ref[h])
    safe = jnp.max(rise) <= MAX_LAG

    @pl.when(safe)
    def _():
        commit(parts, rebased=False)

    @pl.when(jnp.logical_not(safe))
    def _():
        commit(_softmax_pv(segments, v_tiles, lambda h, m_loc: jnp.maximum(m_ref[h], m_loc)), rebased=True)


GROUP_KEYS = 256
GROUPS_PER_CHUNK = TK // GROUP_KEYS


def _bit_transpose32(words):
    a = list(words)
    j, mask = 16, 0x0000FFFF
    while j:
        k = 0
        while k < 32:
            t = (a[k] ^ lax.shift_right_logical(a[k + j], jnp.int32(j))) & mask
            a[k] = a[k] ^ t
            a[k + j] = a[k + j] ^ lax.shift_left(t, jnp.int32(j))
            k = (k + j + 1) & ~j
        j >>= 1
        mask ^= mask << j
    return a


def _dsa_kernel(qa_ref, qi_ref, wit_ref, ki_ref, ka_ref, vat_ref, km_ref, vmt_ref, o_ref,
                qall_ref, qstk_ref, key_ref, plane_ref, bias_ref, tri_ref, sel_ref, m_ref, l_ref, acc_ref,
                *, seq, top_k):
    g = pl.program_id(0)

    @pl.when(g == 0)
    def _():
        plane_ref[...] = jnp.zeros(plane_ref.shape, I32)
        r = lax.broadcasted_iota(I32, (TK, TK), 0)
        c_ = lax.broadcasted_iota(I32, (TK, TK), 1)
        tri_ref[...] = jnp.where(c_ <= r, 1.0, 0.0).astype(BF16)

    is_real, i, n_ch = _step_geometry(g, seq)
    q_pos = i * TQ + lax.broadcasted_iota(I32, (TK, TQ), 1)
    k_off = lax.broadcasted_iota(I32, (TK, TQ), 0)

    qi = qi_ref[...]
    for h in range(IDX_HEADS):
        qall_ref[h * TQ:(h + 1) * TQ, :] = qi[:, h * IDX_DIM:(h + 1) * IDX_DIM]
    for p in range(4):
        qstk_ref[p] = _stack_pair(qa_ref[:, p * 128:(p + 1) * 128])

    wt = wit_ref[...]

    def score_body(c, carry):
        kc = ki_ref[pl.ds(pl.multiple_of(c * TK, TK), TK), :]
        logits = lax.dot_general(kc, qall_ref[...], _NT, preferred_element_type=F32)
        sc = jnp.zeros((TK, TQ), F32)
        for h in range(IDX_HEADS):
            sc = sc + wt[h:h + 1, :] * jnp.maximum(logits[:, h * TQ:(h + 1) * TQ], 0.0)
        sc = jnp.where(sc == 0.0, 0.0, sc)
        bits = pltpu.bitcast(sc, I32)
        key = bits ^ (lax.shift_right_arithmetic(bits, 31) & 0x7FFFFFFF)
        key = jnp.where(c * TK + k_off <= q_pos, key, INT_MIN)
        key_ref[c] = key
        u = key ^ INT_MIN
        for gi in range(GROUPS_PER_CHUNK):
            base = gi * GROUP_KEYS
            planes = _bit_transpose32([u[base + w * 8:base + (w + 1) * 8, :] for w in range(32)])
            for b in range(32):
                plane_ref[c * GROUPS_PER_CHUNK + gi, b] = planes[b]
        return carry

    lax.fori_loop(0, n_ch, score_body, 0)

    n_grp = n_ch * GROUPS_PER_CHUNK

    def radix_select(n_groups):
        alive0 = tuple(jnp.full((8, TQ), -1, I32) * (gq < n_grp).astype(I32) for gq in range(n_groups))

        def bit_body(b, carry):
            alive, k_rem, thr_u = carry
            ones = [a & plane_ref[gq, b] for gq, a in enumerate(alive)]
            cnt8 = lax.population_count(ones[0])
            for o in ones[1:]:
                cnt8 = cnt8 + lax.population_count(o)
            cnt = jnp.sum(cnt8, axis=0, keepdims=True)
            take = cnt >= k_rem
            alive = tuple(jnp.where(take, o, a ^ o) for a, o in zip(alive, ones))
            k_rem = jnp.where(take, k_rem, k_rem - cnt)
            thr_u = thr_u | jnp.where(take, lax.shift_left(jnp.int32(1), 31 - b), 0)
            return alive, k_rem, thr_u

        alive, k_rem, thr_u = lax.fori_loop(
            0, 32, bit_body, (alive0, jnp.full((1, TQ), top_k, I32), jnp.zeros((1, TQ), I32)))
        eq8 = lax.population_count(alive[0])
        for a in alive[1:]:
            eq8 = eq8 + lax.population_count(a)
        sel_ref[0:1, :] = thr_u
        sel_ref[1:2, :] = k_rem
        sel_ref[2:3, :] = jnp.sum(eq8, axis=0, keepdims=True)

    all_groups = plane_ref.shape[0]

    @pl.when(n_grp <= all_groups // 2)
    def _():
        radix_select(all_groups // 2)

    @pl.when(n_grp > all_groups // 2)
    def _():
        radix_select(all_groups)

    thr_u, k_rem, cnt_eq = sel_ref[0:1, :], sel_ref[1:2, :], sel_ref[2:3, :]
    thr = thr_u ^ INT_MIN
    excess = jnp.where(thr_u != 0, cnt_eq - k_rem, 0)
    has_excess = jnp.max(excess) > 0

    @pl.when(jnp.logical_not(has_excess))
    def _():
        thr_eff = jnp.maximum(thr, INT_MIN + 1)

        def cap_body(c, carry):
            bias_ref[c] = jnp.where(key_ref[c] >= thr_eff, BIG, NEG)
            return carry

        lax.fori_loop(0, n_ch, cap_body, 0)

    @pl.when(has_excess)
    def _():
        need = k_rem.astype(F32)

        def cap_body(c, run):
            k = key_ref[c]
            eq = k == thr
            pre = jnp.dot(tri_ref[...], jnp.where(eq, 1.0, 0.0).astype(BF16), preferred_element_type=F32)
            rank = run + pre
            b = jnp.where(k > thr, BIG, jnp.where(eq, jnp.where(rank <= need, BIG, NEG), NEG))
            bias_ref[c] = jnp.where(k == INT_MIN, NEG, b)
            return run + pre[TK - 1:TK, :]

        lax.fori_loop(0, n_ch, cap_body, jnp.zeros((1, TQ), F32))

    fold = (lambda ot: jnp.concatenate([ot[:64, :TQ], ot[64:, TQ:]], axis=0), lambda row: _bcast_halves(row, 64))
    pairs = [slice(p * 128, (p + 1) * 128) for p in range(4)]

    def score(p, k_ref, idx, cap):
        return lambda: jnp.minimum(lax.dot_general(k_ref[idx], qstk_ref[p], _NT, preferred_element_type=F32), cap)

    def tile(ref, idx):
        return lambda: ref[idx]

    mcap = _meta_bias(is_real)
    _attend([[score(p, km_ref, (slice(None), ps), mcap)] for p, ps in enumerate(pairs)],
            [[tile(vmt_ref, (0, ps, slice(0, META_KEYS)))] for ps in pairs],
            m_ref, l_ref, acc_ref, fold, first=True)

    def update(chunks):
        rows = [pl.ds(pl.multiple_of(c * TK, TK), TK) for c in chunks]
        caps = [jnp.concatenate([bias_ref[c]] * 2, axis=1) for c in chunks]
        _attend([[score(p, ka_ref, (row, ps), cap2) for row, cap2 in zip(rows, caps)] for p, ps in enumerate(pairs)],
                [[tile(vat_ref, (c, ps, slice(None))) for c in chunks] for ps in pairs],
                m_ref, l_ref, acc_ref, fold, first=False)

    def pair_body(j, carry):
        update([2 * j, 2 * j + 1])
        return carry

    lax.fori_loop(0, n_ch // 2, pair_body, 0)

    @pl.when(n_ch % 2 == 1)
    def _():
        update([n_ch - 1])

    for p, ps in enumerate(pairs):
        o_ref[:, ps] = (acc_ref[p] * _bcast_halves(1.0 / l_ref[p], 64)).T.astype(BF16)


def _dsa(proj, ki, wit, vat, batch, seq):
    n = proj.shape[0]
    nq_seq = seq // TQ
    n_real_q = batch * nq_seq
    nch = seq // TK
    top_k = min(TOPK_MAX, seq // 4)
    bidx = lambda g: jnp.minimum(g // nq_seq, batch - 1)
    meta_k = batch * seq // META_KEYS
    meta_tile = batch * nch
    return pl.pallas_call(
        functools.partial(_dsa_kernel, seq=seq, top_k=top_k),
        grid=(n // TQ,),
        in_specs=[
            pl.BlockSpec((TQ, HEAD_W), lambda g: (g, 0)),
            pl.BlockSpec((TQ, HEAD_W), lambda g: (g, 2)),
            pl.BlockSpec((16, TQ), lambda g: (0, g)),
            pl.BlockSpec((seq, IDX_DIM), lambda g: (bidx(g), 0)),
            pl.BlockSpec((seq, HEAD_W), lambda g: (bidx(g), 1)),
            pl.BlockSpec((nch, HEAD_W, TK), lambda g: (bidx(g), 0, 0)),
            pl.BlockSpec((META_KEYS, HEAD_W), lambda g: (meta_k, 1)),
            pl.BlockSpec((1, HEAD_W, TK), lambda g: (meta_tile, 0, 0)),
        ],
        out_specs=pl.BlockSpec((TQ, HEAD_W), lambda g: (g, 0)),
        out_shape=jax.ShapeDtypeStruct((n, HEAD_W), BF16),
        scratch_shapes=[
            pltpu.VMEM((IDX_HEADS * TQ, IDX_DIM), BF16),
            pltpu.VMEM((4, 2 * TQ, 128), BF16),
            pltpu.VMEM((nch, TK, TQ), I32),
            pltpu.VMEM((nch * GROUPS_PER_CHUNK, 32, 8, TQ), I32),
            pltpu.VMEM((nch, TK, TQ), F32),
            pltpu.VMEM((TK, TK), BF16),
            pltpu.VMEM((8, TQ), I32),
            pltpu.VMEM((4, 1, 2 * TQ), F32),
            pltpu.VMEM((4, 1, 2 * TQ), F32),
            pltpu.VMEM((4, 128, TQ), F32),
        ],
        compiler_params=_params(),
        name="dsa",
    )(proj, proj, wit, ki, proj, vat, proj, vat)


def _diff_kernel(lam_ref, g_ref, qb_ref, kb_ref, vbt_ref, km_ref, vmt_ref, o_ref,
                 qstk_ref, m_ref, l_ref, acc_ref, *, seq, lam_init):
    g = pl.program_id(0)
    is_real, i, n_ch = _step_geometry(g, seq)
    lp = lam_ref[...]
    lam = (jnp.exp(jnp.sum(lp[0:1] * lp[1:2], axis=1, keepdims=True))
           - jnp.exp(jnp.sum(lp[2:3] * lp[3:4], axis=1, keepdims=True)) + lam_init)
    heads = [slice(h * 128, (h + 1) * 128) for h in range(4)]
    for h, hs in enumerate(heads):
        qstk_ref[h] = _stack_pair(qb_ref[:, hs])

    fold = (lambda ot: ot, lambda row: row)

    def score(h, k_ref, idx, cap=None):
        def fn():
            st = lax.dot_general(k_ref[idx], qstk_ref[h], _NT, preferred_element_type=F32)
            return st if cap is None else jnp.minimum(st, cap)
        return fn

    def tile(ref, idx):
        return lambda: ref[idx]

    n_full = n_ch - is_real.astype(I32)
    mcap = _meta_bias(is_real)
    _attend([[score(h, km_ref, (slice(None), hs), mcap)] for h, hs in enumerate(heads)],
            [[tile(vmt_ref, (0, hs, slice(0, META_KEYS)))] for hs in heads],
            m_ref, l_ref, acc_ref, fold, first=True)

    def update(chunks, caps):
        rows = [pl.ds(pl.multiple_of(c * TK, TK), TK) for c in chunks]
        _attend([[score(h, kb_ref, (row, hs), cap) for row, cap in zip(rows, caps)] for h, hs in enumerate(heads)],
                [[tile(vbt_ref, (c, hs, slice(None))) for c in chunks] for hs in heads],
                m_ref, l_ref, acc_ref, fold, first=False)

    def pair_body(j, carry):
        update([2 * j, 2 * j + 1], [None, None])
        return carry

    lax.fori_loop(0, n_full // 2, pair_body, 0)

    q_pos = i * TQ + lax.broadcasted_iota(I32, (TK, 2 * TQ), 1) % TQ
    k_pos = n_full * TK + lax.broadcasted_iota(I32, (TK, 2 * TQ), 0)
    cap = jnp.where(jnp.where(k_pos <= q_pos, is_real.astype(I32), 0) > 0, BIG, NEG)

    @pl.when(n_full % 2 == 1)
    def _():
        update([n_full - 1, n_full], [None, cap])

    @pl.when(n_full % 2 == 0)
    def _():
        update([n_full], [cap])

    for h, hs in enumerate(heads):
        on = acc_ref[h] * (1.0 / l_ref[h])
        o = on[:, :TQ] - lam * on[:, TQ:]
        y = o * lax.rsqrt(jnp.mean(o * o, axis=0, keepdims=True) + EPS) * g_ref[...] * (1.0 - lam_init)
        o_ref[:, hs] = y.T.astype(BF16)


def _diff(proj, vbt, lam_params, subln_g, batch, seq, lam_init):
    n = proj.shape[0]
    nq_seq = seq // TQ
    nch = seq // TK
    bidx = lambda g: jnp.minimum(g // nq_seq, batch - 1)
    meta_k = batch * seq // META_KEYS
    meta_tile = batch * nch
    return pl.pallas_call(
        functools.partial(_diff_kernel, seq=seq, lam_init=lam_init),
        grid=(n // TQ,),
        in_specs=[
            _resident((4, 64), lambda g: (0, 0)),
            _resident((128, 1), lambda g: (0, 0)),
            pl.BlockSpec((TQ, HEAD_W), lambda g: (g, 3)),
            pl.BlockSpec((seq, HEAD_W), lambda g: (bidx(g), 4)),
            pl.BlockSpec((nch, HEAD_W, TK), lambda g: (bidx(g), 0, 0)),
            pl.BlockSpec((META_KEYS, HEAD_W), lambda g: (meta_k, 4)),
            pl.BlockSpec((1, HEAD_W, TK), lambda g: (meta_tile, 0, 0)),
        ],
        out_specs=pl.BlockSpec((TQ, HEAD_W), lambda g: (g, 0)),
        out_shape=jax.ShapeDtypeStruct((n, HEAD_W), BF16),
        scratch_shapes=[
            pltpu.VMEM((4, 2 * TQ, 128), BF16),
            pltpu.VMEM((4, 1, 2 * TQ), F32),
            pltpu.VMEM((4, 1, 2 * TQ), F32),
            pltpu.VMEM((4, 128, 2 * TQ), F32),
        ],
        compiler_params=_params(),
        name="diff_attn",
    )(lam_params, subln_g.reshape(128, 1), proj, proj, vbt, proj, vbt)


def _rglru_kernel(x_ref, g_ref, win_ref, cw_ref, cb_ref, gw_ref, gb_ref, lam_ref, wout_ref, o_ref,
                  xbuf, a_s, u_s, h_s, p_s, hstate, mstate, mhist, *, tiles_per_seq):
    g = pl.program_id(0)
    x = x_ref[...]
    xn = _rms(x, g_ref[...]).astype(BF16)
    yx = jnp.dot(xn, win_ref[...], preferred_element_type=F32)
    yb = yx[:, :D]
    y_br = 0.5 * yb * (1.0 + jnp.tanh(math.sqrt(2.0 / math.pi) * (yb + 0.044715 * (yb * yb * yb))))

    @pl.when(g == 0)
    def _():
        xbuf[0:8, :] = jnp.zeros((8, D), F32)
        hstate[...] = jnp.zeros(hstate.shape, F32)

    @pl.when(jnp.logical_and(g >= 1, (g - 1) % tiles_per_seq == 0))
    def _():
        xbuf[0:8, :] = mhist[...]
        hstate[...] = mstate[...]

    xbuf[8:, :] = yx[:, D:]
    xc = cb_ref[...] + cw_ref[0:1, :] * xbuf[5:5 + TM, :]
    for j in range(1, CONV_W):
        xc = xc + cw_ref[j:j + 1, :] * xbuf[5 + j:5 + j + TM, :]

    gx, ga = [], []
    for nb in range(LRU_BLOCKS):
        xb = xc[:, nb * LRU_BLOCK_W:(nb + 1) * LRU_BLOCK_W].astype(BF16)
        gx.append(jnp.dot(xb, gw_ref[0, nb], preferred_element_type=F32))
        ga.append(jnp.dot(xb, gw_ref[1, nb], preferred_element_type=F32))
    gate_x = _sigmoid(jnp.concatenate(gx, axis=1) + gb_ref[0:1, :])
    gate_a = _sigmoid(jnp.concatenate(ga, axis=1) + gb_ref[1:2, :])
    lam = lam_ref[...]
    log_sig = jnp.minimum(lam, 0.0) - jnp.log(1.0 + jnp.exp(-jnp.abs(lam)))
    log_a = RG_C * gate_a * log_sig
    a = jnp.exp(log_a)
    a_s[...] = a
    u_s[...] = jnp.sqrt(-jnp.tanh(log_a) * (a * a + 1.0)) * (gate_x * xc)

    seg_rows = TM // SCAN_SEGS

    def step(r, carry):
        hs, ps = carry
        new_h, new_p = [], []
        for s in range(SCAN_SEGS):
            row = pl.ds(s * seg_rows + r, 1)
            a_r = a_s[row, :]
            h_r = a_r * hs[s] + u_s[row, :]
            h_s[row, :] = h_r
            new_h.append(h_r)
            if s > 0:
                p_r = a_r * ps[s - 1]
                p_s[row, :] = p_r
                new_p.append(p_r)
        return tuple(new_h), tuple(new_p)

    zero = jnp.zeros((1, D), F32)
    hs, _ = lax.fori_loop(0, seg_rows, step,
                          ((hstate[...],) + (zero,) * (SCAN_SEGS - 1), (zero + 1.0,) * (SCAN_SEGS - 1)), unroll=4)
    h_in = hs[0]
    for s in range(1, SCAN_SEGS):
        seg = slice(s * seg_rows, (s + 1) * seg_rows)
        h_seg = h_s[seg, :] + p_s[seg, :] * h_in
        h_s[seg, :] = h_seg
        h_in = h_seg[seg_rows - 1:seg_rows, :]
    hstate[...] = h_in

    @pl.when(g == 0)
    def _():
        mstate[...] = h_s[N_META - 1:N_META, :]
        mhist[...] = xbuf[N_META:N_META + 8, :]

    xbuf[0:8, :] = xbuf[TM:TM + 8, :]
    o_ref[...] = x + jnp.dot((h_s[...] * y_br).astype(BF16), wout_ref[...], preferred_element_type=F32)


def _rglru(h, g, w_in, conv_w, conv_b, gate_w, gate_b, lru_lambda, w_out, tiles_per_seq):
    n = h.shape[0]
    nt = n // TM
    tile = lambda s: (jnp.where(s == 0, nt - 1, s - 1), 0)
    const2 = lambda s: (0, 0)
    return pl.pallas_call(
        functools.partial(_rglru_kernel, tiles_per_seq=tiles_per_seq),
        grid=(nt,),
        in_specs=[
            pl.BlockSpec((TM, D), tile),
            _resident((1, D), const2),
            _resident((D, 2 * D), const2),
            _resident((CONV_W, D), const2),
            _resident((1, D), const2),
            _resident((2, LRU_BLOCKS, LRU_BLOCK_W, LRU_BLOCK_W), lambda s: (0, 0, 0, 0)),
            _resident((2, D), const2),
            _resident((1, D), const2),
            _resident((D, D), const2),
        ],
        out_specs=pl.BlockSpec((TM, D), tile),
        out_shape=jax.ShapeDtypeStruct((n, D), F32),
        scratch_shapes=[
            pltpu.VMEM((TM + 8, D), F32),
            pltpu.VMEM((TM, D), F32),
            pltpu.VMEM((TM, D), F32),
            pltpu.VMEM((TM, D), F32),
            pltpu.VMEM((TM, D), F32),
            pltpu.VMEM((1, D), F32),
            pltpu.VMEM((1, D), F32),
            pltpu.VMEM((8, D), F32),
        ],
        compiler_params=_params(),
        name="rglru",
    )(h, g.reshape(1, D), w_in, conv_w, conv_b.reshape(1, D), gate_w, gate_b, lru_lambda.reshape(1, D), w_out)


def _split_attn_w_in(w):
    a = HEAD_W
    o_ki = 4 * a
    o_wi = o_ki + IDX_DIM
    o_qb = o_wi + IDX_HEADS
    qa, ka, va, qi = w[:, 0:a], w[:, a:2 * a], w[:, 2 * a:3 * a], w[:, 3 * a:4 * a]
    ki, wi = w[:, o_ki:o_wi], w[:, o_wi:o_qb]
    qb, kb, vb = w[:, o_qb:o_qb + a], w[:, o_qb + a:o_qb + 2 * a], w[:, o_qb + 2 * a:o_qb + 3 * a]
    w_main = jnp.concatenate([qa, ka, qi, qb, kb, va, vb], axis=1).astype(BF16)
    w_wt = jnp.concatenate([wi.T, jnp.zeros((16 - IDX_HEADS, D), w.dtype)], axis=0).astype(BF16)
    return w_main, ki.astype(BF16), w_wt


def kernel(x, meta_tokens, norm_g, ffn_w_gu, ffn_w_down, attn_w_in, idx_k_ln_g, idx_k_ln_b, diff_lambda, diff_subln_g, attn_w_out, rec_w_in, rec_conv_w, rec_conv_b, rec_gate_w, rec_gate_b, rec_lambda, rec_w_out, final_norm_g):
    batch, seq, _ = x.shape
    depth = norm_g.shape[0]
    n_real = batch * seq
    tail = jnp.concatenate([meta_tokens.astype(x.dtype), jnp.zeros((TM - N_META, D), x.dtype)], axis=0)
    w_gu = ffn_w_gu.astype(BF16)
    w_down = ffn_w_down.astype(BF16)
    h = x.reshape(n_real, D)
    for i in range(depth):
        j = i // 2
        h = _ffn(h, norm_g[i, 0], w_gu, w_down, i, 0, tail=tail if i == 0 else None)
        if i % 2 == 0:
            lam_init = 0.8 - 0.6 * math.exp(-0.3 * i)
            w_main, w_ki, w_wt = _split_attn_w_in(attn_w_in[j])
            proj, ki, wit, vat, vbt = _attn_proj(h, norm_g[i, 1], w_main, w_ki, w_wt, idx_k_ln_g[j], idx_k_ln_b[j])
            oa = _dsa(proj, ki, wit, vat, batch, seq)
            ob = _diff(proj, vbt, diff_lambda[j], diff_subln_g[j], batch, seq, lam_init)
            mixer = (oa, ob, attn_w_out[j].astype(BF16))
        else:
            mixer = None
            h = _rglru(h, norm_g[i, 1], rec_w_in[j].astype(BF16), rec_conv_w[j], rec_conv_b[j],
                       rec_gate_w[j].astype(BF16), rec_gate_b[j], rec_lambda[j], rec_w_out[j].astype(BF16),
                       seq // TM)
        if i == depth - 1:
            h = _ffn(h, norm_g[i, 2], w_gu, w_down, i, 1, mixer=mixer, final_g=final_norm_g, n_rows=n_real)
        else:
            h = _ffn(h, norm_g[i, 2], w_gu, w_down, i, 1, mixer=mixer)
    return h.reshape(batch, seq, D)
```

```python
import functools
import math

import jax
import jax.numpy as jnp
from jax import lax
from jax.experimental import pallas as pl
from jax.experimental.pallas import tpu as pltpu

F32 = jnp.float32
BF16 = jnp.bfloat16
I32 = jnp.int32

D = 1024
N_META = 16
D_FF = 2816
EPS = 1e-6
HEAD_W = 512
IDX_DIM = 64
IDX_HEADS = 8
TOPK_MAX = 256
LRU_BLOCKS = 4
LRU_BLOCK_W = D // LRU_BLOCKS
CONV_W = 4
RG_C = 8.0

TM = 512
TQ = 256
META_KEYS = 16
TK = 512
SCAN_SEGS = 4
MXU_TILE = 256
FF_SPLIT = (D_FF // MXU_TILE + 1) // 2 * MXU_TILE
FF_CHUNKS = ((0, FF_SPLIT), (FF_SPLIT, D_FF))
NEG = -1e30
BIG = 1e30
QK_SCALE = IDX_DIM ** -0.5 * math.log2(math.e)
INT_MIN = -2 ** 31
VMEM_LIMIT = 56 * 1024 * 1024

_NT = (((1,), (1,)), ((), ()))


def _params():
    return pltpu.CompilerParams(dimension_semantics=("arbitrary",), vmem_limit_bytes=VMEM_LIMIT)


def _resident(shape, index_map):
    return pl.BlockSpec(shape, index_map, pipeline_mode=pl.Buffered(1))


def _rms(x, g):
    return x * lax.rsqrt(jnp.mean(x * x, axis=-1, keepdims=True) + EPS) * g


def _sigmoid(x):
    return 0.5 * jnp.tanh(0.5 * x) + 0.5


def _ffn_kernel(*refs, final, n_real_tiles, mixer):
    refs = list(refs)
    o_ref = refs.pop()
    x = refs.pop(0)[...]
    if n_real_tiles is not None:
        x = jnp.where(pl.program_id(0) < n_real_tiles, x, refs.pop(0)[...])
    if mixer:
        oa_ref, ob_ref, wa_ref, wb_ref = refs[:4]
        del refs[:4]
        x = (x + jnp.dot(oa_ref[...], wa_ref[...], preferred_element_type=F32)
             + jnp.dot(ob_ref[...], wb_ref[...], preferred_element_type=F32))
    g_ref, wg_ref, wu_ref, wd_ref = refs[:4]
    xn = _rms(x, g_ref[...]).astype(BF16)
    acc = None
    for lo, hi in FF_CHUNKS:
        sl = slice(lo, hi)
        g = jnp.dot(xn, wg_ref[:, sl], preferred_element_type=F32)
        u = jnp.dot(xn, wu_ref[:, sl], preferred_element_type=F32)
        a = (g * _sigmoid(g) * u).astype(BF16)
        part = jnp.dot(a, wd_ref[sl, :], preferred_element_type=F32)
        acc = part if acc is None else acc + part
    y = x + 0.5 * acc
    if final:
        y = _rms(y, refs[4][...])
    o_ref[...] = y


def _ffn(h, g, w_gu, w_down, layer, half, tail=None, mixer=None, final_g=None, n_rows=None):
    n_real_tiles = None if tail is None else h.shape[0] // TM
    if n_rows is None:
        n_rows = h.shape[0] + (0 if tail is None else TM)
    row_tile = lambda i: (i, 0)
    const = lambda i: (0, 0)
    if tail is None:
        in_specs, args = [pl.BlockSpec((TM, D), row_tile)], [h]
    else:
        in_specs = [pl.BlockSpec((TM, D), lambda i: (jnp.minimum(i, n_real_tiles - 1), 0)), _resident((TM, D), const)]
        args = [h, tail]
    if mixer is not None:
        oa, ob, w_out = mixer
        in_specs += [pl.BlockSpec((TM, HEAD_W), row_tile), pl.BlockSpec((TM, HEAD_W), row_tile),
                     _resident((HEAD_W, D), const), _resident((HEAD_W, D), lambda i: (1, 0))]
        args += [oa, ob, w_out, w_out]
    in_specs += [
        _resident((1, D), const),
        _resident((None, None, D, D_FF), lambda i: (layer, half, 0, 0)),
        _resident((None, None, D, D_FF), lambda i: (layer, half, 0, 1)),
        _resident((None, None, D_FF, D), lambda i: (layer, half, 0, 0)),
    ]
    args += [g.reshape(1, D), w_gu, w_gu, w_down]
    if final_g is not None:
        in_specs.append(_resident((1, D), const))
        args.append(final_g.reshape(1, D))
    return pl.pallas_call(
        functools.partial(_ffn_kernel, final=final_g is not None, n_real_tiles=n_real_tiles,
                          mixer=mixer is not None),
        grid=(n_rows // TM,),
        in_specs=in_specs,
        out_specs=pl.BlockSpec((TM, D), row_tile),
        out_shape=jax.ShapeDtypeStruct((n_rows, D), F32),
        compiler_params=_params(),
        name="ffn",
    )(*args)


N_PROJ = 5 * HEAD_W


def _attn_proj_kernel(x_ref, g_ref, w_ref, wki_ref, wwt_ref, lng_ref, lnb_ref,
                      proj_ref, ki_ref, wit_ref, vat_ref, vbt_ref):
    xn = _rms(x_ref[...], g_ref[...]).astype(BF16)
    y = jnp.dot(xn, w_ref[...], preferred_element_type=F32)
    scale = QK_SCALE
    proj_ref[:, 0:HEAD_W] = (y[:, 0:HEAD_W] * scale).astype(BF16)
    proj_ref[:, HEAD_W:3 * HEAD_W] = y[:, HEAD_W:3 * HEAD_W].astype(BF16)
    proj_ref[:, 3 * HEAD_W:4 * HEAD_W] = (y[:, 3 * HEAD_W:4 * HEAD_W] * scale).astype(BF16)
    proj_ref[:, 4 * HEAD_W:5 * HEAD_W] = y[:, 4 * HEAD_W:5 * HEAD_W].astype(BF16)
    vat_ref[0] = y[:, 5 * HEAD_W:6 * HEAD_W].T.astype(BF16)
    vbt_ref[0] = y[:, 6 * HEAD_W:7 * HEAD_W].T.astype(BF16)
    yk = jnp.dot(xn, wki_ref[...], preferred_element_type=F32)
    mu = jnp.mean(yk, axis=-1, keepdims=True)
    yc = yk - mu
    ln = yc * lax.rsqrt(jnp.mean(yc * yc, axis=-1, keepdims=True) + EPS)
    ki_ref[...] = (ln * lng_ref[...] + lnb_ref[...]).astype(BF16)
    wit = lax.dot_general(wwt_ref[...], xn, _NT, preferred_element_type=F32)
    wit_ref[...] = wit * (IDX_HEADS ** -0.5 * IDX_DIM ** -0.5)


def _attn_proj(h, g, w_main, w_ki, w_wt, ln_g, ln_b):
    n = h.shape[0]
    nt = n // TM
    const = lambda i: (0, 0)
    return pl.pallas_call(
        _attn_proj_kernel,
        grid=(nt,),
        in_specs=[
            pl.BlockSpec((TM, D), lambda i: (i, 0)),
            _resident((1, D), const),
            _resident((D, 7 * HEAD_W), const),
            _resident((D, IDX_DIM), const),
            _resident((16, D), const),
            _resident((1, IDX_DIM), const),
            _resident((1, IDX_DIM), const),
        ],
        out_specs=[
            pl.BlockSpec((TM, N_PROJ), lambda i: (i, 0)),
            pl.BlockSpec((TM, IDX_DIM), lambda i: (i, 0)),
            pl.BlockSpec((16, TM), lambda i: (0, i)),
            pl.BlockSpec((1, HEAD_W, TM), lambda i: (i, 0, 0)),
            pl.BlockSpec((1, HEAD_W, TM), lambda i: (i, 0, 0)),
        ],
        out_shape=[
            jax.ShapeDtypeStruct((n, N_PROJ), BF16),
            jax.ShapeDtypeStruct((n, IDX_DIM), BF16),
            jax.ShapeDtypeStruct((16, n), F32),
            jax.ShapeDtypeStruct((nt, HEAD_W, TM), BF16),
            jax.ShapeDtypeStruct((nt, HEAD_W, TM), BF16),
        ],
        compiler_params=_params(),
        name="attn_proj",
    )(h, g.reshape(1, D), w_main, w_ki, w_wt, ln_g.reshape(1, IDX_DIM), ln_b.reshape(1, IDX_DIM))


def _step_geometry(g, seq):
    nq_seq = seq // TQ
    n_real = pl.num_programs(0) - TM // TQ
    is_real = g < n_real
    i = g % nq_seq
    n_ch = jnp.where(is_real, (i * TQ + TQ + TK - 1) // TK, 0)
    return is_real, i, n_ch


def _stack_pair(q_pair):
    lane = lax.broadcasted_iota(I32, (TQ, 128), 1)
    qf = q_pair.astype(F32)
    top = jnp.where(lane < 64, qf, 0.0).astype(BF16)
    bot = jnp.where(lane >= 64, qf, 0.0).astype(BF16)
    return jnp.concatenate([top, bot], axis=0)


def _meta_bias(is_real):
    j = lax.broadcasted_iota(I32, (META_KEYS, 2 * TQ), 0)
    t = lax.broadcasted_iota(I32, (META_KEYS, 2 * TQ), 1) % TQ
    real = is_real.astype(I32)
    lim = jnp.minimum(real * N_META + (1 - real) * (t + 1), N_META)
    return jnp.where(j < lim, BIG, NEG)


def _bcast_halves(row, n_sub):
    return jnp.concatenate([jnp.broadcast_to(row[:, :TQ], (n_sub, TQ)),
                            jnp.broadcast_to(row[:, TQ:], (n_sub, TQ))], axis=0)


SCORE_LOOKAHEAD = 3
MAX_LAG = 64.0


def _softmax_pv(segments, v_tiles, base):
    issued, out = {}, []
    for h in range(SCORE_LOOKAHEAD):
        issued[h] = [fn() for fn in segments[h]]
    for h in range(4):
        if h + SCORE_LOOKAHEAD < 4:
            issued[h + SCORE_LOOKAHEAD] = [fn() for fn in segments[h + SCORE_LOOKAHEAD]]
        sts = issued.pop(h)
        m_loc = jnp.max(sts[0], axis=0, keepdims=True)
        for st in sts[1:]:
            m_loc = jnp.maximum(m_loc, jnp.max(st, axis=0, keepdims=True))
        m_use = base(h, m_loc)
        l_new, pv = None, None
        for st, vt in zip(sts, v_tiles[h]):
            pc = jnp.exp2(st - m_use)
            s = jnp.sum(pc, axis=0, keepdims=True)
            d = jnp.dot(vt(), pc.astype(BF16), preferred_element_type=F32)
            l_new = s if l_new is None else l_new + s
            pv = d if pv is None else pv + d
        out.append((m_loc, l_new, pv))
    return out


def _attend(segments, v_tiles, m_ref, l_ref, acc_ref, fold, first):
    fold_pv, fold_row = fold
    if first:
        for h, (m_loc, l_new, pv) in enumerate(_softmax_pv(segments, v_tiles, lambda h, m_loc: m_loc)):
            m_ref[h] = m_loc
            l_ref[h] = l_new
            acc_ref[h] = fold_pv(pv)
        return

    def commit(parts, rebased):
        for h, (m_loc, l_new, pv) in enumerate(parts):
            m_new = jnp.maximum(m_ref[h], m_loc)
            alpha = jnp.exp2(m_ref[h] - m_new)
            if rebased:
                l_ref[h] = alpha * l_ref[h] + l_new
                acc_ref[h] = acc_ref[h] * fold_row(alpha) + fold_pv(pv)
            else:
                l_ref[h] = alpha * (l_ref[h] + l_new)
                acc_ref[h] = (acc_ref[h] + fold_pv(pv)) * fold_row(alpha)
            m_ref[h] = m_new

    parts = _softmax_pv(segments, v_tiles, lambda h, m_loc: m_ref[h])
    rise = parts[0][0] - m_ref[0]
    for h in range(1, 4):
        rise = jnp.maximum(rise, parts[h][0] - m_ref[h])
    safe = jnp.max(rise) <= MAX_LAG

    @pl.when(safe)
    def _():
        commit(parts, rebased=False)

    @pl.when(jnp.logical_not(safe))
    def _():
        commit(_softmax_pv(segments, v_tiles, lambda h, m_loc: jnp.maximum(m_ref[h], m_loc)), rebased=True)


GROUP_KEYS = 256
GROUPS_PER_CHUNK = TK // GROUP_KEYS


def _bit_transpose32(words):
    a = list(words)
    j, mask = 16, 0x0000FFFF
    while j:
        k = 0
        while k < 32:
            t = (a[k] ^ lax.shift_right_logical(a[k + j], jnp.int32(j))) & mask
            a[k] = a[k] ^ t
            a[k + j] = a[k + j] ^ lax.shift_left(t, jnp.int32(j))
            k = (k + j + 1) & ~j
        j >>= 1
        mask ^= mask << j
    return a


def _dsa_kernel(qa_ref, qi_ref, wit_ref, ki_ref, ka_ref, vat_ref, km_ref, vmt_ref, o_ref,
                qall_ref, qstk_ref, key_ref, plane_ref, bias_ref, tri_ref, sel_ref, m_ref, l_ref, acc_ref,
                *, seq, top_k):
    g = pl.program_id(0)

    @pl.when(g == 0)
    def _():
        plane_ref[...] = jnp.zeros(plane_ref.shape, I32)
        r = lax.broadcasted_iota(I32, (TK, TK), 0)
        c_ = lax.broadcasted_iota(I32, (TK, TK), 1)
        tri_ref[...] = jnp.where(c_ <= r, 1.0, 0.0).astype(BF16)

    is_real, i, n_ch = _step_geometry(g, seq)
    q_pos = i * TQ + lax.broadcasted_iota(I32, (TK, TQ), 1)
    k_off = lax.broadcasted_iota(I32, (TK, TQ), 0)

    qi = qi_ref[...]
    for h in range(IDX_HEADS):
        qall_ref[h * TQ:(h + 1) * TQ, :] = qi[:, h * IDX_DIM:(h + 1) * IDX_DIM]
    for p in range(4):
        qstk_ref[p] = _stack_pair(qa_ref[:, p * 128:(p + 1) * 128])

    wt = wit_ref[...]

    def score_body(c, carry):
        kc = ki_ref[pl.ds(pl.multiple_of(c * TK, TK), TK), :]
        logits = lax.dot_general(kc, qall_ref[...], _NT, preferred_element_type=F32)
        sc = jnp.zeros((TK, TQ), F32)
        for h in range(IDX_HEADS):
            sc = sc + wt[h:h + 1, :] * jnp.maximum(logits[:, h * TQ:(h + 1) * TQ], 0.0)
        sc = jnp.where(sc == 0.0, 0.0, sc)
        bits = pltpu.bitcast(sc, I32)
        key = bits ^ (lax.shift_right_arithmetic(bits, 31) & 0x7FFFFFFF)
        key = jnp.where(c * TK + k_off <= q_pos, key, INT_MIN)
        key_ref[c] = key
        u = key ^ INT_MIN
        for gi in range(GROUPS_PER_CHUNK):
            base = gi * GROUP_KEYS
            planes = _bit_transpose32([u[base + w * 8:base + (w + 1) * 8, :] for w in range(32)])
            for b in range(32):
                plane_ref[c * GROUPS_PER_CHUNK + gi, b] = planes[b]
        return carry

    lax.fori_loop(0, n_ch, score_body, 0)

    n_grp = n_ch * GROUPS_PER_CHUNK

    def radix_select(n_groups):
        alive0 = tuple(jnp.full((8, TQ), -1, I32) * (gq < n_grp).astype(I32) for gq in range(n_groups))

        def bit_body(b, carry):
            alive, k_rem, thr_u = carry
            ones = [a & plane_ref[gq, b] for gq, a in enumerate(alive)]
            cnt8 = lax.population_count(ones[0])
            for o in ones[1:]:
                cnt8 = cnt8 + lax.population_count(o)
            cnt = jnp.sum(cnt8, axis=0, keepdims=True)
            take = cnt >= k_rem
            alive = tuple(jnp.where(take, o, a ^ o) for a, o in zip(alive, ones))
            k_rem = jnp.where(take, k_rem, k_rem - cnt)
            thr_u = thr_u | jnp.where(take, lax.shift_left(jnp.int32(1), 31 - b), 0)
            return alive, k_rem, thr_u

        alive, k_rem, thr_u = lax.fori_loop(
            0, 32, bit_body, (alive0, jnp.full((1, TQ), top_k, I32), jnp.zeros((1, TQ), I32)))
        eq8 = lax.population_count(alive[0])
        for a in alive[1:]:
            eq8 = eq8 + lax.population_count(a)
        sel_ref[0:1, :] = thr_u
        sel_ref[1:2, :] = k_rem
        sel_ref[2:3, :] = jnp.sum(eq8, axis=0, keepdims=True)

    all_groups = plane_ref.shape[0]

    @pl.when(n_grp <= all_groups // 2)
    def _():
        radix_select(all_groups // 2)

    @pl.when(n_grp > all_groups // 2)
    def _():
        radix_select(all_groups)

    thr_u, k_rem, cnt_eq = sel_ref[0:1, :], sel_ref[1:2, :], sel_ref[2:3, :]
    thr = thr_u ^ INT_MIN
    excess = jnp.where(thr_u != 0, cnt_eq - k_rem, 0)
    has_excess = jnp.max(excess) > 0

    @pl.when(jnp.logical_not(has_excess))
    def _():
        thr_eff = jnp.maximum(thr, INT_MIN + 1)

        def cap_body(c, carry):
            bias_ref[c] = jnp.where(key_ref[c] >= thr_eff, BIG, NEG)
            return carry

        lax.fori_loop(0, n_ch, cap_body, 0)

    @pl.when(has_excess)
    def _():
        need = k_rem.astype(F32)

        def cap_body(c, run):
            k = key_ref[c]
            eq = k == thr
            pre = jnp.dot(tri_ref[...], jnp.where(eq, 1.0, 0.0).astype(BF16), preferred_element_type=F32)
            rank = run + pre
            b = jnp.where(k > thr, BIG, jnp.where(eq, jnp.where(rank <= need, BIG, NEG), NEG))
            bias_ref[c] = jnp.where(k == INT_MIN, NEG, b)
            return run + pre[TK - 1:TK, :]

        lax.fori_loop(0, n_ch, cap_body, jnp.zeros((1, TQ), F32))

    fold = (lambda ot: jnp.concatenate([ot[:64, :TQ], ot[64:, TQ:]], axis=0), lambda row: _bcast_halves(row, 64))
    pairs = [slice(p * 128, (p + 1) * 128) for p in range(4)]

    def score(p, k_ref, idx, cap):
        return lambda: jnp.minimum(lax.dot_general(k_ref[idx], qstk_ref[p], _NT, preferred_element_type=F32), cap)

    def tile(ref, idx):
        return lambda: ref[idx]

    mcap = _meta_bias(is_real)
    _attend([[score(p, km_ref, (slice(None), ps), mcap)] for p, ps in enumerate(pairs)],
            [[tile(vmt_ref, (0, ps, slice(0, META_KEYS)))] for ps in pairs],
            m_ref, l_ref, acc_ref, fold, first=True)

    def update(chunks):
        rows = [pl.ds(pl.multiple_of(c * TK, TK), TK) for c in chunks]
        caps = [jnp.concatenate([bias_ref[c]] * 2, axis=1) for c in chunks]
        _attend([[score(p, ka_ref, (row, ps), cap2) for row, cap2 in zip(rows, caps)] for p, ps in enumerate(pairs)],
                [[tile(vat_ref, (c, ps, slice(None))) for c in chunks] for ps in pairs],
                m_ref, l_ref, acc_ref, fold, first=False)

    def pair_body(j, carry):
        update([2 * j, 2 * j + 1])
        return carry

    lax.fori_loop(0, n_ch // 2, pair_body, 0)

    @pl.when(n_ch % 2 == 1)
    def _():
        update([n_ch - 1])

    for p, ps in enumerate(pairs):
        o_ref[:, ps] = (acc_ref[p] * _bcast_halves(1.0 / l_ref[p], 64)).T.astype(BF16)


def _dsa(proj, ki, wit, vat, batch, seq):
    n = proj.shape[0]
    nq_seq = seq // TQ
    n_real_q = batch * nq_seq
    nch = seq // TK
    top_k = min(TOPK_MAX, seq // 4)
    bidx = lambda g: jnp.minimum(g // nq_seq, batch - 1)
    meta_k = batch * seq // META_KEYS
    meta_tile = batch * nch
    return pl.pallas_call(
        functools.partial(_dsa_kernel, seq=seq, top_k=top_k),
        grid=(n // TQ,),
        in_specs=[
            pl.BlockSpec((TQ, HEAD_W), lambda g: (g, 0)),
            pl.BlockSpec((TQ, HEAD_W), lambda g: (g, 2)),
            pl.BlockSpec((16, TQ), lambda g: (0, g)),
            pl.BlockSpec((seq, IDX_DIM), lambda g: (bidx(g), 0)),
            pl.BlockSpec((seq, HEAD_W), lambda g: (bidx(g), 1)),
            pl.BlockSpec((nch, HEAD_W, TK), lambda g: (bidx(g), 0, 0)),
            pl.BlockSpec((META_KEYS, HEAD_W), lambda g: (meta_k, 1)),
            pl.BlockSpec((1, HEAD_W, TK), lambda g: (meta_tile, 0, 0)),
        ],
        out_specs=pl.BlockSpec((TQ, HEAD_W), lambda g: (g, 0)),
        out_shape=jax.ShapeDtypeStruct((n, HEAD_W), BF16),
        scratch_shapes=[
            pltpu.VMEM((IDX_HEADS * TQ, IDX_DIM), BF16),
            pltpu.VMEM((4, 2 * TQ, 128), BF16),
            pltpu.VMEM((nch, TK, TQ), I32),
            pltpu.VMEM((nch * GROUPS_PER_CHUNK, 32, 8, TQ), I32),
            pltpu.VMEM((nch, TK, TQ), F32),
            pltpu.VMEM((TK, TK), BF16),
            pltpu.VMEM((8, TQ), I32),
            pltpu.VMEM((4, 1, 2 * TQ), F32),
            pltpu.VMEM((4, 1, 2 * TQ), F32),
            pltpu.VMEM((4, 128, TQ), F32),
        ],
        compiler_params=_params(),
        name="dsa",
    )(proj, proj, wit, ki, proj, vat, proj, vat)


def _diff_kernel(lam_ref, g_ref, qb_ref, kb_ref, vbt_ref, km_ref, vmt_ref, o_ref,
                 qstk_ref, m_ref, l_ref, acc_ref, *, seq, lam_init):
    g = pl.program_id(0)
    is_real, i, n_ch = _step_geometry(g, seq)
    lp = lam_ref[...]
    lam = (jnp.exp(jnp.sum(lp[0:1] * lp[1:2], axis=1, keepdims=True))
           - jnp.exp(jnp.sum(lp[2:3] * lp[3:4], axis=1, keepdims=True)) + lam_init)
    heads = [slice(h * 128, (h + 1) * 128) for h in range(4)]
    for h, hs in enumerate(heads):
        qstk_ref[h] = _stack_pair(qb_ref[:, hs])

    fold = (lambda ot: ot, lambda row: row)

    def score(h, k_ref, idx, cap=None):
        def fn():
            st = lax.dot_general(k_ref[idx], qstk_ref[h], _NT, preferred_element_type=F32)
            return st if cap is None else jnp.minimum(st, cap)
        return fn

    def tile(ref, idx):
        return lambda: ref[idx]

    n_full = n_ch - is_real.astype(I32)
    mcap = _meta_bias(is_real)
    _attend([[score(h, km_ref, (slice(None), hs), mcap)] for h, hs in enumerate(heads)],
            [[tile(vmt_ref, (0, hs, slice(0, META_KEYS)))] for hs in heads],
            m_ref, l_ref, acc_ref, fold, first=True)

    def update(chunks, caps):
        rows = [pl.ds(pl.multiple_of(c * TK, TK), TK) for c in chunks]
        _attend([[score(h, kb_ref, (row, hs), cap) for row, cap in zip(rows, caps)] for h, hs in enumerate(heads)],
                [[tile(vbt_ref, (c, hs, slice(None))) for c in chunks] for hs in heads],
                m_ref, l_ref, acc_ref, fold, first=False)

    def pair_body(j, carry):
        update([2 * j, 2 * j + 1], [None, None])
        return carry

    lax.fori_loop(0, n_full // 2, pair_body, 0)

    q_pos = i * TQ + lax.broadcasted_iota(I32, (TK, 2 * TQ), 1) % TQ
    k_pos = n_full * TK + lax.broadcasted_iota(I32, (TK, 2 * TQ), 0)
    cap = jnp.where(jnp.where(k_pos <= q_pos, is_real.astype(I32), 0) > 0, BIG, NEG)

    @pl.when(n_full % 2 == 1)
    def _():
        update([n_full - 1, n_full], [None, cap])

    @pl.when(n_full % 2 == 0)
    def _():
        update([n_full], [cap])

    for h, hs in enumerate(heads):
        on = acc_ref[h] * (1.0 / l_ref[h])
        o = on[:, :TQ] - lam * on[:, TQ:]
        y = o * lax.rsqrt(jnp.mean(o * o, axis=0, keepdims=True) + EPS) * g_ref[...] * (1.0 - lam_init)
        o_ref[:, hs] = y.T.astype(BF16)


def _diff(proj, vbt, lam_params, subln_g, batch, seq, lam_init):
    n = proj.shape[0]
    nq_seq = seq // TQ
    nch = seq // TK
    bidx = lambda g: jnp.minimum(g // nq_seq, batch - 1)
    meta_k = batch * seq // META_KEYS
    meta_tile = batch * nch
    return pl.pallas_call(
        functools.partial(_diff_kernel, seq=seq, lam_init=lam_init),
        grid=(n // TQ,),
        in_specs=[
            _resident((4, 64), lambda g: (0, 0)),
            _resident((128, 1), lambda g: (0, 0)),
            pl.BlockSpec((TQ, HEAD_W), lambda g: (g, 3)),
            pl.BlockSpec((seq, HEAD_W), lambda g: (bidx(g), 4)),
            pl.BlockSpec((nch, HEAD_W, TK), lambda g: (bidx(g), 0, 0)),
            pl.BlockSpec((META_KEYS, HEAD_W), lambda g: (meta_k, 4)),
            pl.BlockSpec((1, HEAD_W, TK), lambda g: (meta_tile, 0, 0)),
        ],
        out_specs=pl.BlockSpec((TQ, HEAD_W), lambda g: (g, 0)),
        out_shape=jax.ShapeDtypeStruct((n, HEAD_W), BF16),
        scratch_shapes=[
            pltpu.VMEM((4, 2 * TQ, 128), BF16),
            pltpu.VMEM((4, 1, 2 * TQ), F32),
            pltpu.VMEM((4, 1, 2 * TQ), F32),
            pltpu.VMEM((4, 128, 2 * TQ), F32),
        ],
        compiler_params=_params(),
        name="diff_attn",
    )(lam_params, subln_g.reshape(128, 1), proj, proj, vbt, proj, vbt)


def _rglru_kernel(x_ref, g_ref, win_ref, cw_ref, cb_ref, gw_ref, gb_ref, lam_ref, wout_ref, o_ref,
                  xbuf, a_s, u_s, h_s, p_s, hstate, mstate, mhist, *, tiles_per_seq):
    g = pl.program_id(0)
    x = x_ref[...]
    xn = _rms(x, g_ref[...]).astype(BF16)
    yx = jnp.dot(xn, win_ref[...], preferred_element_type=F32)
    yb = yx[:, :D]
    y_br = 0.5 * yb * (1.0 + jnp.tanh(math.sqrt(2.0 / math.pi) * (yb + 0.044715 * (yb * yb * yb))))

    @pl.when(g == 0)
    def _():
        xbuf[0:8, :] = jnp.zeros((8, D), F32)
        hstate[...] = jnp.zeros(hstate.shape, F32)

    @pl.when(jnp.logical_and(g >= 1, (g - 1) % tiles_per_seq == 0))
    def _():
        xbuf[0:8, :] = mhist[...]
        hstate[...] = mstate[...]

    xbuf[8:, :] = yx[:, D:]
    xc = cb_ref[...] + cw_ref[0:1, :] * xbuf[5:5 + TM, :]
    for j in range(1, CONV_W):
        xc = xc + cw_ref[j:j + 1, :] * xbuf[5 + j:5 + j + TM, :]

    gx, ga = [], []
    for nb in range(LRU_BLOCKS):
        xb = xc[:, nb * LRU_BLOCK_W:(nb + 1) * LRU_BLOCK_W].astype(BF16)
        gx.append(jnp.dot(xb, gw_ref[0, nb], preferred_element_type=F32))
        ga.append(jnp.dot(xb, gw_ref[1, nb], preferred_element_type=F32))
    gate_x = _sigmoid(jnp.concatenate(gx, axis=1) + gb_ref[0:1, :])
    gate_a = _sigmoid(jnp.concatenate(ga, axis=1) + gb_ref[1:2, :])
    lam = lam_ref[...]
    log_sig = jnp.minimum(lam, 0.0) - jnp.log(1.0 + jnp.exp(-jnp.abs(lam)))
    log_a = RG_C * gate_a * log_sig
    a = jnp.exp(log_a)
    a_s[...] = a
    u_s[...] = jnp.sqrt(-jnp.tanh(log_a) * (a * a + 1.0)) * (gate_x * xc)

    seg_rows = TM // SCAN_SEGS

    def step(r, carry):
        hs, ps = carry
        new_h, new_p = [], []
        for s in range(SCAN_SEGS):
            row = pl.ds(s * seg_rows + r, 1)
            a_r = a_s[row, :]
            h_r = a_r * hs[s] + u_s[row, :]
            h_s[row, :] = h_r
            new_h.append(h_r)
            if s > 0:
                p_r = a_r * ps[s - 1]
                p_s[row, :] = p_r
                new_p.append(p_r)
        return tuple(new_h), tuple(new_p)

    zero = jnp.zeros((1, D), F32)
    hs, _ = lax.fori_loop(0, seg_rows, step,
                          ((hstate[...],) + (zero,) * (SCAN_SEGS - 1), (zero + 1.0,) * (SCAN_SEGS - 1)), unroll=4)
    h_in = hs[0]
    for s in range(1, SCAN_SEGS):
        seg = slice(s * seg_rows, (s + 1) * seg_rows)
        h_seg = h_s[seg, :] + p_s[seg, :] * h_in
        h_s[seg, :] = h_seg
        h_in = h_seg[seg_rows - 1:seg_rows, :]
    hstate[...] = h_in

    @pl.when(g == 0)
    def _():
        mstate[...] = h_s[N_META - 1:N_META, :]
        mhist[...] = xbuf[N_META:N_META + 8, :]

    xbuf[0:8, :] = xbuf[TM:TM + 8, :]
    o_ref[...] = x + jnp.dot((h_s[...] * y_br).astype(BF16), wout_ref[...], preferred_element_type=F32)


def _rglru(h, g, w_in, conv_w, conv_b, gate_w, gate_b, lru_lambda, w_out, tiles_per_seq):
    n = h.shape[0]
    nt = n // TM
    tile = lambda s: (jnp.where(s == 0, nt - 1, s - 1), 0)
    const2 = lambda s: (0, 0)
    return pl.pallas_call(
        functools.partial(_rglru_kernel, tiles_per_seq=tiles_per_seq),
        grid=(nt,),
        in_specs=[
            pl.BlockSpec((TM, D), tile),
            _resident((1, D), const2),
            _resident((D, 2 * D), const2),
            _resident((CONV_W, D), const2),
            _resident((1, D), const2),
            _resident((2, LRU_BLOCKS, LRU_BLOCK_W, LRU_BLOCK_W), lambda s: (0, 0, 0, 0)),
            _resident((2, D), const2),
            _resident((1, D), const2),
            _resident((D, D), const2),
        ],
        out_specs=pl.BlockSpec((TM, D), tile),
        out_shape=jax.ShapeDtypeStruct((n, D), F32),
        scratch_shapes=[
            pltpu.VMEM((TM + 8, D), F32),
            pltpu.VMEM((TM, D), F32),
            pltpu.VMEM((TM, D), F32),
            pltpu.VMEM((TM, D), F32),
            pltpu.VMEM((TM, D), F32),
            pltpu.VMEM((1, D), F32),
            pltpu.VMEM((1, D), F32),
            pltpu.VMEM((8, D), F32),
        ],
        compiler_params=_params(),
        name="rglru",
    )(h, g.reshape(1, D), w_in, conv_w, conv_b.reshape(1, D), gate_w, gate_b, lru_lambda.reshape(1, D), w_out)


def _split_attn_w_in(w):
    a = HEAD_W
    o_ki = 4 * a
    o_wi = o_ki + IDX_DIM
    o_qb = o_wi + IDX_HEADS
    qa, ka, va, qi = w[:, 0:a], w[:, a:2 * a], w[:, 2 * a:3 * a], w[:, 3 * a:4 * a]
    ki, wi = w[:, o_ki:o_wi], w[:, o_wi:o_qb]
    qb, kb, vb = w[:, o_qb:o_qb + a], w[:, o_qb + a:o_qb + 2 * a], w[:, o_qb + 2 * a:o_qb + 3 * a]
    w_main = jnp.concatenate([qa, ka, qi, qb, kb, va, vb], axis=1).astype(BF16)
    w_wt = jnp.concatenate([wi.T, jnp.zeros((16 - IDX_HEADS, D), w.dtype)], axis=0).astype(BF16)
    return w_main, ki.astype(BF16), w_wt


def kernel(x, meta_tokens, norm_g, ffn_w_gu, ffn_w_down, attn_w_in, idx_k_ln_g, idx_k_ln_b, diff_lambda, diff_subln_g, attn_w_out, rec_w_in, rec_conv_w, rec_conv_b, rec_gate_w, rec_gate_b, rec_lambda, rec_w_out, final_norm_g):
    batch, seq, _ = x.shape
    depth = norm_g.shape[0]
    n_real = batch * seq
    tail = jnp.concatenate([meta_tokens.astype(x.dtype), jnp.zeros((TM - N_META, D), x.dtype)], axis=0)
    w_gu = ffn_w_gu.astype(BF16)
    w_down = ffn_w_down.astype(BF16)
    h = x.reshape(n_real, D)
    for i in range(depth):
        j = i // 2
        h = _ffn(h, norm_g[i, 0], w_gu, w_down, i, 0, tail=tail if i == 0 else None)
        if i % 2 == 0:
            lam_init = 0.8 - 0.6 * math.exp(-0.3 * i)
            w_main, w_ki, w_wt = _split_attn_w_in(attn_w_in[j])
            proj, ki, wit, vat, vbt = _attn_proj(h, norm_g[i, 1], w_main, w_ki, w_wt, idx_k_ln_g[j], idx_k_ln_b[j])
            oa = _dsa(proj, ki, wit, vat, batch, seq)
            ob = _diff(proj, vbt, diff_lambda[j], diff_subln_g[j], batch, seq, lam_init)
            mixer = (oa, ob, attn_w_out[j].astype(BF16))
        else:
            mixer = None
            h = _rglru(h, norm_g[i, 1], rec_w_in[j].astype(BF16), rec_conv_w[j], rec_conv_b[j],
                       rec_gate_w[j].astype(BF16), rec_gate_b[j], rec_lambda[j], rec_w_out[j].astype(BF16),
                       seq // TM)
        if i == depth - 1:
            h = _ffn(h, norm_g[i, 2], w_gu, w_down, i, 1, mixer=mixer, final_g=final_norm_g, n_rows=n_real)
        else:
            h = _ffn(h, norm_g[i, 2], w_gu, w_down, i, 1, mixer=mixer)
    return h.reshape(batch, seq, D)
```

```python
import functools
import math

import jax
import jax.numpy as jnp
from jax import lax
from jax.experimental import pallas as pl
from jax.experimental.pallas import tpu as pltpu

F32 = jnp.float32
BF16 = jnp.bfloat16
I32 = jnp.int32

D = 1024
N_META = 16
D_FF = 2816
EPS = 1e-6
HEAD_W = 512
IDX_DIM = 64
IDX_HEADS = 8
TOPK_MAX = 256
LRU_BLOCKS = 4
LRU_BLOCK_W = D // LRU_BLOCKS
CONV_W = 4
RG_C = 8.0

TM = 512
TQ = 256
META_KEYS = 16
TK = 512
SCAN_SEGS = 4
MXU_TILE = 256
FF_SPLIT = (D_FF // MXU_TILE + 1) // 2 * MXU_TILE
FF_CHUNKS = ((0, FF_SPLIT), (FF_SPLIT, D_FF))
NEG = -1e30
BIG = 1e30
QK_SCALE = IDX_DIM ** -0.5 * math.log2(math.e)
INT_MIN = -2 ** 31
VMEM_LIMIT = 56 * 1024 * 1024

_NT = (((1,), (1,)), ((), ()))


def _params():
    return pltpu.CompilerParams(dimension_semantics=("arbitrary",), vmem_limit_bytes=VMEM_LIMIT)


def _resident(shape, index_map):
    return pl.BlockSpec(shape, index_map, pipeline_mode=pl.Buffered(1))


def _rms(x, g):
    return x * lax.rsqrt(jnp.mean(x * x, axis=-1, keepdims=True) + EPS) * g


def _sigmoid(x):
    return 0.5 * jnp.tanh(0.5 * x) + 0.5


def _ffn_kernel(*refs, final, n_real_tiles, mixer):
    refs = list(refs)
    o_ref = refs.pop()
    x = refs.pop(0)[...]
    if n_real_tiles is not None:
        x = jnp.where(pl.program_id(0) < n_real_tiles, x, refs.pop(0)[...])
    if mixer:
        oa_ref, ob_ref, wa_ref, wb_ref = refs[:4]
        del refs[:4]
        x = (x + jnp.dot(oa_ref[...], wa_ref[...], preferred_element_type=F32)
             + jnp.dot(ob_ref[...], wb_ref[...], preferred_element_type=F32))
    g_ref, wg_ref, wu_ref, wd_ref = refs[:4]
    xn = _rms(x, g_ref[...]).astype(BF16)
    acc = None
    for lo, hi in FF_CHUNKS:
        sl = slice(lo, hi)
        g = jnp.dot(xn, wg_ref[:, sl], preferred_element_type=F32)
        u = jnp.dot(xn, wu_ref[:, sl], preferred_element_type=F32)
        a = (g * _sigmoid(g) * u).astype(BF16)
        part = jnp.dot(a, wd_ref[sl, :], preferred_element_type=F32)
        acc = part if acc is None else acc + part
    y = x + 0.5 * acc
    if final:
        y = _rms(y, refs[4][...])
    o_ref[...] = y


def _ffn(h, g, w_gu, w_down, layer, half, tail=None, mixer=None, final_g=None, n_rows=None):
    n_real_tiles = None if tail is None else h.shape[0] // TM
    if n_rows is None:
        n_rows = h.shape[0] + (0 if tail is None else TM)
    row_tile = lambda i: (i, 0)
    const = lambda i: (0, 0)
    if tail is None:
        in_specs, args = [pl.BlockSpec((TM, D), row_tile)], [h]
    else:
        in_specs = [pl.BlockSpec((TM, D), lambda i: (jnp.minimum(i, n_real_tiles - 1), 0)), _resident((TM, D), const)]
        args = [h, tail]
    if mixer is not None:
        oa, ob, w_out = mixer
        in_specs += [pl.BlockSpec((TM, HEAD_W), row_tile), pl.BlockSpec((TM, HEAD_W), row_tile),
                     _resident((HEAD_W, D), const), _resident((HEAD_W, D), lambda i: (1, 0))]
        args += [oa, ob, w_out, w_out]
    in_specs += [
        _resident((1, D), const),
        _resident((None, None, D, D_FF), lambda i: (layer, half, 0, 0)),
        _resident((None, None, D, D_FF), lambda i: (layer, half, 0, 1)),
        _resident((None, None, D_FF, D), lambda i: (layer, half, 0, 0)),
    ]
    args += [g.reshape(1, D), w_gu, w_gu, w_down]
    if final_g is not None:
        in_specs.append(_resident((1, D), const))
        args.append(final_g.reshape(1, D))
    return pl.pallas_call(
        functools.partial(_ffn_kernel, final=final_g is not None, n_real_tiles=n_real_tiles,
                          mixer=mixer is not None),
        grid=(n_rows // TM,),
        in_specs=in_specs,
        out_specs=pl.BlockSpec((TM, D), row_tile),
        out_shape=jax.ShapeDtypeStruct((n_rows, D), F32),
        compiler_params=_params(),
        name="ffn",
    )(*args)


N_PROJ = 5 * HEAD_W


def _attn_proj_kernel(x_ref, g_ref, w_ref, wki_ref, wwt_ref, lng_ref, lnb_ref,
                      proj_ref, ki_ref, wit_ref, vat_ref, vbt_ref):
    xn = _rms(x_ref[...], g_ref[...]).astype(BF16)
    y = jnp.dot(xn, w_ref[...], preferred_element_type=F32)
    scale = QK_SCALE
    proj_ref[:, 0:HEAD_W] = (y[:, 0:HEAD_W] * scale).astype(BF16)
    proj_ref[:, HEAD_W:3 * HEAD_W] = y[:, HEAD_W:3 * HEAD_W].astype(BF16)
    proj_ref[:, 3 * HEAD_W:4 * HEAD_W] = (y[:, 3 * HEAD_W:4 * HEAD_W] * scale).astype(BF16)
    proj_ref[:, 4 * HEAD_W:5 * HEAD_W] = y[:, 4 * HEAD_W:5 * HEAD_W].astype(BF16)
    vat_ref[0] = y[:, 5 * HEAD_W:6 * HEAD_W].T.astype(BF16)
    vbt_ref[0] = y[:, 6 * HEAD_W:7 * HEAD_W].T.astype(BF16)
    yk = jnp.dot(xn, wki_ref[...], preferred_element_type=F32)
    mu = jnp.mean(yk, axis=-1, keepdims=True)
    yc = yk - mu
    ln = yc * lax.rsqrt(jnp.mean(yc * yc, axis=-1, keepdims=True) + EPS)
    ki_ref[...] = (ln * lng_ref[...] + lnb_ref[...]).astype(BF16)
    wit = lax.dot_general(wwt_ref[...], xn, _NT, preferred_element_type=F32)
    wit_ref[...] = wit * (IDX_HEADS ** -0.5 * IDX_DIM ** -0.5)


def _attn_proj(h, g, w_main, w_ki, w_wt, ln_g, ln_b):
    n = h.shape[0]
    nt = n // TM
    const = lambda i: (0, 0)
    return pl.pallas_call(
        _attn_proj_kernel,
        grid=(nt,),
        in_specs=[
            pl.BlockSpec((TM, D), lambda i: (i, 0)),
            _resident((1, D), const),
            _resident((D, 7 * HEAD_W), const),
            _resident((D, IDX_DIM), const),
            _resident((16, D), const),
            _resident((1, IDX_DIM), const),
            _resident((1, IDX_DIM), const),
        ],
        out_specs=[
            pl.BlockSpec((TM, N_PROJ), lambda i: (i, 0)),
            pl.BlockSpec((TM, IDX_DIM), lambda i: (i, 0)),
            pl.BlockSpec((16, TM), lambda i: (0, i)),
            pl.BlockSpec((1, HEAD_W, TM), lambda i: (i, 0, 0)),
            pl.BlockSpec((1, HEAD_W, TM), lambda i: (i, 0, 0)),
        ],
        out_shape=[
            jax.ShapeDtypeStruct((n, N_PROJ), BF16),
            jax.ShapeDtypeStruct((n, IDX_DIM), BF16),
            jax.ShapeDtypeStruct((16, n), F32),
            jax.ShapeDtypeStruct((nt, HEAD_W, TM), BF16),
            jax.ShapeDtypeStruct((nt, HEAD_W, TM), BF16),
        ],
        compiler_params=_params(),
        name="attn_proj",
    )(h, g.reshape(1, D), w_main, w_ki, w_wt, ln_g.reshape(1, IDX_DIM), ln_b.reshape(1, IDX_DIM))


def _step_geometry(g, seq):
    nq_seq = seq // TQ
    n_real = pl.num_programs(0) - TM // TQ
    is_real = g < n_real
    i = g % nq_seq
    n_ch = jnp.where(is_real, (i * TQ + TQ + TK - 1) // TK, 0)
    return is_real, i, n_ch


def _stack_pair(q_pair):
    lane = lax.broadcasted_iota(I32, (TQ, 128), 1)
    qf = q_pair.astype(F32)
    top = jnp.where(lane < 64, qf, 0.0).astype(BF16)
    bot = jnp.where(lane >= 64, qf, 0.0).astype(BF16)
    return jnp.concatenate([top, bot], axis=0)


def _meta_bias(is_real):
    j = lax.broadcasted_iota(I32, (META_KEYS, 2 * TQ), 0)
    t = lax.broadcasted_iota(I32, (META_KEYS, 2 * TQ), 1) % TQ
    real = is_real.astype(I32)
    lim = jnp.minimum(real * N_META + (1 - real) * (t + 1), N_META)
    return jnp.where(j < lim, BIG, NEG)


def _bcast_halves(row, n_sub):
    return jnp.concatenate([jnp.broadcast_to(row[:, :TQ], (n_sub, TQ)),
                            jnp.broadcast_to(row[:, TQ:], (n_sub, TQ))], axis=0)


SCORE_LOOKAHEAD = 3
MAX_LAG = 64.0


def _softmax_pv(segments, v_tiles, base):
    issued, out = {}, []
    for h in range(SCORE_LOOKAHEAD):
        issued[h] = [fn() for fn in segments[h]]
    for h in range(4):
        if h + SCORE_LOOKAHEAD < 4:
            issued[h + SCORE_LOOKAHEAD] = [fn() for fn in segments[h + SCORE_LOOKAHEAD]]
        sts = issued.pop(h)
        m_loc = jnp.max(sts[0], axis=0, keepdims=True)
        for st in sts[1:]:
            m_loc = jnp.maximum(m_loc, jnp.max(st, axis=0, keepdims=True))
        m_use = base(h, m_loc)
        l_new, pv = None, None
        for st, vt in zip(sts, v_tiles[h]):
            pc = jnp.exp2(st - m_use)
            s = jnp.sum(pc, axis=0, keepdims=True)
            d = jnp.dot(vt(), pc.astype(BF16), preferred_element_type=F32)
            l_new = s if l_new is None else l_new + s
            pv = d if pv is None else pv + d
        out.append((m_loc, l_new, pv))
    return out


def _attend(segments, v_tiles, m_ref, l_ref, acc_ref, fold, first):
    fold_pv, fold_row = fold
    if first:
        for h, (m_loc, l_new, pv) in enumerate(_softmax_pv(segments, v_tiles, lambda h, m_loc: m_loc)):
            m_ref[h] = m_loc
            l_ref[h] = l_new
            acc_ref[h] = fold_pv(pv)
        return

    def commit(parts, rebased):
        for h, (m_loc, l_new, pv) in enumerate(parts):
            m_new = jnp.maximum(m_ref[h], m_loc)
            alpha = jnp.exp2(m_ref[h] - m_new)
            if rebased:
                l_ref[h] = alpha * l_ref[h] + l_new
                acc_ref[h] = acc_ref[h] * fold_row(alpha) + fold_pv(pv)
            else:
                l_ref[h] = alpha * (l_ref[h] + l_new)
                acc_ref[h] = (acc_ref[h] + fold_pv(pv)) * fold_row(alpha)
            m_ref[h] = m_new

    parts = _softmax_pv(segments, v_tiles, lambda h, m_loc: m_ref[h])
    rise = parts[0][0] - m_ref[0]
    for h in range(1, 4):
        rise = jnp.maximum(rise, parts[h][0] - m_ref[h])
    safe = jnp.max(rise) <= MAX_LAG

    @pl.when(safe)
    def _():
        commit(parts, rebased=False)

    @pl.when(jnp.logical_not(safe))
    def _():
        commit(_softmax_pv(segments, v_tiles, lambda h, m_loc: jnp.maximum(m_ref[h], m_loc)), rebased=True)


GROUP_KEYS = 256
GROUPS_PER_CHUNK = TK // GROUP_KEYS


def _bit_transpose32(words):
    a = list(words)
    j, mask = 16, 0x0000FFFF
    while j:
        k = 0
        while k < 32:
            t = (a[k] ^ lax.shift_right_logical(a[k + j], jnp.int32(j))) & mask
            a[k] = a[k] ^ t
            a[k + j] = a[k + j] ^ lax.shift_left(t, jnp.int32(j))
            k = (k + j + 1) & ~j
        j >>= 1
        mask ^= mask << j
    return a


def _dsa_kernel(qa_ref, qi_ref, wit_ref, ki_ref, ka_ref, vat_ref, km_ref, vmt_ref, o_ref,
                qall_ref, qstk_ref, key_ref, plane_ref, bias_ref, tri_ref, sel_ref, m_ref, l_ref, acc_ref,
                *, seq, top_k):
    g = pl.program_id(0)

    @pl.when(g == 0)
    def _():
        r = lax.broadcasted_iota(I32, (TK, TK), 0)
        c_ = lax.broadcasted_iota(I32, (TK, TK), 1)
        tri_ref[...] = jnp.where(c_ <= r, 1.0, 0.0).astype(BF16)

    is_real, i, n_ch = _step_geometry(g, seq)
    q_pos = i * TQ + lax.broadcasted_iota(I32, (TK, TQ), 1)
    k_off = lax.broadcasted_iota(I32, (TK, TQ), 0)

    qi = qi_ref[...]
    for h in range(IDX_HEADS):
        qall_ref[h * TQ:(h + 1) * TQ, :] = qi[:, h * IDX_DIM:(h + 1) * IDX_DIM]
    for p in range(4):
        qstk_ref[p] = _stack_pair(qa_ref[:, p * 128:(p + 1) * 128])

    wt = wit_ref[...]

    def score_body(c, carry):
        kc = ki_ref[pl.ds(pl.multiple_of(c * TK, TK), TK), :]
        logits = lax.dot_general(kc, qall_ref[...], _NT, preferred_element_type=F32)
        sc = jnp.zeros((TK, TQ), F32)
        for h in range(IDX_HEADS):
            sc = sc + wt[h:h + 1, :] * jnp.maximum(logits[:, h * TQ:(h + 1) * TQ], 0.0)
        sc = jnp.where(sc == 0.0, 0.0, sc)
        bits = pltpu.bitcast(sc, I32)
        key = bits ^ (lax.shift_right_arithmetic(bits, 31) & 0x7FFFFFFF)
        key = jnp.where(c * TK + k_off <= q_pos, key, INT_MIN)
        key_ref[c] = key
        u = key ^ INT_MIN
        for gi in range(GROUPS_PER_CHUNK):
            base = gi * GROUP_KEYS
            planes = _bit_transpose32([u[base + w * 8:base + (w + 1) * 8, :] for w in range(32)])
            for b in range(32):
                plane_ref[c * GROUPS_PER_CHUNK + gi, b] = planes[b]
        return carry

    lax.fori_loop(0, n_ch, score_body, 0)

    n_grp = n_ch * GROUPS_PER_CHUNK

    def radix_select(n_groups):
        alive0 = (jnp.full((8, TQ), -1, I32),) * n_groups

        def bit_body(b, carry):
            alive, k_rem, thr_u = carry
            ones = [a & plane_ref[gq, b] for gq, a in enumerate(alive)]
            cnt8 = lax.population_count(ones[0])
            for o in ones[1:]:
                cnt8 = cnt8 + lax.population_count(o)
            cnt = jnp.sum(cnt8, axis=0, keepdims=True)
            take = cnt >= k_rem
            alive = tuple(jnp.where(take, o, a ^ o) for a, o in zip(alive, ones))
            k_rem = jnp.where(take, k_rem, k_rem - cnt)
            thr_u = thr_u | jnp.where(take, lax.shift_left(jnp.int32(1), 31 - b), 0)
            return alive, k_rem, thr_u

        alive, k_rem, thr_u = lax.fori_loop(
            0, 32, bit_body, (alive0, jnp.full((1, TQ), top_k, I32), jnp.zeros((1, TQ), I32)))
        eq8 = lax.population_count(alive[0])
        for a in alive[1:]:
            eq8 = eq8 + lax.population_count(a)
        sel_ref[0:1, :] = thr_u
        sel_ref[1:2, :] = k_rem
        sel_ref[2:3, :] = jnp.sum(eq8, axis=0, keepdims=True)

    for n_variant in range(GROUPS_PER_CHUNK, plane_ref.shape[0] + 1, GROUPS_PER_CHUNK):
        pl.when(n_grp == n_variant)(functools.partial(radix_select, n_variant))

    thr_u, k_rem, cnt_eq = sel_ref[0:1, :], sel_ref[1:2, :], sel_ref[2:3, :]
    thr = thr_u ^ INT_MIN
    excess = jnp.where(thr_u != 0, cnt_eq - k_rem, 0)
    has_excess = jnp.max(excess) > 0

    @pl.when(jnp.logical_not(has_excess))
    def _():
        thr_eff = jnp.maximum(thr, INT_MIN + 1)

        def cap_body(c, carry):
            bias_ref[c] = jnp.where(key_ref[c] >= thr_eff, BIG, NEG)
            return carry

        lax.fori_loop(0, n_ch, cap_body, 0)

    @pl.when(has_excess)
    def _():
        need = k_rem.astype(F32)

        def cap_body(c, run):
            k = key_ref[c]
            eq = k == thr
            pre = jnp.dot(tri_ref[...], jnp.where(eq, 1.0, 0.0).astype(BF16), preferred_element_type=F32)
            rank = run + pre
            b = jnp.where(k > thr, BIG, jnp.where(eq, jnp.where(rank <= need, BIG, NEG), NEG))
            bias_ref[c] = jnp.where(k == INT_MIN, NEG, b)
            return run + pre[TK - 1:TK, :]

        lax.fori_loop(0, n_ch, cap_body, jnp.zeros((1, TQ), F32))

    fold = (lambda ot: jnp.concatenate([ot[:64, :TQ], ot[64:, TQ:]], axis=0), lambda row: _bcast_halves(row, 64))
    pairs = [slice(p * 128, (p + 1) * 128) for p in range(4)]

    def score(p, k_ref, idx, cap):
        return lambda: jnp.minimum(lax.dot_general(k_ref[idx], qstk_ref[p], _NT, preferred_element_type=F32), cap)

    def tile(ref, idx):
        return lambda: ref[idx]

    mcap = _meta_bias(is_real)
    _attend([[score(p, km_ref, (slice(None), ps), mcap)] for p, ps in enumerate(pairs)],
            [[tile(vmt_ref, (0, ps, slice(0, META_KEYS)))] for ps in pairs],
            m_ref, l_ref, acc_ref, fold, first=True)

    def update(chunks):
        rows = [pl.ds(pl.multiple_of(c * TK, TK), TK) for c in chunks]
        caps = [jnp.concatenate([bias_ref[c]] * 2, axis=1) for c in chunks]
        _attend([[score(p, ka_ref, (row, ps), cap2) for row, cap2 in zip(rows, caps)] for p, ps in enumerate(pairs)],
                [[tile(vat_ref, (c, ps, slice(None))) for c in chunks] for ps in pairs],
                m_ref, l_ref, acc_ref, fold, first=False)

    def pair_body(j, carry):
        update([2 * j, 2 * j + 1])
        return carry

    lax.fori_loop(0, n_ch // 2, pair_body, 0)

    @pl.when(n_ch % 2 == 1)
    def _():
        update([n_ch - 1])

    for p, ps in enumerate(pairs):
        o_ref[:, ps] = (acc_ref[p] * _bcast_halves(1.0 / l_ref[p], 64)).T.astype(BF16)


def _dsa(proj, ki, wit, vat, batch, seq):
    n = proj.shape[0]
    nq_seq = seq // TQ
    n_real_q = batch * nq_seq
    nch = seq // TK
    top_k = min(TOPK_MAX, seq // 4)
    bidx = lambda g: jnp.minimum(g // nq_seq, batch - 1)
    meta_k = batch * seq // META_KEYS
    meta_tile = batch * nch
    return pl.pallas_call(
        functools.partial(_dsa_kernel, seq=seq, top_k=top_k),
        grid=(n // TQ,),
        in_specs=[
            pl.BlockSpec((TQ, HEAD_W), lambda g: (g, 0)),
            pl.BlockSpec((TQ, HEAD_W), lambda g: (g, 2)),
            pl.BlockSpec((16, TQ), lambda g: (0, g)),
            pl.BlockSpec((seq, IDX_DIM), lambda g: (bidx(g), 0)),
            pl.BlockSpec((seq, HEAD_W), lambda g: (bidx(g), 1)),
            pl.BlockSpec((nch, HEAD_W, TK), lambda g: (bidx(g), 0, 0)),
            pl.BlockSpec((META_KEYS, HEAD_W), lambda g: (meta_k, 1)),
            pl.BlockSpec((1, HEAD_W, TK), lambda g: (meta_tile, 0, 0)),
        ],
        out_specs=pl.BlockSpec((TQ, HEAD_W), lambda g: (g, 0)),
        out_shape=jax.ShapeDtypeStruct((n, HEAD_W), BF16),
        scratch_shapes=[
            pltpu.VMEM((IDX_HEADS * TQ, IDX_DIM), BF16),
            pltpu.VMEM((4, 2 * TQ, 128), BF16),
            pltpu.VMEM((nch, TK, TQ), I32),
            pltpu.VMEM((nch * GROUPS_PER_CHUNK, 32, 8, TQ), I32),
            pltpu.VMEM((nch, TK, TQ), F32),
            pltpu.VMEM((TK, TK), BF16),
            pltpu.VMEM((8, TQ), I32),
            pltpu.VMEM((4, 1, 2 * TQ), F32),
            pltpu.VMEM((4, 1, 2 * TQ), F32),
            pltpu.VMEM((4, 128, TQ), F32),
        ],
        compiler_params=_params(),
        name="dsa",
    )(proj, proj, wit, ki, proj, vat, proj, vat)


def _diff_kernel(lam_ref, g_ref, qb_ref, kb_ref, vbt_ref, km_ref, vmt_ref, o_ref,
                 qstk_ref, m_ref, l_ref, acc_ref, *, seq, lam_init):
    g = pl.program_id(0)
    is_real, i, n_ch = _step_geometry(g, seq)
    lp = lam_ref[...]
    lam = (jnp.exp(jnp.sum(lp[0:1] * lp[1:2], axis=1, keepdims=True))
           - jnp.exp(jnp.sum(lp[2:3] * lp[3:4], axis=1, keepdims=True)) + lam_init)
    heads = [slice(h * 128, (h + 1) * 128) for h in range(4)]
    for h, hs in enumerate(heads):
        qstk_ref[h] = _stack_pair(qb_ref[:, hs])

    fold = (lambda ot: ot, lambda row: row)

    def score(h, k_ref, idx, cap=None):
        def fn():
            st = lax.dot_general(k_ref[idx], qstk_ref[h], _NT, preferred_element_type=F32)
            return st if cap is None else jnp.minimum(st, cap)
        return fn

    def tile(ref, idx):
        return lambda: ref[idx]

    n_full = n_ch - is_real.astype(I32)
    mcap = _meta_bias(is_real)
    _attend([[score(h, km_ref, (slice(None), hs), mcap)] for h, hs in enumerate(heads)],
            [[tile(vmt_ref, (0, hs, slice(0, META_KEYS)))] for hs in heads],
            m_ref, l_ref, acc_ref, fold, first=True)

    def update(chunks, caps):
        rows = [pl.ds(pl.multiple_of(c * TK, TK), TK) for c in chunks]
        _attend([[score(h, kb_ref, (row, hs), cap) for row, cap in zip(rows, caps)] for h, hs in enumerate(heads)],
                [[tile(vbt_ref, (c, hs, slice(None))) for c in chunks] for hs in heads],
                m_ref, l_ref, acc_ref, fold, first=False)

    def pair_body(j, carry):
        update([2 * j, 2 * j + 1], [None, None])
        return carry

    lax.fori_loop(0, n_full // 2, pair_body, 0)

    q_pos = i * TQ + lax.broadcasted_iota(I32, (TK, 2 * TQ), 1) % TQ
    k_pos = n_full * TK + lax.broadcasted_iota(I32, (TK, 2 * TQ), 0)
    cap = jnp.where(jnp.where(k_pos <= q_pos, is_real.astype(I32), 0) > 0, BIG, NEG)

    @pl.when(n_full % 2 == 1)
    def _():
        update([n_full - 1, n_full], [None, cap])

    @pl.when(n_full % 2 == 0)
    def _():
        update([n_full], [cap])

    for h, hs in enumerate(heads):
        on = acc_ref[h] * (1.0 / l_ref[h])
        o = on[:, :TQ] - lam * on[:, TQ:]
        y = o * lax.rsqrt(jnp.mean(o * o, axis=0, keepdims=True) + EPS) * g_ref[...] * (1.0 - lam_init)
        o_ref[:, hs] = y.T.astype(BF16)


def _diff(proj, vbt, lam_params, subln_g, batch, seq, lam_init):
    n = proj.shape[0]
    nq_seq = seq // TQ
    nch = seq // TK
    bidx = lambda g: jnp.minimum(g // nq_seq, batch - 1)
    meta_k = batch * seq // META_KEYS
    meta_tile = batch * nch
    return pl.pallas_call(
        functools.partial(_diff_kernel, seq=seq, lam_init=lam_init),
        grid=(n // TQ,),
        in_specs=[
            _resident((4, 64), lambda g: (0, 0)),
            _resident((128, 1), lambda g: (0, 0)),
            pl.BlockSpec((TQ, HEAD_W), lambda g: (g, 3)),
            pl.BlockSpec((seq, HEAD_W), lambda g: (bidx(g), 4)),
            pl.BlockSpec((nch, HEAD_W, TK), lambda g: (bidx(g), 0, 0)),
            pl.BlockSpec((META_KEYS, HEAD_W), lambda g: (meta_k, 4)),
            pl.BlockSpec((1, HEAD_W, TK), lambda g: (meta_tile, 0, 0)),
        ],
        out_specs=pl.BlockSpec((TQ, HEAD_W), lambda g: (g, 0)),
        out_shape=jax.ShapeDtypeStruct((n, HEAD_W), BF16),
        scratch_shapes=[
            pltpu.VMEM((4, 2 * TQ, 128), BF16),
            pltpu.VMEM((4, 1, 2 * TQ), F32),
            pltpu.VMEM((4, 1, 2 * TQ), F32),
            pltpu.VMEM((4, 128, 2 * TQ), F32),
        ],
        compiler_params=_params(),
        name="diff_attn",
    )(lam_params, subln_g.reshape(128, 1), proj, proj, vbt, proj, vbt)


def _rglru_kernel(x_ref, g_ref, win_ref, cw_ref, cb_ref, gw_ref, gb_ref, lam_ref, wout_ref, o_ref,
                  xbuf, a_s, u_s, h_s, p_s, hstate, mstate, mhist, *, tiles_per_seq):
    g = pl.program_id(0)
    x = x_ref[...]
    xn = _rms(x, g_ref[...]).astype(BF16)
    yx = jnp.dot(xn, win_ref[...], preferred_element_type=F32)
    yb = yx[:, :D]
    y_br = 0.5 * yb * (1.0 + jnp.tanh(math.sqrt(2.0 / math.pi) * (yb + 0.044715 * (yb * yb * yb))))

    @pl.when(g == 0)
    def _():
        xbuf[0:8, :] = jnp.zeros((8, D), F32)
        hstate[...] = jnp.zeros(hstate.shape, F32)

    @pl.when(jnp.logical_and(g >= 1, (g - 1) % tiles_per_seq == 0))
    def _():
        xbuf[0:8, :] = mhist[...]
        hstate[...] = mstate[...]

    xbuf[8:, :] = yx[:, D:]
    xc = cb_ref[...] + cw_ref[0:1, :] * xbuf[5:5 + TM, :]
    for j in range(1, CONV_W):
        xc = xc + cw_ref[j:j + 1, :] * xbuf[5 + j:5 + j + TM, :]

    gx, ga = [], []
    for nb in range(LRU_BLOCKS):
        xb = xc[:, nb * LRU_BLOCK_W:(nb + 1) * LRU_BLOCK_W].astype(BF16)
        gx.append(jnp.dot(xb, gw_ref[0, nb], preferred_element_type=F32))
        ga.append(jnp.dot(xb, gw_ref[1, nb], preferred_element_type=F32))
    gate_x = _sigmoid(jnp.concatenate(gx, axis=1) + gb_ref[0:1, :])
    gate_a = _sigmoid(jnp.concatenate(ga, axis=1) + gb_ref[1:2, :])
    lam = lam_ref[...]
    log_sig = jnp.minimum(lam, 0.0) - jnp.log(1.0 + jnp.exp(-jnp.abs(lam)))
    log_a = RG_C * gate_a * log_sig
    a = jnp.exp(log_a)
    a_s[...] = a
    u_s[...] = jnp.sqrt(-jnp.tanh(log_a) * (a * a + 1.0)) * (gate_x * xc)

    seg_rows = TM // SCAN_SEGS

    def step(r, carry):
        hs, ps = carry
        new_h, new_p = [], []
        for s in range(SCAN_SEGS):
            row = pl.ds(s * seg_rows + r, 1)
            a_r = a_s[row, :]
            h_r = a_r * hs[s] + u_s[row, :]
            h_s[row, :] = h_r
            new_h.append(h_r)
            if s > 0:
                p_r = a_r * ps[s - 1]
                p_s[row, :] = p_r
                new_p.append(p_r)
        return tuple(new_h), tuple(new_p)

    zero = jnp.zeros((1, D), F32)
    hs, _ = lax.fori_loop(0, seg_rows, step,
                          ((hstate[...],) + (zero,) * (SCAN_SEGS - 1), (zero + 1.0,) * (SCAN_SEGS - 1)), unroll=4)
    h_in = hs[0]
    for s in range(1, SCAN_SEGS):
        seg = slice(s * seg_rows, (s + 1) * seg_rows)
        h_seg = h_s[seg, :] + p_s[seg, :] * h_in
        h_s[seg, :] = h_seg
        h_in = h_seg[seg_rows - 1:seg_rows, :]
    hstate[...] = h_in

    @pl.when(g == 0)
    def _():
        mstate[...] = h_s[N_META - 1:N_META, :]
        mhist[...] = xbuf[N_META:N_META + 8, :]

    xbuf[0:8, :] = xbuf[TM:TM + 8, :]
    o_ref[...] = x + jnp.dot((h_s[...] * y_br).astype(BF16), wout_ref[...], preferred_element_type=F32)


def _rglru(h, g, w_in, conv_w, conv_b, gate_w, gate_b, lru_lambda, w_out, tiles_per_seq):
    n = h.shape[0]
    nt = n // TM
    tile = lambda s: (jnp.where(s == 0, nt - 1, s - 1), 0)
    const2 = lambda s: (0, 0)
    return pl.pallas_call(
        functools.partial(_rglru_kernel, tiles_per_seq=tiles_per_seq),
        grid=(nt,),
        in_specs=[
            pl.BlockSpec((TM, D), tile),
            _resident((1, D), const2),
            _resident((D, 2 * D), const2),
            _resident((CONV_W, D), const2),
            _resident((1, D), const2),
            _resident((2, LRU_BLOCKS, LRU_BLOCK_W, LRU_BLOCK_W), lambda s: (0, 0, 0, 0)),
            _resident((2, D), const2),
            _resident((1, D), const2),
            _resident((D, D), const2),
        ],
        out_specs=pl.BlockSpec((TM, D), tile),
        out_shape=jax.ShapeDtypeStruct((n, D), F32),
        scratch_shapes=[
            pltpu.VMEM((TM + 8, D), F32),
            pltpu.VMEM((TM, D), F32),
            pltpu.VMEM((TM, D), F32),
            pltpu.VMEM((TM, D), F32),
            pltpu.VMEM((TM, D), F32),
            pltpu.VMEM((1, D), F32),
            pltpu.VMEM((1, D), F32),
            pltpu.VMEM((8, D), F32),
        ],
        compiler_params=_params(),
        name="rglru",
    )(h, g.reshape(1, D), w_in, conv_w, conv_b.reshape(1, D), gate_w, gate_b, lru_lambda.reshape(1, D), w_out)


def _split_attn_w_in(w):
    a = HEAD_W
    o_ki = 4 * a
    o_wi = o_ki + IDX_DIM
    o_qb = o_wi + IDX_HEADS
    qa, ka, va, qi = w[:, 0:a], w[:, a:2 * a], w[:, 2 * a:3 * a], w[:, 3 * a:4 * a]
    ki, wi = w[:, o_ki:o_wi], w[:, o_wi:o_qb]
    qb, kb, vb = w[:, o_qb:o_qb + a], w[:, o_qb + a:o_qb + 2 * a], w[:, o_qb + 2 * a:o_qb + 3 * a]
    w_main = jnp.concatenate([qa, ka, qi, qb, kb, va, vb], axis=1).astype(BF16)
    w_wt = jnp.concatenate([wi.T, jnp.zeros((16 - IDX_HEADS, D), w.dtype)], axis=0).astype(BF16)
    return w_main, ki.astype(BF16), w_wt


def kernel(x, meta_tokens, norm_g, ffn_w_gu, ffn_w_down, attn_w_in, idx_k_ln_g, idx_k_ln_b, diff_lambda, diff_subln_g, attn_w_out, rec_w_in, rec_conv_w, rec_conv_b, rec_gate_w, rec_gate_b, rec_lambda, rec_w_out, final_norm_g):
    batch, seq, _ = x.shape
    depth = norm_g.shape[0]
    n_real = batch * seq
    tail = jnp.concatenate([meta_tokens.astype(x.dtype), jnp.zeros((TM - N_META, D), x.dtype)], axis=0)
    w_gu = ffn_w_gu.astype(BF16)
    w_down = ffn_w_down.astype(BF16)
    h = x.reshape(n_real, D)
    for i in range(depth):
        j = i // 2
        h = _ffn(h, norm_g[i, 0], w_gu, w_down, i, 0, tail=tail if i == 0 else None)
        if i % 2 == 0:
            lam_init = 0.8 - 0.6 * math.exp(-0.3 * i)
            w_main, w_ki, w_wt = _split_attn_w_in(attn_w_in[j])
            proj, ki, wit, vat, vbt = _attn_proj(h, norm_g[i, 1], w_main, w_ki, w_wt, idx_k_ln_g[j], idx_k_ln_b[j])
            oa = _dsa(proj, ki, wit, vat, batch, seq)
            ob = _diff(proj, vbt, diff_lambda[j], diff_subln_g[j], batch, seq, lam_init)
            mixer = (oa, ob, attn_w_out[j].astype(BF16))
        else:
            mixer = None
            h = _rglru(h, norm_g[i, 1], rec_w_in[j].astype(BF16), rec_conv_w[j], rec_conv_b[j],
                       rec_gate_w[j].astype(BF16), rec_gate_b[j], rec_lambda[j], rec_w_out[j].astype(BF16),
                       seq // TM)
        if i == depth - 1:
            h = _ffn(h, norm_g[i, 2], w_gu, w_down, i, 1, mixer=mixer, final_g=final_norm_g, n_rows=n_real)
        else:
            h = _ffn(h, norm_g[i, 2], w_gu, w_down, i, 1, mixer=mixer)
    return h.reshape(batch, seq, D)
```

```python
import functools
import math

import jax
import jax.numpy as jnp
from jax import lax
from jax.experimental import pallas as pl
from jax.experimental.pallas import tpu as pltpu

F32 = jnp.float32
BF16 = jnp.bfloat16
I32 = jnp.int32

D = 1024
N_META = 16
D_FF = 2816
EPS = 1e-6
HEAD_W = 512
IDX_DIM = 64
IDX_HEADS = 8
TOPK_MAX = 256
LRU_BLOCKS = 4
LRU_BLOCK_W = D // LRU_BLOCKS
CONV_W = 4
RG_C = 8.0

TM = 512
TQ = 256
META_KEYS = 16
TK = 512
SCAN_SEGS = 4
MXU_TILE = 256
FF_SPLIT = (D_FF // MXU_TILE + 1) // 2 * MXU_TILE
FF_CHUNKS = ((0, FF_SPLIT), (FF_SPLIT, D_FF))
NEG = -1e30
BIG = 1e30
QK_SCALE = IDX_DIM ** -0.5 * math.log2(math.e)
INT_MIN = -2 ** 31
VMEM_LIMIT = 56 * 1024 * 1024

_NT = (((1,), (1,)), ((), ()))


def _params():
    return pltpu.CompilerParams(dimension_semantics=("arbitrary",), vmem_limit_bytes=VMEM_LIMIT)


def _resident(shape, index_map):
    return pl.BlockSpec(shape, index_map, pipeline_mode=pl.Buffered(1))


def _rms(x, g):
    return x * lax.rsqrt(jnp.mean(x * x, axis=-1, keepdims=True) + EPS) * g


def _sigmoid(x):
    return 0.5 * jnp.tanh(0.5 * x) + 0.5


def _ffn_kernel(*refs, final, n_real_tiles, mixer):
    refs = list(refs)
    o_ref = refs.pop()
    x = refs.pop(0)[...]
    if n_real_tiles is not None:
        x = jnp.where(pl.program_id(0) < n_real_tiles, x, refs.pop(0)[...])
    if mixer:
        oa_ref, ob_ref, wa_ref, wb_ref = refs[:4]
        del refs[:4]
        x = (x + jnp.dot(oa_ref[...], wa_ref[...], preferred_element_type=F32)
             + jnp.dot(ob_ref[...], wb_ref[...], preferred_element_type=F32))
    g_ref, wg_ref, wu_ref, wd_ref = refs[:4]
    xn = _rms(x, g_ref[...]).astype(BF16)
    acc = None
    for lo, hi in FF_CHUNKS:
        sl = slice(lo, hi)
        g = jnp.dot(xn, wg_ref[:, sl], preferred_element_type=F32)
        u = jnp.dot(xn, wu_ref[:, sl], preferred_element_type=F32)
        a = (g * _sigmoid(g) * u).astype(BF16)
        part = jnp.dot(a, wd_ref[sl, :], preferred_element_type=F32)
        acc = part if acc is None else acc + part
    y = x + 0.5 * acc
    if final:
        y = _rms(y, refs[4][...])
    o_ref[...] = y


def _ffn(h, g, w_gu, w_down, layer, half, tail=None, mixer=None, final_g=None, n_rows=None):
    n_real_tiles = None if tail is None else h.shape[0] // TM
    if n_rows is None:
        n_rows = h.shape[0] + (0 if tail is None else TM)
    row_tile = lambda i: (i, 0)
    const = lambda i: (0, 0)
    if tail is None:
        in_specs, args = [pl.BlockSpec((TM, D), row_tile)], [h]
    else:
        in_specs = [pl.BlockSpec((TM, D), lambda i: (jnp.minimum(i, n_real_tiles - 1), 0)), _resident((TM, D), const)]
        args = [h, tail]
    if mixer is not None:
        oa, ob, w_out = mixer
        in_specs += [pl.BlockSpec((TM, HEAD_W), row_tile), pl.BlockSpec((TM, HEAD_W), row_tile),
                     _resident((HEAD_W, D), const), _resident((HEAD_W, D), lambda i: (1, 0))]
        args += [oa, ob, w_out, w_out]
    in_specs += [
        _resident((1, D), const),
        _resident((None, None, D, D_FF), lambda i: (layer, half, 0, 0)),
        _resident((None, None, D, D_FF), lambda i: (layer, half, 0, 1)),
        _resident((None, None, D_FF, D), lambda i: (layer, half, 0, 0)),
    ]
    args += [g.reshape(1, D), w_gu, w_gu, w_down]
    if final_g is not None:
        in_specs.append(_resident((1, D), const))
        args.append(final_g.reshape(1, D))
    return pl.pallas_call(
        functools.partial(_ffn_kernel, final=final_g is not None, n_real_tiles=n_real_tiles,
                          mixer=mixer is not None),
        grid=(n_rows // TM,),
        in_specs=in_specs,
        out_specs=pl.BlockSpec((TM, D), row_tile),
        out_shape=jax.ShapeDtypeStruct((n_rows, D), F32),
        compiler_params=_params(),
        name="ffn",
    )(*args)


N_PROJ = 5 * HEAD_W


def _attn_proj_kernel(x_ref, g_ref, w_ref, wki_ref, wwt_ref, lng_ref, lnb_ref,
                      proj_ref, ki_ref, wit_ref, vat_ref, vbt_ref):
    xn = _rms(x_ref[...], g_ref[...]).astype(BF16)
    y = jnp.dot(xn, w_ref[...], preferred_element_type=F32)
    scale = QK_SCALE
    proj_ref[:, 0:HEAD_W] = (y[:, 0:HEAD_W] * scale).astype(BF16)
    proj_ref[:, HEAD_W:3 * HEAD_W] = y[:, HEAD_W:3 * HEAD_W].astype(BF16)
    proj_ref[:, 3 * HEAD_W:4 * HEAD_W] = (y[:, 3 * HEAD_W:4 * HEAD_W] * scale).astype(BF16)
    proj_ref[:, 4 * HEAD_W:5 * HEAD_W] = y[:, 4 * HEAD_W:5 * HEAD_W].astype(BF16)
    vat_ref[0] = y[:, 5 * HEAD_W:6 * HEAD_W].T.astype(BF16)
    vbt_ref[0] = y[:, 6 * HEAD_W:7 * HEAD_W].T.astype(BF16)
    yk = jnp.dot(xn, wki_ref[...], preferred_element_type=F32)
    mu = jnp.mean(yk, axis=-1, keepdims=True)
    yc = yk - mu
    ln = yc * lax.rsqrt(jnp.mean(yc * yc, axis=-1, keepdims=True) + EPS)
    ki_ref[...] = (ln * lng_ref[...] + lnb_ref[...]).astype(BF16)
    wit = lax.dot_general(wwt_ref[...], xn, _NT, preferred_element_type=F32)
    wit_ref[...] = wit * (IDX_HEADS ** -0.5 * IDX_DIM ** -0.5)


def _attn_proj(h, g, w_main, w_ki, w_wt, ln_g, ln_b):
    n = h.shape[0]
    nt = n // TM
    const = lambda i: (0, 0)
    return pl.pallas_call(
        _attn_proj_kernel,
        grid=(nt,),
        in_specs=[
            pl.BlockSpec((TM, D), lambda i: (i, 0)),
            _resident((1, D), const),
            _resident((D, 7 * HEAD_W), const),
            _resident((D, IDX_DIM), const),
            _resident((16, D), const),
            _resident((1, IDX_DIM), const),
            _resident((1, IDX_DIM), const),
        ],
        out_specs=[
            pl.BlockSpec((TM, N_PROJ), lambda i: (i, 0)),
            pl.BlockSpec((TM, IDX_DIM), lambda i: (i, 0)),
            pl.BlockSpec((16, TM), lambda i: (0, i)),
            pl.BlockSpec((1, HEAD_W, TM), lambda i: (i, 0, 0)),
            pl.BlockSpec((1, HEAD_W, TM), lambda i: (i, 0, 0)),
        ],
        out_shape=[
            jax.ShapeDtypeStruct((n, N_PROJ), BF16),
            jax.ShapeDtypeStruct((n, IDX_DIM), BF16),
            jax.ShapeDtypeStruct((16, n), F32),
            jax.ShapeDtypeStruct((nt, HEAD_W, TM), BF16),
            jax.ShapeDtypeStruct((nt, HEAD_W, TM), BF16),
        ],
        compiler_params=_params(),
        name="attn_proj",
    )(h, g.reshape(1, D), w_main, w_ki, w_wt, ln_g.reshape(1, IDX_DIM), ln_b.reshape(1, IDX_DIM))


def _step_geometry(g, seq):
    nq_seq = seq // TQ
    n_real = pl.num_programs(0) - TM // TQ
    is_real = g < n_real
    i = g % nq_seq
    n_ch = jnp.where(is_real, (i * TQ + TQ + TK - 1) // TK, 0)
    return is_real, i, n_ch


def _stack_pair(q_pair):
    lane = lax.broadcasted_iota(I32, (TQ, 128), 1)
    qf = q_pair.astype(F32)
    top = jnp.where(lane < 64, qf, 0.0).astype(BF16)
    bot = jnp.where(lane >= 64, qf, 0.0).astype(BF16)
    return jnp.concatenate([top, bot], axis=0)


def _meta_bias(is_real):
    j = lax.broadcasted_iota(I32, (META_KEYS, 2 * TQ), 0)
    t = lax.broadcasted_iota(I32, (META_KEYS, 2 * TQ), 1) % TQ
    real = is_real.astype(I32)
    lim = jnp.minimum(real * N_META + (1 - real) * (t + 1), N_META)
    return jnp.where(j < lim, BIG, NEG)


def _bcast_halves(row, n_sub):
    return jnp.concatenate([jnp.broadcast_to(row[:, :TQ], (n_sub, TQ)),
                            jnp.broadcast_to(row[:, TQ:], (n_sub, TQ))], axis=0)


SCORE_LOOKAHEAD = 3
MAX_LAG = 64.0


def _softmax_pv(segments, v_tiles, base):
    issued, out = {}, []
    for h in range(SCORE_LOOKAHEAD):
        issued[h] = [fn() for fn in segments[h]]
    for h in range(4):
        if h + SCORE_LOOKAHEAD < 4:
            issued[h + SCORE_LOOKAHEAD] = [fn() for fn in segments[h + SCORE_LOOKAHEAD]]
        sts = issued.pop(h)
        m_loc = jnp.max(sts[0], axis=0, keepdims=True)
        for st in sts[1:]:
            m_loc = jnp.maximum(m_loc, jnp.max(st, axis=0, keepdims=True))
        m_use = base(h, m_loc)
        l_new, pv = None, None
        for st, vt in zip(sts, v_tiles[h]):
            pc = jnp.exp2(st - m_use)
            s = jnp.sum(pc, axis=0, keepdims=True)
            d = jnp.dot(vt(), pc.astype(BF16), preferred_element_type=F32)
            l_new = s if l_new is None else l_new + s
            pv = d if pv is None else pv + d
        out.append((m_loc, l_new, pv))
    return out


def _attend(segments, v_tiles, m_ref, l_ref, acc_ref, fold, first):
    fold_pv, fold_row = fold
    if first:
        for h, (m_loc, l_new, pv) in enumerate(_softmax_pv(segments, v_tiles, lambda h, m_loc: m_loc)):
            m_ref[h] = m_loc
            l_ref[h] = l_new
            acc_ref[h] = fold_pv(pv)
        return

    def commit(parts, rebased):
        for h, (m_loc, l_new, pv) in enumerate(parts):
            m_new = jnp.maximum(m_ref[h], m_loc)
            alpha = jnp.exp2(m_ref[h] - m_new)
            if rebased:
                l_ref[h] = alpha * l_ref[h] + l_new
                acc_ref[h] = acc_ref[h] * fold_row(alpha) + fold_pv(pv)
            else:
                l_ref[h] = alpha * (l_ref[h] + l_new)
                acc_ref[h] = (acc_ref[h] + fold_pv(pv)) * fold_row(alpha)
            m_ref[h] = m_new

    parts = _softmax_pv(segments, v_tiles, lambda h, m_loc: m_ref[h])
    rise = parts[0][0] - m_ref[0]
    for h in range(1, 4):
        rise = jnp.maximum(rise, parts[h][0] - m_ref[h])
    safe = jnp.max(rise) <= MAX_LAG

    @pl.when(safe)
    def _():
        commit(parts, rebased=False)

    @pl.when(jnp.logical_not(safe))
    def _():
        commit(_softmax_pv(segments, v_tiles, lambda h, m_loc: jnp.maximum(m_ref[h], m_loc)), rebased=True)


GROUP_KEYS = 256
GROUPS_PER_CHUNK = TK // GROUP_KEYS


def _bit_transpose32(words):
    a = list(words)
    j, mask = 16, 0x0000FFFF
    while j:
        k = 0
        while k < 32:
            t = (a[k] ^ lax.shift_right_logical(a[k + j], jnp.int32(j))) & mask
            a[k] = a[k] ^ t
            a[k + j] = a[k + j] ^ lax.shift_left(t, jnp.int32(j))
            k = (k + j + 1) & ~j
        j >>= 1
        mask ^= mask << j
    return a


def _dsa_kernel(qa_ref, qi_ref, wit_ref, ki_ref, ka_ref, vat_ref, km_ref, vmt_ref, o_ref,
                qall_ref, qstk_ref, key_ref, plane_ref, bias_ref, tri_ref, sel_ref, m_ref, l_ref, acc_ref,
                *, seq, top_k):
    g = pl.program_id(0)

    @pl.when(g == 0)
    def _():
        r = lax.broadcasted_iota(I32, (TK, TK), 0)
        c_ = lax.broadcasted_iota(I32, (TK, TK), 1)
        tri_ref[...] = jnp.where(c_ <= r, 1.0, 0.0).astype(BF16)

    is_real, i, n_ch = _step_geometry(g, seq)
    q_pos = i * TQ + lax.broadcasted_iota(I32, (TK, TQ), 1)
    k_off = lax.broadcasted_iota(I32, (TK, TQ), 0)

    qi = qi_ref[...]
    for h in range(IDX_HEADS):
        qall_ref[h * TQ:(h + 1) * TQ, :] = qi[:, h * IDX_DIM:(h + 1) * IDX_DIM]
    for p in range(4):
        qstk_ref[p] = _stack_pair(qa_ref[:, p * 128:(p + 1) * 128])

    wt = wit_ref[...]

    def score_chunk(c, on_diagonal):
        kc = ki_ref[pl.ds(pl.multiple_of(c * TK, TK), TK), :]
        logits = lax.dot_general(kc, qall_ref[...], _NT, preferred_element_type=F32)
        sc = jnp.zeros((TK, TQ), F32)
        for h in range(IDX_HEADS):
            sc = sc + wt[h:h + 1, :] * jnp.maximum(logits[:, h * TQ:(h + 1) * TQ], 0.0)
        sc = jnp.where(sc == 0.0, 0.0, sc)
        bits = pltpu.bitcast(sc, I32)
        key = bits ^ (lax.shift_right_arithmetic(bits, 31) & 0x7FFFFFFF)
        if on_diagonal:
            key = jnp.where(c * TK + k_off <= q_pos, key, INT_MIN)
        key_ref[c] = key
        u = key ^ INT_MIN
        for gi in range(GROUPS_PER_CHUNK):
            base = gi * GROUP_KEYS
            planes = _bit_transpose32([u[base + w * 8:base + (w + 1) * 8, :] for w in range(32)])
            for b in range(32):
                plane_ref[c * GROUPS_PER_CHUNK + gi, b] = planes[b]

    def score_body(c, carry):
        score_chunk(c, on_diagonal=False)
        return carry

    lax.fori_loop(0, n_ch - 1, score_body, 0)
    pl.when(n_ch > 0)(lambda: score_chunk(n_ch - 1, on_diagonal=True))

    n_grp = n_ch * GROUPS_PER_CHUNK

    def radix_select(n_groups):
        alive0 = (jnp.full((8, TQ), -1, I32),) * n_groups

        def bit_body(b, carry):
            alive, k_rem, thr_u = carry
            ones = [a & plane_ref[gq, b] for gq, a in enumerate(alive)]
            cnt8 = lax.population_count(ones[0])
            for o in ones[1:]:
                cnt8 = cnt8 + lax.population_count(o)
            cnt = jnp.sum(cnt8, axis=0, keepdims=True)
            take = cnt >= k_rem
            alive = tuple(jnp.where(take, o, a ^ o) for a, o in zip(alive, ones))
            k_rem = jnp.where(take, k_rem, k_rem - cnt)
            thr_u = thr_u | jnp.where(take, lax.shift_left(jnp.int32(1), 31 - b), 0)
            return alive, k_rem, thr_u

        alive, k_rem, thr_u = lax.fori_loop(
            0, 32, bit_body, (alive0, jnp.full((1, TQ), top_k, I32), jnp.zeros((1, TQ), I32)))
        eq8 = lax.population_count(alive[0])
        for a in alive[1:]:
            eq8 = eq8 + lax.population_count(a)
        sel_ref[0:1, :] = thr_u
        sel_ref[1:2, :] = k_rem
        sel_ref[2:3, :] = jnp.sum(eq8, axis=0, keepdims=True)

    for n_variant in range(GROUPS_PER_CHUNK, plane_ref.shape[0] + 1, GROUPS_PER_CHUNK):
        pl.when(n_grp == n_variant)(functools.partial(radix_select, n_variant))

    thr_u, k_rem, cnt_eq = sel_ref[0:1, :], sel_ref[1:2, :], sel_ref[2:3, :]
    thr = thr_u ^ INT_MIN
    excess = jnp.where(thr_u != 0, cnt_eq - k_rem, 0)
    has_excess = jnp.max(excess) > 0

    @pl.when(jnp.logical_not(has_excess))
    def _():
        thr_eff = jnp.maximum(thr, INT_MIN + 1)

        def cap_body(c, carry):
            bias_ref[c] = jnp.where(key_ref[c] >= thr_eff, BIG, NEG)
            return carry

        lax.fori_loop(0, n_ch, cap_body, 0)

    @pl.when(has_excess)
    def _():
        need = k_rem.astype(F32)

        def cap_body(c, run):
            k = key_ref[c]
            eq = k == thr
            pre = jnp.dot(tri_ref[...], jnp.where(eq, 1.0, 0.0).astype(BF16), preferred_element_type=F32)
            rank = run + pre
            b = jnp.where(k > thr, BIG, jnp.where(eq, jnp.where(rank <= need, BIG, NEG), NEG))
            bias_ref[c] = jnp.where(k == INT_MIN, NEG, b)
            return run + pre[TK - 1:TK, :]

        lax.fori_loop(0, n_ch, cap_body, jnp.zeros((1, TQ), F32))

    fold = (lambda ot: jnp.concatenate([ot[:64, :TQ], ot[64:, TQ:]], axis=0), lambda row: _bcast_halves(row, 64))
    pairs = [slice(p * 128, (p + 1) * 128) for p in range(4)]

    def score(p, k_ref, idx, cap):
        return lambda: jnp.minimum(lax.dot_general(k_ref[idx], qstk_ref[p], _NT, preferred_element_type=F32), cap)

    def tile(ref, idx):
        return lambda: ref[idx]

    mcap = _meta_bias(is_real)
    _attend([[score(p, km_ref, (slice(None), ps), mcap)] for p, ps in enumerate(pairs)],
            [[tile(vmt_ref, (0, ps, slice(0, META_KEYS)))] for ps in pairs],
            m_ref, l_ref, acc_ref, fold, first=True)

    def update(chunks):
        rows = [pl.ds(pl.multiple_of(c * TK, TK), TK) for c in chunks]
        caps = [jnp.concatenate([bias_ref[c]] * 2, axis=1) for c in chunks]
        _attend([[score(p, ka_ref, (row, ps), cap2) for row, cap2 in zip(rows, caps)] for p, ps in enumerate(pairs)],
                [[tile(vat_ref, (c, ps, slice(None))) for c in chunks] for ps in pairs],
                m_ref, l_ref, acc_ref, fold, first=False)

    def pair_body(j, carry):
        update([2 * j, 2 * j + 1])
        return carry

    lax.fori_loop(0, n_ch // 2, pair_body, 0)

    @pl.when(n_ch % 2 == 1)
    def _():
        update([n_ch - 1])

    for p, ps in enumerate(pairs):
        o_ref[:, ps] = (acc_ref[p] * _bcast_halves(1.0 / l_ref[p], 64)).T.astype(BF16)


def _dsa(proj, ki, wit, vat, batch, seq):
    n = proj.shape[0]
    nq_seq = seq // TQ
    n_real_q = batch * nq_seq
    nch = seq // TK
    top_k = min(TOPK_MAX, seq // 4)
    bidx = lambda g: jnp.minimum(g // nq_seq, batch - 1)
    meta_k = batch * seq // META_KEYS
    meta_tile = batch * nch
    return pl.pallas_call(
        functools.partial(_dsa_kernel, seq=seq, top_k=top_k),
        grid=(n // TQ,),
        in_specs=[
            pl.BlockSpec((TQ, HEAD_W), lambda g: (g, 0)),
            pl.BlockSpec((TQ, HEAD_W), lambda g: (g, 2)),
            pl.BlockSpec((16, TQ), lambda g: (0, g)),
            pl.BlockSpec((seq, IDX_DIM), lambda g: (bidx(g), 0)),
            pl.BlockSpec((seq, HEAD_W), lambda g: (bidx(g), 1)),
            pl.BlockSpec((nch, HEAD_W, TK), lambda g: (bidx(g), 0, 0)),
            pl.BlockSpec((META_KEYS, HEAD_W), lambda g: (meta_k, 1)),
            pl.BlockSpec((1, HEAD_W, TK), lambda g: (meta_tile, 0, 0)),
        ],
        out_specs=pl.BlockSpec((TQ, HEAD_W), lambda g: (g, 0)),
        out_shape=jax.ShapeDtypeStruct((n, HEAD_W), BF16),
        scratch_shapes=[
            pltpu.VMEM((IDX_HEADS * TQ, IDX_DIM), BF16),
            pltpu.VMEM((4, 2 * TQ, 128), BF16),
            pltpu.VMEM((nch, TK, TQ), I32),
            pltpu.VMEM((nch * GROUPS_PER_CHUNK, 32, 8, TQ), I32),
            pltpu.VMEM((nch, TK, TQ), F32),
            pltpu.VMEM((TK, TK), BF16),
            pltpu.VMEM((8, TQ), I32),
            pltpu.VMEM((4, 1, 2 * TQ), F32),
            pltpu.VMEM((4, 1, 2 * TQ), F32),
            pltpu.VMEM((4, 128, TQ), F32),
        ],
        compiler_params=_params(),
        name="dsa",
    )(proj, proj, wit, ki, proj, vat, proj, vat)


def _diff_kernel(lam_ref, g_ref, qb_ref, kb_ref, vbt_ref, km_ref, vmt_ref, o_ref,
                 qstk_ref, m_ref, l_ref, acc_ref, *, seq, lam_init):
    g = pl.program_id(0)
    is_real, i, n_ch = _step_geometry(g, seq)
    lp = lam_ref[...]
    lam = (jnp.exp(jnp.sum(lp[0:1] * lp[1:2], axis=1, keepdims=True))
           - jnp.exp(jnp.sum(lp[2:3] * lp[3:4], axis=1, keepdims=True)) + lam_init)
    heads = [slice(h * 128, (h + 1) * 128) for h in range(4)]
    for h, hs in enumerate(heads):
        qstk_ref[h] = _stack_pair(qb_ref[:, hs])

    fold = (lambda ot: ot, lambda row: row)

    def score(h, k_ref, idx, cap=None):
        def fn():
            st = lax.dot_general(k_ref[idx], qstk_ref[h], _NT, preferred_element_type=F32)
            return st if cap is None else jnp.minimum(st, cap)
        return fn

    def tile(ref, idx):
        return lambda: ref[idx]

    n_full = n_ch - is_real.astype(I32)
    mcap = _meta_bias(is_real)
    _attend([[score(h, km_ref, (slice(None), hs), mcap)] for h, hs in enumerate(heads)],
            [[tile(vmt_ref, (0, hs, slice(0, META_KEYS)))] for hs in heads],
            m_ref, l_ref, acc_ref, fold, first=True)

    def update(chunks, caps):
        rows = [pl.ds(pl.multiple_of(c * TK, TK), TK) for c in chunks]
        _attend([[score(h, kb_ref, (row, hs), cap) for row, cap in zip(rows, caps)] for h, hs in enumerate(heads)],
                [[tile(vbt_ref, (c, hs, slice(None))) for c in chunks] for hs in heads],
                m_ref, l_ref, acc_ref, fold, first=False)

    def pair_body(j, carry):
        update([2 * j, 2 * j + 1], [None, None])
        return carry

    lax.fori_loop(0, n_full // 2, pair_body, 0)

    q_pos = i * TQ + lax.broadcasted_iota(I32, (TK, 2 * TQ), 1) % TQ
    k_pos = n_full * TK + lax.broadcasted_iota(I32, (TK, 2 * TQ), 0)
    cap = jnp.where(jnp.where(k_pos <= q_pos, is_real.astype(I32), 0) > 0, BIG, NEG)

    @pl.when(n_full % 2 == 1)
    def _():
        update([n_full - 1, n_full], [None, cap])

    @pl.when(n_full % 2 == 0)
    def _():
        update([n_full], [cap])

    for h, hs in enumerate(heads):
        on = acc_ref[h] * (1.0 / l_ref[h])
        o = on[:, :TQ] - lam * on[:, TQ:]
        y = o * lax.rsqrt(jnp.mean(o * o, axis=0, keepdims=True) + EPS) * g_ref[...] * (1.0 - lam_init)
        o_ref[:, hs] = y.T.astype(BF16)


def _diff(proj, vbt, lam_params, subln_g, batch, seq, lam_init):
    n = proj.shape[0]
    nq_seq = seq // TQ
    nch = seq // TK
    bidx = lambda g: jnp.minimum(g // nq_seq, batch - 1)
    meta_k = batch * seq // META_KEYS
    meta_tile = batch * nch
    return pl.pallas_call(
        functools.partial(_diff_kernel, seq=seq, lam_init=lam_init),
        grid=(n // TQ,),
        in_specs=[
            _resident((4, 64), lambda g: (0, 0)),
            _resident((128, 1), lambda g: (0, 0)),
            pl.BlockSpec((TQ, HEAD_W), lambda g: (g, 3)),
            pl.BlockSpec((seq, HEAD_W), lambda g: (bidx(g), 4)),
            pl.BlockSpec((nch, HEAD_W, TK), lambda g: (bidx(g), 0, 0)),
            pl.BlockSpec((META_KEYS, HEAD_W), lambda g: (meta_k, 4)),
            pl.BlockSpec((1, HEAD_W, TK), lambda g: (meta_tile, 0, 0)),
        ],
        out_specs=pl.BlockSpec((TQ, HEAD_W), lambda g: (g, 0)),
        out_shape=jax.ShapeDtypeStruct((n, HEAD_W), BF16),
        scratch_shapes=[
            pltpu.VMEM((4, 2 * TQ, 128), BF16),
            pltpu.VMEM((4, 1, 2 * TQ), F32),
            pltpu.VMEM((4, 1, 2 * TQ), F32),
            pltpu.VMEM((4, 128, 2 * TQ), F32),
        ],
        compiler_params=_params(),
        name="diff_attn",
    )(lam_params, subln_g.reshape(128, 1), proj, proj, vbt, proj, vbt)


def _rglru_kernel(x_ref, g_ref, win_ref, cw_ref, cb_ref, gw_ref, gb_ref, lam_ref, wout_ref, o_ref,
                  xbuf, a_s, u_s, h_s, p_s, hstate, mstate, mhist, *, tiles_per_seq):
    g = pl.program_id(0)
    x = x_ref[...]
    xn = _rms(x, g_ref[...]).astype(BF16)
    yx = jnp.dot(xn, win_ref[...], preferred_element_type=F32)
    yb = yx[:, :D]
    y_br = 0.5 * yb * (1.0 + jnp.tanh(math.sqrt(2.0 / math.pi) * (yb + 0.044715 * (yb * yb * yb))))

    @pl.when(g == 0)
    def _():
        xbuf[0:8, :] = jnp.zeros((8, D), F32)
        hstate[...] = jnp.zeros(hstate.shape, F32)

    @pl.when(jnp.logical_and(g >= 1, (g - 1) % tiles_per_seq == 0))
    def _():
        xbuf[0:8, :] = mhist[...]
        hstate[...] = mstate[...]

    xbuf[8:, :] = yx[:, D:]
    xc = cb_ref[...] + cw_ref[0:1, :] * xbuf[5:5 + TM, :]
    for j in range(1, CONV_W):
        xc = xc + cw_ref[j:j + 1, :] * xbuf[5 + j:5 + j + TM, :]

    gx, ga = [], []
    for nb in range(LRU_BLOCKS):
        xb = xc[:, nb * LRU_BLOCK_W:(nb + 1) * LRU_BLOCK_W].astype(BF16)
        gx.append(jnp.dot(xb, gw_ref[0, nb], preferred_element_type=F32))
        ga.append(jnp.dot(xb, gw_ref[1, nb], preferred_element_type=F32))
    gate_x = _sigmoid(jnp.concatenate(gx, axis=1) + gb_ref[0:1, :])
    gate_a = _sigmoid(jnp.concatenate(ga, axis=1) + gb_ref[1:2, :])
    lam = lam_ref[...]
    log_sig = jnp.minimum(lam, 0.0) - jnp.log(1.0 + jnp.exp(-jnp.abs(lam)))
    log_a = RG_C * gate_a * log_sig
    a = jnp.exp(log_a)
    a_s[...] = a
    u_s[...] = jnp.sqrt(-jnp.tanh(log_a) * (a * a + 1.0)) * (gate_x * xc)

    seg_rows = TM // SCAN_SEGS

    def step(r, carry):
        hs, ps = carry
        new_h, new_p = [], []
        for s in range(SCAN_SEGS):
            row = pl.ds(s * seg_rows + r, 1)
            a_r = a_s[row, :]
            h_r = a_r * hs[s] + u_s[row, :]
            h_s[row, :] = h_r
            new_h.append(h_r)
            if s > 0:
                p_r = a_r * ps[s - 1]
                p_s[row, :] = p_r
                new_p.append(p_r)
        return tuple(new_h), tuple(new_p)

    zero = jnp.zeros((1, D), F32)
    hs, _ = lax.fori_loop(0, seg_rows, step,
                          ((hstate[...],) + (zero,) * (SCAN_SEGS - 1), (zero + 1.0,) * (SCAN_SEGS - 1)), unroll=4)
    h_in = hs[0]
    for s in range(1, SCAN_SEGS):
        seg = slice(s * seg_rows, (s + 1) * seg_rows)
        h_seg = h_s[seg, :] + p_s[seg, :] * h_in
        h_s[seg, :] = h_seg
        h_in = h_seg[seg_rows - 1:seg_rows, :]
    hstate[...] = h_in

    @pl.when(g == 0)
    def _():
        mstate[...] = h_s[N_META - 1:N_META, :]
        mhist[...] = xbuf[N_META:N_META + 8, :]

    xbuf[0:8, :] = xbuf[TM:TM + 8, :]
    o_ref[...] = x + jnp.dot((h_s[...] * y_br).astype(BF16), wout_ref[...], preferred_element_type=F32)


def _rglru(h, g, w_in, conv_w, conv_b, gate_w, gate_b, lru_lambda, w_out, tiles_per_seq):
    n = h.shape[0]
    nt = n // TM
    tile = lambda s: (jnp.where(s == 0, nt - 1, s - 1), 0)
    const2 = lambda s: (0, 0)
    return pl.pallas_call(
        functools.partial(_rglru_kernel, tiles_per_seq=tiles_per_seq),
        grid=(nt,),
        in_specs=[
            pl.BlockSpec((TM, D), tile),
            _resident((1, D), const2),
            _resident((D, 2 * D), const2),
            _resident((CONV_W, D), const2),
            _resident((1, D), const2),
            _resident((2, LRU_BLOCKS, LRU_BLOCK_W, LRU_BLOCK_W), lambda s: (0, 0, 0, 0)),
            _resident((2, D), const2),
            _resident((1, D), const2),
            _resident((D, D), const2),
        ],
        out_specs=pl.BlockSpec((TM, D), tile),
        out_shape=jax.ShapeDtypeStruct((n, D), F32),
        scratch_shapes=[
            pltpu.VMEM((TM + 8, D), F32),
            pltpu.VMEM((TM, D), F32),
            pltpu.VMEM((TM, D), F32),
            pltpu.VMEM((TM, D), F32),
            pltpu.VMEM((TM, D), F32),
            pltpu.VMEM((1, D), F32),
            pltpu.VMEM((1, D), F32),
            pltpu.VMEM((8, D), F32),
        ],
        compiler_params=_params(),
        name="rglru",
    )(h, g.reshape(1, D), w_in, conv_w, conv_b.reshape(1, D), gate_w, gate_b, lru_lambda.reshape(1, D), w_out)


def _split_attn_w_in(w):
    a = HEAD_W
    o_ki = 4 * a
    o_wi = o_ki + IDX_DIM
    o_qb = o_wi + IDX_HEADS
    qa, ka, va, qi = w[:, 0:a], w[:, a:2 * a], w[:, 2 * a:3 * a], w[:, 3 * a:4 * a]
    ki, wi = w[:, o_ki:o_wi], w[:, o_wi:o_qb]
    qb, kb, vb = w[:, o_qb:o_qb + a], w[:, o_qb + a:o_qb + 2 * a], w[:, o_qb + 2 * a:o_qb + 3 * a]
    w_main = jnp.concatenate([qa, ka, qi, qb, kb, va, vb], axis=1).astype(BF16)
    w_wt = jnp.concatenate([wi.T, jnp.zeros((16 - IDX_HEADS, D), w.dtype)], axis=0).astype(BF16)
    return w_main, ki.astype(BF16), w_wt


def kernel(x, meta_tokens, norm_g, ffn_w_gu, ffn_w_down, attn_w_in, idx_k_ln_g, idx_k_ln_b, diff_lambda, diff_subln_g, attn_w_out, rec_w_in, rec_conv_w, rec_conv_b, rec_gate_w, rec_gate_b, rec_lambda, rec_w_out, final_norm_g):
    batch, seq, _ = x.shape
    depth = norm_g.shape[0]
    n_real = batch * seq
    tail = jnp.concatenate([meta_tokens.astype(x.dtype), jnp.zeros((TM - N_META, D), x.dtype)], axis=0)
    w_gu = ffn_w_gu.astype(BF16)
    w_down = ffn_w_down.astype(BF16)
    h = x.reshape(n_real, D)
    for i in range(depth):
        j = i // 2
        h = _ffn(h, norm_g[i, 0], w_gu, w_down, i, 0, tail=tail if i == 0 else None)
        if i % 2 == 0:
            lam_init = 0.8 - 0.6 * math.exp(-0.3 * i)
            w_main, w_ki, w_wt = _split_attn_w_in(attn_w_in[j])
            proj, ki, wit, vat, vbt = _attn_proj(h, norm_g[i, 1], w_main, w_ki, w_wt, idx_k_ln_g[j], idx_k_ln_b[j])
            oa = _dsa(proj, ki, wit, vat, batch, seq)
            ob = _diff(proj, vbt, diff_lambda[j], diff_subln_g[j], batch, seq, lam_init)
            mixer = (oa, ob, attn_w_out[j].astype(BF16))
        else:
            mixer = None
            h = _rglru(h, norm_g[i, 1], rec_w_in[j].astype(BF16), rec_conv_w[j], rec_conv_b[j],
                       rec_gate_w[j].astype(BF16), rec_gate_b[j], rec_lambda[j], rec_w_out[j].astype(BF16),
                       seq // TM)
        if i == depth - 1:
            h = _ffn(h, norm_g[i, 2], w_gu, w_down, i, 1, mixer=mixer, final_g=final_norm_g, n_rows=n_real)
        else:
            h = _ffn(h, norm_g[i, 2], w_gu, w_down, i, 1, mixer=mixer)
    return h.reshape(batch, seq, D)
```
